```python
import math, functools
import jax, jax.numpy as jnp
from jax import lax
import numpy as np

D_MODEL = 1024
BATCH = 4
SEQ = 4096
DEPTH = 2
DEC_BATCH = 128
DEC_SEQ = 4
PAST_LEN = 2048
PAGE_SIZE = 128

D_RNN = D_MODEL
RNN_BLOCKS = 16
RNN_BLOCK_W = D_RNN // RNN_BLOCKS
CONV_W = 4
LRU_C = 8.0
N_HEADS = 8
HEAD_DIM = 64
ATT_W = N_HEADS * HEAD_DIM
IDX_HEADS = 8
IDX_DIM = 64
TOPK_MAX = 256
QBLOCK = 128
N_BUCKETS = 32
MAX_DISTANCE = 128
D_FF = ((-(-8 * D_MODEL // 3) + 255) // 256) * 256
P_DIM = 256
EPS = 1e-6
IN_WIDTHS = (D_RNN, D_RNN, ATT_W, ATT_W, ATT_W, IDX_HEADS * IDX_DIM, IDX_DIM, IDX_HEADS, D_MODEL, D_MODEL)
D_IN = sum(IN_WIDTHS)

kernel_name = 'hybrid_rglru_dsa_decoder_step'


def _split_points():
    return np.cumsum(np.array(IN_WIDTHS))[:-1].tolist()


def _rmsnorm(x, g):
    xf = x.astype(jnp.float32)
    y = xf * lax.rsqrt(jnp.mean(xf * xf, axis=-1, keepdims=True) + EPS)
    return (y * g.astype(jnp.float32)).astype(x.dtype)


def _causal_conv(x, buf, w, b):
    T = x.shape[1]
    xx = jnp.concatenate([buf.astype(x.dtype), x], axis=1)
    y = b + sum(xx[:, j:j + T] * w[j] for j in range(CONV_W))
    return y, xx[:, T:]


def _rg_lru(x, h0, w_a, b_a, w_x, b_x, lam):
    B, T, _ = x.shape
    xf = x.astype(jnp.float32)
    xb = xf.reshape(B, T, RNN_BLOCKS, RNN_BLOCK_W)
    r = jax.nn.sigmoid(jnp.einsum('btnc,ncd->btnd', xb, w_a.astype(jnp.float32)).reshape(B, T, D_RNN) + b_a.astype(jnp.float32))
    i = jax.nn.sigmoid(jnp.einsum('btnc,ncd->btnd', xb, w_x.astype(jnp.float32)).reshape(B, T, D_RNN) + b_x.astype(jnp.float32))
    log_a = LRU_C * r * jax.nn.log_sigmoid(lam.astype(jnp.float32))
    a = jnp.exp(log_a)
    u = jnp.sqrt(-jnp.expm1(2.0 * log_a)) * (i * xf)

    def step(h, au):
        h = au[0] * h + au[1]
        return h, h

    h_last, hs = lax.scan(step, h0.astype(jnp.float32), (a.swapaxes(0, 1), u.swapaxes(0, 1)))
    return hs.swapaxes(0, 1).astype(x.dtype), h_last


def _rel_bucket(n):
    max_exact = N_BUCKETS // 2
    nf = jnp.maximum(n, 1).astype(jnp.float32)
    large = max_exact + (jnp.log(nf / max_exact) / math.log(MAX_DISTANCE / max_exact) * (N_BUCKETS - max_exact)).astype(jnp.int32)
    large = jnp.minimum(large, N_BUCKETS - 1)
    return jnp.where(n < max_exact, n, large)


def _index_select(iq, iw, ik, q_pos, k_top):
    dots = jnp.einsum('bthd,bsd->bths', iq, ik).astype(jnp.float32) * IDX_DIM ** -0.5
    score = jnp.einsum('bths,bth->bts', jax.nn.relu(dots), iw.astype(jnp.float32))
    L = ik.shape[1]
    causal = jnp.arange(L, dtype=jnp.int32)[None, :] <= q_pos[:, None]
    score = jnp.where(causal[None], score, -jnp.inf)
    _, sel = lax.top_k(score, k_top)
    valid = sel <= q_pos[None, :, None]
    return sel, valid


def _sparse_attend(q, k_sel, v_sel, sel, valid, q_pos, rel_bias):
    logits = jnp.einsum('bthd,btkhd->bthk', q, k_sel).astype(jnp.float32) * HEAD_DIM ** -0.5
    bucket = _rel_bucket(jnp.maximum(q_pos[None, :, None] - sel, 0))
    bias = rel_bias.astype(jnp.float32)[bucket]
    logits = logits + bias.transpose(0, 1, 3, 2)
    logits = jnp.where(valid[:, :, None, :], logits, -jnp.inf)
    p = jax.nn.softmax(logits, axis=-1)
    return jnp.einsum('bthk,btkhd->bthd', p.astype(v_sel.dtype), v_sel)


def _take_rows(rows, idx):
    return jax.vmap(lambda r, i: r[i])(rows, idx)


def _attend_prompt(q, k, v, iq, ik, iw, *, rel_bias):
    B, T = q.shape[:2]
    k_top = min(TOPK_MAX, T // 4)
    nb = T // QBLOCK

    def to_blocks(a):
        return a.reshape((B, nb, QBLOCK) + a.shape[2:]).swapaxes(0, 1)

    def blk(args):
        qb, iqb, iwb, pos = args
        sel, valid = _index_select(iqb, iwb, ik, pos, k_top)
        return _sparse_attend(qb, _take_rows(k, sel), _take_rows(v, sel), sel, valid, pos, rel_bias)

    pos = jnp.arange(T, dtype=jnp.int32).reshape(nb, QBLOCK)
    out = lax.map(blk, (to_blocks(q), to_blocks(iq), to_blocks(iw), pos))
    return out.swapaxes(0, 1).reshape(q.shape)


def _gather_paged(pool, page_table, sel, new_rows):
    B = sel.shape[0]
    past_len = page_table.shape[1] * PAGE_SIZE
    is_past = sel < past_len
    sp = jnp.minimum(sel, past_len - 1)
    phys = jnp.take_along_axis(page_table, (sp // PAGE_SIZE).reshape(B, -1), axis=1).reshape(sel.shape)
    from_pool = pool[phys, sp % PAGE_SIZE].astype(new_rows.dtype)
    from_new = _take_rows(new_rows, jnp.clip(sel - past_len, 0, new_rows.shape[1] - 1))
    mask = is_past.reshape(is_past.shape + (1,) * (from_pool.ndim - 3))
    return jnp.where(mask, from_pool, from_new)


def _attend_sample(q, k, v, iq, ik, iw, *, rel_bias, k_pool, v_pool, ik_pool, page_table):
    B, T = q.shape[:2]
    past_len = page_table.shape[1] * PAGE_SIZE
    k_top = min(TOPK_MAX, (past_len + T) // 4)
    ik_past = ik_pool[page_table].reshape(B, past_len, IDX_DIM).astype(ik.dtype)
    ik_all = jnp.concatenate([ik_past, ik], axis=1)
    pos = past_len + jnp.arange(T, dtype=jnp.int32)
    sel, valid = _index_select(iq, iw, ik_all, pos, k_top)
    k_sel = _gather_paged(k_pool, page_table, sel, k)
    v_sel = _gather_paged(v_pool, page_table, sel, v)
    return _sparse_attend(q, k_sel, v_sel, sel, valid, pos, rel_bias)


def _layer(x, p_l, conv_buf, h0, attend, lw):
    (norm_mix, w_in, conv_w, conv_b, w_rg_a, b_rg_a, w_rg_x, b_rg_x, lam,
     w_rnn_out, w_att_out, w_o, norm_ffn, w_ffn_gate, w_ffn_up, w_ffn_down,
     norm_ple, w_ple_gate, w_ple_proj) = lw
    B, T, _ = x.shape
    h = _rmsnorm(x, norm_mix)
    z = h @ w_in
    xr, gr, q, k, v, iq, ik, iw, ga, gb = jnp.split(z, _split_points(), axis=-1)
    xc, new_buf = _causal_conv(xr, conv_buf, conv_w, conv_b)
    yr, h_last = _rg_lru(xc, h0, w_rg_a, b_rg_a, w_rg_x, b_rg_x, lam)
    branch_a = (yr * jax.nn.gelu(gr)) @ w_rnn_out
    q = q.reshape(B, T, N_HEADS, HEAD_DIM)
    k = k.reshape(B, T, N_HEADS, HEAD_DIM)
    v = v.reshape(B, T, N_HEADS, HEAD_DIM)
    iq = iq.reshape(B, T, IDX_HEADS, IDX_DIM)
    iw = iw * IDX_HEADS ** -0.5
    o = attend(q, k, v, iq, ik, iw)
    branch_b = o.reshape(B, T, ATT_W) @ w_att_out
    x = x + (jax.nn.sigmoid(ga) * branch_a + jax.nn.sigmoid(gb) * branch_b) @ w_o
    h2 = _rmsnorm(x, norm_ffn)
    x = x + (jax.nn.silu(h2 @ w_ffn_gate) * (h2 @ w_ffn_up)) @ w_ffn_down
    h3 = _rmsnorm(x, norm_ple)
    x = x + jax.nn.sigmoid(h3 @ w_ple_gate) * (p_l @ w_ple_proj)
    return x, (k, v, ik, h_last, new_buf)


def setup_inputs(seed: int = 0) -> dict:
    key = jax.random.key(seed)
    ks = jax.random.split(key, 32)

    def nrm(i, shape, scale):
        return scale * jax.random.normal(ks[i], shape, jnp.float32)

    n_pages = PAST_LEN // PAGE_SIZE
    n_used = DEC_BATCH * n_pages
    n_pool = (5 * n_used + 3) // 4
    perm = jax.random.permutation(ks[9], n_pool)
    page_table = perm[:n_used].reshape(DEC_BATCH, n_pages).astype(jnp.int32)
    a_pow = jax.random.uniform(ks[19], (DEPTH, D_RNN), jnp.float32, 0.9, 0.999)
    a_base = a_pow ** (1.0 / LRU_C)
    lru_lambda = jnp.log(a_base) - jnp.log1p(-a_base)
    return {
        'x_prompt': nrm(0, (BATCH, SEQ, D_MODEL), 1.0),
        'x_sample': nrm(1, (DEC_BATCH, DEC_SEQ, D_MODEL), 1.0),
        'p_prompt': nrm(2, (DEPTH, BATCH, SEQ, P_DIM), 1.0),
        'p_sample': nrm(3, (DEPTH, DEC_BATCH, DEC_SEQ, P_DIM), 1.0),
        'cache_k': nrm(4, (DEPTH, n_pool, PAGE_SIZE, N_HEADS, HEAD_DIM), 1.0),
        'cache_v': nrm(5, (DEPTH, n_pool, PAGE_SIZE, N_HEADS, HEAD_DIM), 1.0),
        'cache_idx_k': nrm(6, (DEPTH, n_pool, PAGE_SIZE, IDX_DIM), 1.0),
        'state_rglru_h': nrm(7, (DEPTH, DEC_BATCH, D_RNN), 0.5),
        'state_conv': nrm(8, (DEPTH, DEC_BATCH, CONV_W - 1, D_RNN), 1.0),
        'page_table': page_table,
        'rel_bias': nrm(10, (N_BUCKETS, N_HEADS), 0.5),
        'norm_mix': 1.0 + nrm(11, (DEPTH, D_MODEL), 0.05),
        'w_in': nrm(12, (DEPTH, D_MODEL, D_IN), D_MODEL ** -0.5),
        'conv_w': nrm(13, (DEPTH, CONV_W, D_RNN), CONV_W ** -0.5),
        'conv_b': nrm(14, (DEPTH, D_RNN), 0.01),
        'w_rg_a': nrm(15, (DEPTH, RNN_BLOCKS, RNN_BLOCK_W, RNN_BLOCK_W), RNN_BLOCK_W ** -0.5),
        'b_rg_a': nrm(16, (DEPTH, D_RNN), 0.01),
        'w_rg_x': nrm(17, (DEPTH, RNN_BLOCKS, RNN_BLOCK_W, RNN_BLOCK_W), RNN_BLOCK_W ** -0.5),
        'b_rg_x': nrm(18, (DEPTH, D_RNN), 0.01),
        'lru_lambda': lru_lambda,
        'w_rnn_out': nrm(20, (DEPTH, D_RNN, D_MODEL), D_RNN ** -0.5),
        'w_att_out': nrm(21, (DEPTH, ATT_W, D_MODEL), ATT_W ** -0.5),
        'w_o': nrm(22, (DEPTH, D_MODEL, D_MODEL), D_MODEL ** -0.5),
        'norm_ffn': 1.0 + nrm(23, (DEPTH, D_MODEL), 0.05),
        'w_ffn_gate': nrm(24, (DEPTH, D_MODEL, D_FF), D_MODEL ** -0.5),
        'w_ffn_up': nrm(25, (DEPTH, D_MODEL, D_FF), D_MODEL ** -0.5),
        'w_ffn_down': nrm(26, (DEPTH, D_FF, D_MODEL), D_FF ** -0.5),
        'norm_ple': 1.0 + nrm(27, (DEPTH, D_MODEL), 0.05),
        'w_ple_gate': nrm(28, (DEPTH, D_MODEL, D_MODEL), D_MODEL ** -0.5),
        'w_ple_proj': nrm(29, (DEPTH, P_DIM, D_MODEL), P_DIM ** -0.5),
        'norm_final': 1.0 + nrm(30, (D_MODEL,), 0.05),
    }


def reference(x_prompt, x_sample, p_prompt, p_sample, cache_k, cache_v, cache_idx_k,
              state_rglru_h, state_conv, page_table, rel_bias, norm_mix, w_in, conv_w, conv_b,
              w_rg_a, b_rg_a, w_rg_x, b_rg_x, lru_lambda, w_rnn_out, w_att_out, w_o,
              norm_ffn, w_ffn_gate, w_ffn_up, w_ffn_down, norm_ple, w_ple_gate, w_ple_proj,
              norm_final):
    xp, xs = x_prompt, x_sample
    bp = xp.shape[0]
    attend_p = functools.partial(_attend_prompt, rel_bias=rel_bias)
    new_p, new_s = [], []
    for l in range(DEPTH):
        lw = (norm_mix[l], w_in[l], conv_w[l], conv_b[l], w_rg_a[l], b_rg_a[l], w_rg_x[l],
              b_rg_x[l], lru_lambda[l], w_rnn_out[l], w_att_out[l], w_o[l], norm_ffn[l],
              w_ffn_gate[l], w_ffn_up[l], w_ffn_down[l], norm_ple[l], w_ple_gate[l], w_ple_proj[l])
        zero_buf = jnp.zeros((bp, CONV_W - 1, D_RNN), xp.dtype)
        zero_h = jnp.zeros((bp, D_RNN), jnp.float32)
        xp, sp = _layer(xp, p_prompt[l], zero_buf, zero_h, attend_p, lw)
        attend_s = functools.partial(_attend_sample, rel_bias=rel_bias, k_pool=cache_k[l],
                                     v_pool=cache_v[l], ik_pool=cache_idx_k[l], page_table=page_table)
        xs, ss = _layer(xs, p_sample[l], state_conv[l], state_rglru_h[l], attend_s, lw)
        new_p.append(sp)
        new_s.append(ss)
    y_prompt = _rmsnorm(xp, norm_final)
    y_sample = _rmsnorm(xs, norm_final)
    k_prompt = jnp.stack([s[0] for s in new_p])
    v_prompt = jnp.stack([s[1] for s in new_p])
    idx_k_prompt = jnp.stack([s[2] for s in new_p])
    h_prompt = jnp.stack([s[3] for s in new_p])
    conv_prompt = jnp.stack([s[4] for s in new_p])
    k_sample = jnp.stack([s[0] for s in new_s])
    v_sample = jnp.stack([s[1] for s in new_s])
    idx_k_sample = jnp.stack([s[2] for s in new_s])
    h_sample = jnp.stack([s[3] for s in new_s])
    conv_sample = jnp.stack([s[4] for s in new_s])
    return (y_prompt, y_sample, k_prompt, v_prompt, idx_k_prompt, h_prompt, conv_prompt,
            k_sample, v_sample, idx_k_sample, h_sample, conv_sample)
```

```python
import functools
import math

import jax
import jax.numpy as jnp
from jax import lax
from jax.experimental import pallas as pl
from jax.experimental.pallas import tpu as pltpu

F32 = jnp.float32
BF16 = jnp.bfloat16
I32 = jnp.int32

N_HEADS = 8
HEAD_DIM = 64
ATT_W = N_HEADS * HEAD_DIM
IDX_HEADS = 8
IDX_DIM = 64
TOPK_MAX = 256
LRU_C = 8.0
MAX_DISTANCE = 128
EPS = 1e-6

LANES = 128
TQ = 128
TK = 256
NEG = -1e30
INT_MIN = -2147483648
VMEM_LIMIT = 56 * 1024 * 1024


def _dot(a, b):
    return jnp.dot(a, b, preferred_element_type=F32)


def _dot_nt(a, b):
    return lax.dot_general(a, b, (((1,), (1,)), ((), ())), preferred_element_type=F32)


def _sigmoid(x):
    return 1.0 / (1.0 + jnp.exp(-x))


def _gelu_tanh(x):
    c = math.sqrt(2.0 / math.pi)
    return 0.5 * x * (1.0 + jnp.tanh(c * (x + 0.044715 * (x * x * x))))


def _rms(x, g):
    y = x * lax.rsqrt(jnp.mean(x * x, axis=-1, keepdims=True) + EPS)
    return y * g


def _params(n_axes=1):
    return pltpu.CompilerParams(dimension_semantics=("arbitrary",) * n_axes,
                                vmem_limit_bytes=VMEM_LIMIT)


def _const_spec(shape):
    nd = len(shape)
    return pl.BlockSpec(shape, lambda *_: (0,) * nd)


def _inproj_p_kernel(x_ref, g_ref, wm_ref, wt_ref,
                     xr_ref, gr_ref, ga_ref, gb_ref, k_ref, v_ref, ikw_ref, kb_ref, ikwb_ref,
                     qT_ref, iqT_ref, vT_ref, iwT_ref):
    d = x_ref.shape[1]
    hb = _rms(x_ref[...], g_ref[...]).astype(BF16)
    z = _dot(hb, wm_ref[...])
    xr_ref[...] = z[:, 0:d]
    gr_ref[...] = z[:, d:2 * d]
    ga_ref[...] = z[:, 2 * d:3 * d]
    gb_ref[...] = z[:, 3 * d:4 * d]
    o = 4 * d
    k = z[:, o:o + ATT_W]
    k_ref[...] = k
    kb_ref[...] = k.astype(BF16)
    v_ref[...] = z[:, o + ATT_W:o + 2 * ATT_W]
    ikw = z[:, o + 2 * ATT_W:o + 2 * ATT_W + LANES]
    ikw_ref[...] = ikw
    ikwb_ref[...] = ikw.astype(BF16)
    zt = _dot_nt(wt_ref[...], hb)
    qT_ref[...] = (zt[0:ATT_W] * (HEAD_DIM ** -0.5)).astype(BF16)
    iqT_ref[...] = zt[ATT_W:2 * ATT_W].astype(BF16)
    for c in range(vT_ref.shape[0]):
        vT_ref[c] = zt[2 * ATT_W:3 * ATT_W, c * TK:(c + 1) * TK].astype(BF16)
    iwT_ref[...] = zt[3 * ATT_W:3 * ATT_W + IDX_HEADS]


def _inproj_p(x, g, wm, wt, tm):
    n, d = x.shape
    grid = (n // tm,)
    row = lambda w: pl.BlockSpec((tm, w), lambda i: (i, 0))
    out_shape = (
        jax.ShapeDtypeStruct((n, d), F32), jax.ShapeDtypeStruct((n, d), F32),
        jax.ShapeDtypeStruct((n, d), F32), jax.ShapeDtypeStruct((n, d), F32),
        jax.ShapeDtypeStruct((n, ATT_W), F32), jax.ShapeDtypeStruct((n, ATT_W), F32),
        jax.ShapeDtypeStruct((n, LANES), F32),
        jax.ShapeDtypeStruct((n, ATT_W), BF16), jax.ShapeDtypeStruct((n, LANES), BF16),
        jax.ShapeDtypeStruct((ATT_W, n), BF16), jax.ShapeDtypeStruct((ATT_W, n), BF16),
        jax.ShapeDtypeStruct((n // TK, ATT_W, TK), BF16),
        jax.ShapeDtypeStruct((IDX_HEADS, n), F32),
    )
    colT = lambda r: pl.BlockSpec((r, tm), lambda i: (0, i))
    out_specs = (row(d), row(d), row(d), row(d), row(ATT_W), row(ATT_W), row(LANES),
                 row(ATT_W), row(LANES), colT(ATT_W), colT(ATT_W),
                 pl.BlockSpec((tm // TK, ATT_W, TK), lambda i: (i, 0, 0)), colT(IDX_HEADS))
    return pl.pallas_call(
        _inproj_p_kernel, grid=grid,
        in_specs=[row(d), _const_spec(g.shape), _const_spec(wm.shape), _const_spec(wt.shape)],
        out_specs=out_specs, out_shape=out_shape, compiler_params=_params(1),
        name="inproj_prompt")(x, g, wm, wt)


def _inproj_s_kernel(x_ref, g_ref, wm_ref,
                     xr_ref, gr_ref, ga_ref, gb_ref, k_ref, v_ref, ikw_ref, q_ref, iq_ref):
    d = x_ref.shape[1]
    hb = _rms(x_ref[...], g_ref[...]).astype(BF16)
    z = _dot(hb, wm_ref[...])
    xr_ref[...] = z[:, 0:d]
    gr_ref[...] = z[:, d:2 * d]
    ga_ref[...] = z[:, 2 * d:3 * d]
    gb_ref[...] = z[:, 3 * d:4 * d]
    o = 4 * d
    k_ref[...] = z[:, o:o + ATT_W]
    v_ref[...] = z[:, o + ATT_W:o + 2 * ATT_W]
    ikw_ref[...] = z[:, o + 2 * ATT_W:o + 2 * ATT_W + LANES]
    o2 = o + 2 * ATT_W + LANES
    q_ref[...] = (z[:, o2:o2 + ATT_W] * (HEAD_DIM ** -0.5)).astype(BF16)
    iq_ref[...] = z[:, o2 + ATT_W:o2 + 2 * ATT_W].astype(BF16)


def _inproj_s(x, g, wm, tm):
    n, d = x.shape
    row = lambda w: pl.BlockSpec((tm, w), lambda i: (i, 0))
    out_shape = (
        jax.ShapeDtypeStruct((n, d), F32), jax.ShapeDtypeStruct((n, d), F32),
        jax.ShapeDtypeStruct((n, d), F32), jax.ShapeDtypeStruct((n, d), F32),
        jax.ShapeDtypeStruct((n, ATT_W), F32), jax.ShapeDtypeStruct((n, ATT_W), F32),
        jax.ShapeDtypeStruct((n, LANES), F32),
        jax.ShapeDtypeStruct((n, ATT_W), BF16), jax.ShapeDtypeStruct((n, ATT_W), BF16),
    )
    out_specs = (row(d), row(d), row(d), row(d), row(ATT_W), row(ATT_W), row(LANES),
                 row(ATT_W), row(ATT_W))
    return pl.pallas_call(
        _inproj_s_kernel, grid=(n // tm,),
        in_specs=[row(d), _const_spec(g.shape), _const_spec(wm.shape)],
        out_specs=out_specs, out_shape=out_shape, compiler_params=_params(1),
        name="inproj_sample")(x, g, wm)


def _lru_gates(xc, wa_ref, wx_ref, ba, bx, lam):
    xcb = xc.astype(BF16)
    gw = wa_ref.shape[1]
    r_parts, i_parts = [], []
    for g in range(wa_ref.shape[0]):
        xs = xcb[:, g * gw:(g + 1) * gw]
        r_parts.append(_dot(xs, wa_ref[g]))
        i_parts.append(_dot(xs, wx_ref[g]))
    r = _sigmoid(jnp.concatenate(r_parts, axis=1) + ba)
    i = _sigmoid(jnp.concatenate(i_parts, axis=1) + bx)
    log_sig_lam = jnp.minimum(lam, 0.0) - jnp.log(1.0 + jnp.exp(-jnp.abs(lam)))
    log_a = LRU_C * r * log_sig_lam
    a = jnp.exp(log_a)
    u = jnp.sqrt(1.0 - jnp.exp(2.0 * log_a)) * (i * xc)
    return a, u


def _rglru_p_kernel(xr_ref, gr_ref, cw_ref, cb_ref, wa_ref, wx_ref, ba_ref, bx_ref, lam_ref,
                    ya_ref, hlast_ref, buf_ref,
                    xx_ref, a_ref, u_ref, h_ref):
    step = pl.program_id(0)
    nb, tt, d = xr_ref.shape
    cw = cw_ref.shape[0]

    @pl.when(step == 0)
    def _():
        xx_ref[:, 0:8, :] = jnp.zeros((nb, 8, d), F32)
        h_ref[...] = jnp.zeros_like(h_ref)

    for b in range(nb):
        x = xr_ref[b]
        xx_ref[b, 8:8 + tt, :] = x
        xc = cb_ref[...] + x * cw_ref[cw - 1:cw, :]
        for j in range(cw - 1):
            sh = cw - 1 - j
            xc = xc + xx_ref[b, 8 - sh:8 - sh + tt, :] * cw_ref[j:j + 1, :]
        xx_ref[b, 0:8, :] = x[tt - 8:tt, :]
        a, u = _lru_gates(xc, wa_ref, wx_ref, ba_ref[...], bx_ref[...], lam_ref[...])
        a_ref[b] = a
        u_ref[b] = u

    def scan_body(t, hs):
        new = []
        for b in range(nb):
            h = a_ref[b, pl.ds(t, 1), :] * hs[b] + u_ref[b, pl.ds(t, 1), :]
            u_ref[b, pl.ds(t, 1), :] = h
            new.append(h)
        return tuple(new)

    hs = lax.fori_loop(0, tt, scan_body, tuple(h_ref[b:b + 1, :] for b in range(nb)), unroll=8)
    for b in range(nb):
        h_ref[b:b + 1, :] = hs[b]
        ya_ref[b] = (u_ref[b] * _gelu_tanh(gr_ref[b])).astype(BF16)
        buf_ref[b] = xr_ref[b, tt - (cw - 1):tt, :]
    hlast_ref[...] = h_ref[...]


def _rglru_p(xr, gr, cw, cb, wa, wx, ba, bx, lam, tt):
    nb, t, d = xr.shape
    blk = pl.BlockSpec((nb, tt, d), lambda i: (0, i, 0))
    consts = [cw, cb, wa, wx, ba, bx, lam]
    return pl.pallas_call(
        _rglru_p_kernel, grid=(t // tt,),
        in_specs=[blk, blk] + [_const_spec(c.shape) for c in consts],
        out_specs=(blk, _const_spec((nb, d)), _const_spec((nb, cw.shape[0] - 1, d))),
        out_shape=(jax.ShapeDtypeStruct((nb, t, d), BF16), jax.ShapeDtypeStruct((nb, d), F32),
                   jax.ShapeDtypeStruct((nb, cw.shape[0] - 1, d), F32)),
        scratch_shapes=[pltpu.VMEM((nb, tt + 8, d), F32), pltpu.VMEM((nb, tt, d), F32),
                        pltpu.VMEM((nb, tt, d), F32), pltpu.VMEM((nb, d), F32)],
        compiler_params=_params(1), name="rglru_prompt")(xr, gr, *consts)


def _rglru_s_kernel(xr_ref, gr_ref, st_ref, h0_ref, cw_ref, cb_ref, wa_ref, wx_ref, ba_ref,
                    bx_ref, lam_ref, ya_ref, hlast_ref, buf_ref):
    t_len, nb, d = xr_ref.shape
    cw = cw_ref.shape[0]
    rows = [st_ref[j] for j in range(cw - 1)] + [xr_ref[t] for t in range(t_len)]
    h = h0_ref[...]
    for t in range(t_len):
        xc = cb_ref[...] + rows[t + cw - 1] * cw_ref[cw - 1:cw, :]
        for j in range(cw - 1):
            xc = xc + rows[t + j] * cw_ref[j:j + 1, :]
        a, u = _lru_gates(xc, wa_ref, wx_ref, ba_ref[...], bx_ref[...], lam_ref[...])
        h = a * h + u
        ya_ref[t] = (h * _gelu_tanh(gr_ref[t])).astype(BF16)
    hlast_ref[...] = h
    for j in range(cw - 1):
        buf_ref[j] = rows[t_len + j]


def _rglru_s(xr, gr, st, h0, cw, cb, wa, wx, ba, bx, lam):
    t_len, nb, d = xr.shape
    args = [xr, gr, st, h0, cw, cb, wa, wx, ba, bx, lam]
    return pl.pallas_call(
        _rglru_s_kernel, grid=(1,),
        in_specs=[_const_spec(a.shape) for a in args],
        out_specs=(_const_spec((t_len, nb, d)), _const_spec((nb, d)),
                   _const_spec((cw.shape[0] - 1, nb, d))),
        out_shape=(jax.ShapeDtypeStruct((t_len, nb, d), BF16), jax.ShapeDtypeStruct((nb, d), F32),
                   jax.ShapeDtypeStruct((cw.shape[0] - 1, nb, d), F32)),
        compiler_params=_params(1), name="rglru_sample")(*args)


def _sortable_key(score):
    bits = pltpu.bitcast(score, I32)
    return bits ^ ((bits >> 31) & 0x7FFFFFFF)


def _attn_p_kernel(qT_ref, iqT_ref, iwT_ref, kb_ref, vT_ref, ikw_ref, dtab_ref, o_ref,
                   keys_ref, rhs_ref, qbd_ref, m_ref, l_ref, acc_ref, *, k_top):
    qb = pl.program_id(1)
    t0 = qb * TQ
    n_chunks = (qb + 2) >> 1
    n_far = jnp.maximum(qb - 1, 0) >> 1
    half = LANES // 2

    zeros_half = jnp.zeros((half, TQ), BF16)
    for c in range(N_HEADS // 2):
        for hh in range(2):
            h = 2 * c + hh
            iq_h = iqT_ref[h * IDX_DIM:(h + 1) * IDX_DIM, :]
            rhs_ref[c, :, hh * TQ:(hh + 1) * TQ] = jnp.concatenate([iq_h, zeros_half], axis=0)
            q_h = qT_ref[h * HEAD_DIM:(h + 1) * HEAD_DIM, :]
            parts = [zeros_half, q_h] if hh else [q_h, zeros_half]
            qbd_ref[c, :, hh * TQ:(hh + 1) * TQ] = jnp.concatenate(parts, axis=0)

    w = (iwT_ref[...] * (IDX_HEADS ** -0.5)) * (IDX_DIM ** -0.5)
    s_iota = lax.broadcasted_iota(I32, (TK, TQ), 0)
    t_glob = t0 + lax.broadcasted_iota(I32, (TK, TQ), 1)

    def score_body(j, carry):
        off = pl.multiple_of(j * TK, TK)
        ikc = ikw_ref[pl.ds(off, TK), :]
        score = jnp.zeros((TK, TQ), F32)
        for c in range(N_HEADS // 2):
            dd = _dot(ikc, rhs_ref[c])
            for hh in range(2):
                h = 2 * c + hh
                score = score + jnp.maximum(dd[:, hh * TQ:(hh + 1) * TQ], 0.0) * w[h:h + 1, :]
        key = _sortable_key(score)
        key = jnp.where(s_iota + off <= t_glob, key, INT_MIN)
        keys_ref[pl.ds(off, TK), :] = key
        return carry

    lax.fori_loop(0, n_chunks, score_body, 0)

    def count(pred):
        def body(j, acc):
            off = pl.multiple_of(j * TK, TK)
            m = pred(keys_ref[pl.ds(off, TK), :]).astype(I32)
            return acc + m.reshape(TK // 64, 64, TQ).sum(axis=0)
        acc = lax.fori_loop(0, n_chunks, body, jnp.zeros((64, TQ), I32))
        return acc.sum(axis=0, keepdims=True)

    def bit_body(p, thr):
        bit = jnp.left_shift(jnp.int32(1), 31 - p)
        cand = thr ^ bit
        cnt = count(lambda kc: kc >= cand)
        return jnp.where(cnt >= k_top, cand, thr)

    thr = lax.fori_loop(0, 32, bit_body, jnp.full((1, TQ), INT_MIN, I32))
    need = (k_top - count(lambda kc: kc > thr)).astype(F32)

    m_ref[...] = jnp.full(m_ref.shape, NEG, F32)
    l_ref[...] = jnp.zeros(l_ref.shape, F32)
    acc_ref[...] = jnp.zeros(acc_ref.shape, F32)
    ltri = (lax.broadcasted_iota(I32, (TK, TK), 1) <= lax.broadcasted_iota(I32, (TK, TK), 0)
            ).astype(BF16)

    def attend(j, carry, band):
        off = pl.multiple_of(j * TK, TK)
        keyc = keys_ref[pl.ds(off, TK), :]
        eq = keyc == thr
        prefix = _dot(ltri, eq.astype(BF16))
        sel = (keyc > thr) | (eq & (prefix + carry <= need))
        if band:
            sel = sel & (s_iota + off <= t_glob)
            tab = (t0 - off) // TQ
        kc = kb_ref[pl.ds(off, TK), :]
        for c in range(N_HEADS // 2):
            lg = _dot(kc[:, c * LANES:(c + 1) * LANES], qbd_ref[c])
            for hh in range(2):
                h = 2 * c + hh
                x = lg[:, hh * TQ:(hh + 1) * TQ]
                if band:
                    x = x + dtab_ref[tab, h]
                x = jnp.where(sel, x, NEG)
                m_old = m_ref[h:h + 1, :]
                m_new = jnp.maximum(m_old, x.max(axis=0, keepdims=True))
                alpha = jnp.exp(m_old - m_new)
                p = jnp.exp(x - m_new)
                l_ref[h:h + 1, :] = l_ref[h:h + 1, :] * alpha + p.sum(axis=0, keepdims=True)
                pv = _dot(vT_ref[j, h * HEAD_DIM:(h + 1) * HEAD_DIM, :], p.astype(BF16))
                rows = slice(h * HEAD_DIM, (h + 1) * HEAD_DIM)
                acc_ref[rows, :] = acc_ref[rows, :] * alpha + pv
                m_ref[h:h + 1, :] = m_new
        return carry + prefix[TK - 1:TK, :]

    carry = lax.fori_loop(0, n_far, functools.partial(attend, band=False),
                          jnp.zeros((1, TQ), F32))
    lax.fori_loop(n_far, n_chunks, functools.partial(attend, band=True), carry)

    inv = 1.0 / l_ref[...]
    for h in range(N_HEADS):
        rows = slice(h * HEAD_DIM, (h + 1) * HEAD_DIM)
        acc_ref[rows, :] = acc_ref[rows, :] * inv[h:h + 1, :]
    o_ref[...] = acc_ref[...].T.astype(BF16)


def _attn_p(qT, iqT, iwT, kb, vT, ikwb, dtab, nb, t):
    n = nb * t
    nq = t // TQ
    k_top = min(TOPK_MAX, t // 4)
    colT = lambda r: pl.BlockSpec((r, TQ), lambda b, q: (0, b * nq + q))
    return pl.pallas_call(
        functools.partial(_attn_p_kernel, k_top=k_top), grid=(nb, nq),
        in_specs=[colT(ATT_W), colT(ATT_W), colT(IDX_HEADS),
                  pl.BlockSpec((t, ATT_W), lambda b, q: (b, 0)),
                  pl.BlockSpec((t // TK, ATT_W, TK), lambda b, q: (b, 0, 0)),
                  pl.BlockSpec((t, LANES), lambda b, q: (b, 0)),
                  _const_spec(dtab.shape)],
        out_specs=pl.BlockSpec((TQ, ATT_W), lambda b, q: (b * nq + q, 0)),
        out_shape=jax.ShapeDtypeStruct((n, ATT_W), BF16),
        scratch_shapes=[pltpu.VMEM((t, TQ), I32),
                        pltpu.VMEM((N_HEADS // 2, LANES, 2 * TQ), BF16),
                        pltpu.VMEM((N_HEADS // 2, LANES, 2 * TQ), BF16),
                        pltpu.VMEM((N_HEADS, TQ), F32), pltpu.VMEM((N_HEADS, TQ), F32),
                        pltpu.VMEM((ATT_W, TQ), F32)],
        compiler_params=_params(2), name="attn_prompt")(qT, iqT, iwT, kb, vT, ikwb, dtab)


def _select_s_kernel(pt_ref, iq_ref, wcol_ref, iknew_ref, pool_ref, mask_ref,
                     ikbuf_ref, score_ref, sem, *, k_top, layer_off, t_len):
    g = pl.program_id(0)
    gs, s_pad, _ = ikbuf_ref.shape
    n_pages = pt_ref.shape[1]
    page = pool_ref.shape[1]
    past = n_pages * page
    rows_q = t_len * IDX_HEADS

    def page_copy(i, j):
        return pltpu.make_async_copy(pool_ref.at[layer_off + pt_ref[g * gs + i, j]],
                                     ikbuf_ref.at[i, pl.ds(j * page, page), :], sem)

    def start_body(idx, c):
        page_copy(idx // n_pages, idx % n_pages).start()
        return c

    lax.fori_loop(0, gs * n_pages, start_body, 0)
    ikbuf_ref[:, past:s_pad, :] = jnp.zeros((gs, s_pad - past, IDX_DIM), F32)
    ikbuf_ref[:, past:past + t_len, :] = iknew_ref[...]

    def wait_body(idx, c):
        page_copy(idx // n_pages, idx % n_pages).wait()
        return c

    lax.fori_loop(0, gs * n_pages, wait_body, 0)

    def score_body(i, c):
        ikb = ikbuf_ref[i].astype(BF16)
        dd = _dot_nt(iq_ref[i], ikb)
        wv = (wcol_ref[i] * (IDX_HEADS ** -0.5)) * (IDX_DIM ** -0.5)
        sc = (jnp.maximum(dd, 0.0) * wv).reshape(t_len, IDX_HEADS, s_pad).sum(axis=1)
        score_ref[i, 0:t_len, :] = sc
        return c

    lax.fori_loop(0, gs, score_body, 0)

    tile = score_ref.shape[1]
    rows = gs * tile
    s_idx = lax.broadcasted_iota(I32, (rows, s_pad), 1)
    t_idx = lax.broadcasted_iota(I32, (rows, s_pad), 0) % tile
    valid = (s_idx <= past + t_idx) & (t_idx < t_len)
    keys = jnp.where(valid, _sortable_key(score_ref[...].reshape(rows, s_pad)), INT_MIN)

    def bit_body(p, thr):
        bit = jnp.left_shift(jnp.int32(1), 31 - p)
        cand = thr ^ bit
        cnt = jnp.sum((keys >= cand).astype(I32), axis=1, keepdims=True)
        return jnp.where(cnt >= k_top, cand, thr)

    thr = lax.fori_loop(0, 32, bit_body, jnp.full((rows, 1), INT_MIN, I32))
    gt = keys > thr
    eq = keys == thr
    need = (k_top - jnp.sum(gt.astype(I32), axis=1, keepdims=True)).astype(F32)
    utri = (lax.broadcasted_iota(I32, (LANES, LANES), 0) <= lax.broadcasted_iota(I32, (LANES, LANES), 1)
            ).astype(BF16)
    eqb = eq.astype(BF16)
    carry = jnp.zeros((rows, 1), F32)
    for c in range(s_pad // LANES):
        cols = slice(c * LANES, (c + 1) * LANES)
        prefix = _dot(eqb[:, cols], utri) + carry
        take = gt[:, cols] | (eq[:, cols] & (prefix <= need))
        take = take & valid[:, cols]
        m = jnp.where(take, 0.0, NEG)
        mask_ref[:, :, cols] = m.reshape(gs, tile, LANES)[:, 0:t_len, :]
        carry = prefix[:, LANES - 1:LANES]


def _select_s(page_table, iq, wcol, iknew, pool, layer, gs, k_top):
    nb, rows_q, _ = iq.shape
    t_len = rows_q // IDX_HEADS
    n_pages = page_table.shape[1]
    page = pool.shape[1]
    s_pad = n_pages * page + LANES
    kern = functools.partial(_select_s_kernel, k_top=k_top, layer_off=layer, t_len=t_len)
    grid_spec = pltpu.PrefetchScalarGridSpec(
        num_scalar_prefetch=1, grid=(nb // gs,),
        in_specs=[pl.BlockSpec((gs, rows_q, IDX_DIM), lambda g, pt: (g, 0, 0)),
                  pl.BlockSpec((gs, rows_q, 1), lambda g, pt: (g, 0, 0)),
                  pl.BlockSpec((gs, t_len, IDX_DIM), lambda g, pt: (g, 0, 0)),
                  pl.BlockSpec(memory_space=pl.ANY)],
        out_specs=pl.BlockSpec((gs, t_len, s_pad), lambda g, pt: (g, 0, 0)),
        scratch_shapes=[pltpu.VMEM((gs, s_pad, IDX_DIM), F32),
                        pltpu.VMEM((gs, 8, s_pad), F32),
                        pltpu.SemaphoreType.DMA(())])
    return pl.pallas_call(
        kern, grid_spec=grid_spec,
        out_shape=jax.ShapeDtypeStruct((nb, t_len, s_pad), F32),
        compiler_params=_params(1), name="select_sample")(page_table, iq, wcol, iknew, pool)


def _attn_s_kernel(pt_ref, q_ref, knew_ref, vnew_ref, mask_ref, dtab_ref, *rest, n_pages, t_len):
    k_pages = rest[:n_pages]
    v_pages = rest[n_pages:2 * n_pages]
    o_ref, kpad_ref, vpad_ref = rest[2 * n_pages:]
    rows = t_len * N_HEADS
    page = k_pages[0].shape[0]

    def per_head_rows(x):
        return jnp.concatenate(
            [jnp.broadcast_to(x[t:t + 1, :], (N_HEADS, x.shape[1])) for t in range(t_len)], axis=0)

    col_head = lax.broadcasted_iota(I32, (rows, ATT_W), 1) // HEAD_DIM
    row_head = lax.broadcasted_iota(I32, (rows, ATT_W), 0) % N_HEADS
    own = col_head == row_head
    qbd = jnp.where(own, per_head_rows(q_ref[...].astype(F32)), 0.0).astype(BF16)
    kpad_ref[...] = jnp.zeros(kpad_ref.shape, F32)
    vpad_ref[...] = jnp.zeros(vpad_ref.shape, F32)
    kpad_ref[0:t_len, :] = knew_ref[...]
    vpad_ref[0:t_len, :] = vnew_ref[...]
    knew = kpad_ref[...].astype(BF16)
    vnew = vpad_ref[...].astype(BF16)
    logits = [_dot_nt(qbd, kp[...].astype(BF16)) for kp in k_pages] + [_dot_nt(qbd, knew)]
    x = jnp.concatenate(logits, axis=1)
    x = x + dtab_ref[...] + per_head_rows(mask_ref[...])
    m = x.max(axis=1, keepdims=True)
    p = jnp.exp(x - m)
    l = p.sum(axis=1, keepdims=True)
    pb = p.astype(BF16)
    out = _dot(pb[:, n_pages * page:], vnew)
    for j, vp in enumerate(v_pages):
        out = out + _dot(pb[:, j * page:(j + 1) * page], vp[...].astype(BF16))
    out = out / l
    out = jnp.where(own, out, 0.0).reshape(t_len, N_HEADS, ATT_W).sum(axis=1)
    o_ref[...] = out.astype(BF16)


def _attn_s(page_table, q, knew, vnew, mask, dtab, k_pool, v_pool, layer_off):
    nb, t_len, _ = q.shape
    n_pages = page_table.shape[1]
    page = k_pool.shape[1]
    s_pad = n_pages * page + LANES

    def page_spec(j):
        return pl.BlockSpec((None, page, ATT_W), lambda b, pt, j=j: (layer_off + pt[b, j], 0, 0))

    seq = lambda w: pl.BlockSpec((None, t_len, w), lambda b, pt: (b, 0, 0))
    grid_spec = pltpu.PrefetchScalarGridSpec(
        num_scalar_prefetch=1, grid=(nb,),
        in_specs=[seq(ATT_W), seq(ATT_W), seq(ATT_W), seq(s_pad),
                  pl.BlockSpec(dtab.shape, lambda b, pt: (0, 0))]
                 + [page_spec(j) for j in range(n_pages)] * 2,
        out_specs=seq(ATT_W),
        scratch_shapes=[pltpu.VMEM((page, ATT_W), F32), pltpu.VMEM((page, ATT_W), F32)])
    kern = functools.partial(_attn_s_kernel, n_pages=n_pages, t_len=t_len)
    return pl.pallas_call(
        kern, grid_spec=grid_spec, out_shape=jax.ShapeDtypeStruct((nb, t_len, ATT_W), BF16),
        compiler_params=_params(1), name="attn_sample")(
            page_table, q, knew, vnew, mask, dtab, *([k_pool] * n_pages), *([v_pool] * n_pages))


def _merge_kernel(x_ref, ya_ref, o_ref, ga_ref, gb_ref, wr_ref, wa_ref, wo_ref, out_ref):
    a = _dot(ya_ref[...], wr_ref[...])
    b = _dot(o_ref[...], wa_ref[...])
    mix = _sigmoid(ga_ref[...]) * a + _sigmoid(gb_ref[...]) * b
    out_ref[...] = x_ref[...] + _dot(mix.astype(BF16), wo_ref[...])


def _merge(x, ya, o, ga, gb, wr, wa, wo, tm):
    n, d = x.shape
    row = lambda w: pl.BlockSpec((tm, w), lambda i: (i, 0))
    return pl.pallas_call(
        _merge_kernel, grid=(n // tm,),
        in_specs=[row(d), row(d), row(ATT_W), row(d), row(d),
                  _const_spec(wr.shape), _const_spec(wa.shape), _const_spec(wo.shape)],
        out_specs=row(d), out_shape=jax.ShapeDtypeStruct((n, d), F32),
        compiler_params=_params(1), name="merge")(x, ya, o, ga, gb, wr, wa, wo)


def _ffn_kernel(x_ref, p_ref, nf_ref, wg_ref, wu_ref, wd_ref, np_ref, wpg_ref, wpp_ref, nfin_ref,
                out_ref, *, final):
    x = x_ref[...]
    h2 = _rms(x, nf_ref[...]).astype(BF16)
    g = _dot(h2, wg_ref[...])
    u = _dot(h2, wu_ref[...])
    act = (g * _sigmoid(g)) * u
    x = x + _dot(act.astype(BF16), wd_ref[...])
    h3 = _rms(x, np_ref[...]).astype(BF16)
    x = x + _sigmoid(_dot(h3, wpg_ref[...])) * _dot(p_ref[...].astype(BF16), wpp_ref[...])
    if final:
        x = _rms(x, nfin_ref[...])
    out_ref[...] = x


def _ffn(x, p, nf, wg, wu, wd, npl, wpg, wpp, nfin, tm, final):
    n, d = x.shape
    row = lambda w: pl.BlockSpec((tm, w), lambda i: (i, 0))
    consts = [nf, wg, wu, wd, npl, wpg, wpp, nfin]
    return pl.pallas_call(
        functools.partial(_ffn_kernel, final=final), grid=(n // tm,),
        in_specs=[row(d), row(p.shape[1])] + [_const_spec(c.shape) for c in consts],
        out_specs=row(d), out_shape=jax.ShapeDtypeStruct((n, d), F32),
        compiler_params=_params(1), name="ffn_ple")(x, p, *consts)


def _rel_bucket(n, n_buckets):
    max_exact = n_buckets // 2
    nf = jnp.maximum(n, 1).astype(F32)
    large = max_exact + (jnp.log(nf / max_exact) / math.log(MAX_DISTANCE / max_exact)
                         * (n_buckets - max_exact)).astype(I32)
    large = jnp.minimum(large, n_buckets - 1)
    return jnp.where(n < max_exact, n, large)


def _bias_delta(rel_bias, n):
    nbk = rel_bias.shape[0]
    b = rel_bias.astype(F32)
    return b[_rel_bucket(jnp.maximum(n, 0), nbk)] - b[nbk - 1]


def _block_diag_groups(w, group):
    nbk, c, _ = w.shape
    per = group // c
    wg = w.reshape(nbk // per, per, c, c)
    eye = jnp.eye(per, dtype=w.dtype)
    return jnp.einsum('gpcd,pq->gpcqd', wg, eye).reshape(nbk // per, group, group)


def kernel(x_prompt, x_sample, p_prompt, p_sample, cache_k, cache_v, cache_idx_k, state_rglru_h,
           state_conv, page_table, rel_bias, norm_mix, w_in, conv_w, conv_b, w_rg_a, b_rg_a,
           w_rg_x, b_rg_x, lru_lambda, w_rnn_out, w_att_out, w_o, norm_ffn, w_ffn_gate, w_ffn_up,
           w_ffn_down, norm_ple, w_ple_gate, w_ple_proj, norm_final):
    nb, t, d = x_prompt.shape
    sb, st, _ = x_sample.shape
    depth = w_in.shape[0]
    n_pool, page = cache_k.shape[1], cache_k.shape[2]
    n_pages = page_table.shape[1]
    past = n_pages * page
    cw = conv_w.shape[1]
    s_pad = past + LANES
    k_top_s = min(TOPK_MAX, (past + st) // 4)
    row2 = lambda v: v.reshape(1, -1).astype(F32)

    idx3 = jnp.arange(3, dtype=I32)[:, None, None]
    s_rel = jnp.arange(TK, dtype=I32)[None, :, None]
    t_rel = jnp.arange(TQ, dtype=I32)[None, None, :]
    dtab_p = _bias_delta(rel_bias, idx3 * TQ + t_rel - s_rel).transpose(0, 3, 1, 2)
    s_all = jnp.arange(s_pad, dtype=I32)[None, :]
    q_pos = past + jnp.arange(st, dtype=I32)[:, None]
    dtab_s = _bias_delta(rel_bias, q_pos - s_all)
    dtab_s = dtab_s.transpose(0, 2, 1).reshape(st * N_HEADS, s_pad)

    k_pool = cache_k.reshape(depth * n_pool, page, ATT_W)
    v_pool = cache_v.reshape(depth * n_pool, page, ATT_W)
    ik_pool = cache_idx_k.reshape(depth * n_pool, page, IDX_DIM)

    xp = x_prompt.reshape(nb * t, d)
    xs = x_sample.transpose(1, 0, 2).reshape(st * sb, d)
    outs_p, outs_s = [], []
    for l in range(depth):
        splits = [d, 2 * d, 2 * d + ATT_W, 2 * d + 2 * ATT_W, 2 * d + 3 * ATT_W,
                  2 * d + 3 * ATT_W + IDX_HEADS * IDX_DIM,
                  2 * d + 3 * ATT_W + IDX_HEADS * IDX_DIM + IDX_DIM,
                  2 * d + 3 * ATT_W + IDX_HEADS * IDX_DIM + IDX_DIM + IDX_HEADS,
                  3 * d + 3 * ATT_W + IDX_HEADS * IDX_DIM + IDX_DIM + IDX_HEADS]
        w_xr, w_gr, w_q, w_k, w_v, w_iq, w_ik, w_iw, w_ga, w_gb = jnp.split(w_in[l], splits, axis=1)
        w_ikw = jnp.concatenate(
            [w_ik, w_iw, jnp.zeros((d, LANES - IDX_DIM - IDX_HEADS), F32)], axis=1)
        wm_p = jnp.concatenate([w_xr, w_gr, w_ga, w_gb, w_k, w_v, w_ikw], axis=1).astype(BF16)
        wm_s = jnp.concatenate([w_xr, w_gr, w_ga, w_gb, w_k, w_v, w_ikw, w_q, w_iq], axis=1).astype(BF16)
        wt_p = jnp.concatenate([w_q, w_iq, w_v, w_iw], axis=1).T.astype(BF16)
        gw = 2 * LANES
        wa_bd = _block_diag_groups(w_rg_a[l], gw).astype(BF16)
        wx_bd = _block_diag_groups(w_rg_x[l], gw).astype(BF16)
        lru_consts = (conv_w[l], row2(conv_b[l]), wa_bd, wx_bd, row2(b_rg_a[l]), row2(b_rg_x[l]),
                      row2(lru_lambda[l]))
        w_r, w_a, w_oo = (w_rnn_out[l].astype(BF16), w_att_out[l].astype(BF16), w_o[l].astype(BF16))
        ffn_consts = (row2(norm_ffn[l]), w_ffn_gate[l].astype(BF16), w_ffn_up[l].astype(BF16),
                      w_ffn_down[l].astype(BF16), row2(norm_ple[l]), w_ple_gate[l].astype(BF16),
                      w_ple_proj[l].astype(BF16), row2(norm_final))
        final = l == depth - 1

        (xr, gr, ga, gb, k, v, ikw, kb, ikwb, qT, iqT, vT, iwT) = _inproj_p(
            xp, row2(norm_mix[l]), wm_p, wt_p, tm=256)
        ya, h_last, new_buf = _rglru_p(xr.reshape(nb, t, d), gr.reshape(nb, t, d), *lru_consts, tt=128)
        o = _attn_p(qT, iqT, iwT, kb, vT, ikwb, dtab_p, nb, t)
        x1 = _merge(xp, ya.reshape(nb * t, d), o, ga, gb, w_r, w_a, w_oo, tm=512)
        xp = _ffn(x1, p_prompt[l].reshape(nb * t, -1), *ffn_consts, tm=256, final=final)
        outs_p.append((k.reshape(nb, t, N_HEADS, HEAD_DIM), v.reshape(nb, t, N_HEADS, HEAD_DIM),
                       ikw[:, :IDX_DIM].reshape(nb, t, IDX_DIM), h_last, new_buf))

        (xr, gr, ga, gb, k, v, ikw, q, iq) = _inproj_s(xs, row2(norm_mix[l]), wm_s, tm=st * sb)
        tmaj = lambda a2: a2.reshape(st, sb, -1)
        ya, h_last, new_buf = _rglru_s(tmaj(xr), tmaj(gr), state_conv[l].transpose(1, 0, 2),
                                       state_rglru_h[l], *lru_consts)
        bmaj = lambda a2: a2.reshape(st, sb, -1).transpose(1, 0, 2)
        k_b, v_b, ikw_b, q_b, iq_b = bmaj(k), bmaj(v), bmaj(ikw), bmaj(q), bmaj(iq)
        iq_rows = iq_b.reshape(sb, st * IDX_HEADS, IDX_DIM)
        w_col = ikw_b[:, :, IDX_DIM:IDX_DIM + IDX_HEADS].reshape(sb, st * IDX_HEADS, 1)
        mask = _select_s(page_table, iq_rows, w_col, ikw_b[:, :, :IDX_DIM], ik_pool, l * n_pool,
                         gs=min(sb, 16), k_top=k_top_s)
        o = _attn_s(page_table, q_b, k_b, v_b, mask, dtab_s, k_pool, v_pool, l * n_pool)
        o = o.transpose(1, 0, 2).reshape(st * sb, ATT_W)
        x1 = _merge(xs, ya.reshape(st * sb, d), o, ga, gb, w_r, w_a, w_oo, tm=st * sb)
        xs = _ffn(x1, p_sample[l].transpose(1, 0, 2).reshape(st * sb, -1), *ffn_consts,
                  tm=min(256, st * sb), final=final)
        outs_s.append((k_b.reshape(sb, st, N_HEADS, HEAD_DIM), v_b.reshape(sb, st, N_HEADS, HEAD_DIM),
                       ikw_b[:, :, :IDX_DIM], h_last, new_buf.transpose(1, 0, 2)))

    y_prompt = xp.reshape(nb, t, d)
    y_sample = xs.reshape(st, sb, d).transpose(1, 0, 2)
    stack = lambda outs, i: jnp.stack([o_[i] for o_ in outs])
    return (y_prompt, y_sample,
            stack(outs_p, 0), stack(outs_p, 1), stack(outs_p, 2), stack(outs_p, 3), stack(outs_p, 4),
            stack(outs_s, 0), stack(outs_s, 1), stack(outs_s, 2), stack(outs_s, 3), stack(outs_s, 4))
```

```python
import functools
import math

import jax
import jax.numpy as jnp
from jax import lax
from jax.experimental import pallas as pl
from jax.experimental.pallas import tpu as pltpu

F32 = jnp.float32
BF16 = jnp.bfloat16
I32 = jnp.int32

N_HEADS = 8
HEAD_DIM = 64
ATT_W = N_HEADS * HEAD_DIM
IDX_HEADS = 8
IDX_DIM = 64
TOPK_MAX = 256
LRU_C = 8.0
MAX_DISTANCE = 128
EPS = 1e-6

LANES = 128
TQ = 128
TK = 256
NEG = -1e30
INT_MIN = -2147483648
VMEM_LIMIT = 56 * 1024 * 1024


def _dot(a, b):
    return jnp.dot(a, b, preferred_element_type=F32)


def _dot_nt(a, b):
    return lax.dot_general(a, b, (((1,), (1,)), ((), ())), preferred_element_type=F32)


def _sigmoid(x):
    return 1.0 / (1.0 + jnp.exp(-x))


def _gelu_tanh(x):
    c = math.sqrt(2.0 / math.pi)
    return 0.5 * x * (1.0 + jnp.tanh(c * (x + 0.044715 * (x * x * x))))


def _rms(x, g):
    y = x * lax.rsqrt(jnp.mean(x * x, axis=-1, keepdims=True) + EPS)
    return y * g


def _params(n_axes=1):
    return pltpu.CompilerParams(dimension_semantics=("arbitrary",) * n_axes,
                                vmem_limit_bytes=VMEM_LIMIT)


def _fori_pairs(lo, hi, body, carry):
    n = hi - lo

    def pair(i, c):
        j = lo + 2 * i
        return body(j + 1, body(j, c))

    carry = lax.fori_loop(0, n >> 1, pair, carry)
    return lax.cond((n & 1) == 1, lambda c: body(hi - 1, c), lambda c: c, carry)


def _const_spec(shape):
    nd = len(shape)
    return pl.BlockSpec(shape, lambda *_: (0,) * nd)


def _inproj_p_kernel(x_ref, g_ref, wm_ref, wt_ref,
                     xr_ref, gr_ref, ga_ref, gb_ref, kb_ref, ikwb_ref,
                     qT_ref, iqT_ref, vTb_ref, iwT_ref, kT_ref, vT_ref, ikT_ref):
    d = x_ref.shape[1]
    hb = _rms(x_ref[...], g_ref[...]).astype(BF16)
    z = _dot(hb, wm_ref[...])
    xr_ref[...] = z[:, 0:d]
    gr_ref[...] = z[:, d:2 * d]
    ga_ref[...] = z[:, 2 * d:3 * d]
    gb_ref[...] = z[:, 3 * d:4 * d]
    o = 4 * d
    kb_ref[...] = z[:, o:o + ATT_W].astype(BF16)
    ikwb_ref[...] = z[:, o + ATT_W:o + ATT_W + LANES].astype(BF16)
    zt = _dot_nt(wt_ref[...], hb)
    qT_ref[...] = (zt[0:ATT_W] * (HEAD_DIM ** -0.5)).astype(BF16)
    iqT_ref[...] = zt[ATT_W:2 * ATT_W].astype(BF16)
    kT_ref[...] = zt[2 * ATT_W:3 * ATT_W]
    vt = zt[3 * ATT_W:4 * ATT_W]
    vT_ref[...] = vt
    for c in range(vTb_ref.shape[0]):
        vTb_ref[c] = vt[:, c * TK:(c + 1) * TK].astype(BF16)
    ikT_ref[...] = zt[4 * ATT_W:4 * ATT_W + IDX_DIM]
    iwT_ref[...] = zt[4 * ATT_W + IDX_DIM:4 * ATT_W + IDX_DIM + IDX_HEADS]


def _inproj_p(x, g, wm, wt, tm, nb):
    n, d = x.shape
    t = n // nb
    per_b = t // tm
    row = lambda w: pl.BlockSpec((tm, w), lambda i: (i, 0))
    out_shape = (
        jax.ShapeDtypeStruct((n, d), F32), jax.ShapeDtypeStruct((n, d), F32),
        jax.ShapeDtypeStruct((n, d), F32), jax.ShapeDtypeStruct((n, d), F32),
        jax.ShapeDtypeStruct((n, ATT_W), BF16), jax.ShapeDtypeStruct((n, LANES), BF16),
        jax.ShapeDtypeStruct((ATT_W, n), BF16), jax.ShapeDtypeStruct((ATT_W, n), BF16),
        jax.ShapeDtypeStruct((n // TK, ATT_W, TK), BF16),
        jax.ShapeDtypeStruct((IDX_HEADS, n), F32),
        jax.ShapeDtypeStruct((ATT_W, n), F32), jax.ShapeDtypeStruct((ATT_W, n), F32),
        jax.ShapeDtypeStruct((nb, IDX_DIM, t), F32),
    )
    colT = lambda r: pl.BlockSpec((r, tm), lambda i: (0, i))
    out_specs = (row(d), row(d), row(d), row(d), row(ATT_W), row(LANES), colT(ATT_W), colT(ATT_W),
                 pl.BlockSpec((tm // TK, ATT_W, TK), lambda i: (i, 0, 0)), colT(IDX_HEADS),
                 colT(ATT_W), colT(ATT_W),
                 pl.BlockSpec((None, IDX_DIM, tm), lambda i: (i // per_b, 0, i % per_b)))
    return pl.pallas_call(
        _inproj_p_kernel, grid=(n // tm,),
        in_specs=[row(d), _const_spec(g.shape), _const_spec(wm.shape), _const_spec(wt.shape)],
        out_specs=out_specs, out_shape=out_shape, compiler_params=_params(1),
        name="inproj_prompt")(x, g, wm, wt)


def _inproj_s_kernel(x_ref, g_ref, wm_ref,
                     xr_ref, gr_ref, ga_ref, gb_ref, k_ref, v_ref, ikw_ref, q_ref, iq_ref):
    d = x_ref.shape[1]
    hb = _rms(x_ref[...], g_ref[...]).astype(BF16)
    z = _dot(hb, wm_ref[...])
    xr_ref[...] = z[:, 0:d]
    gr_ref[...] = z[:, d:2 * d]
    ga_ref[...] = z[:, 2 * d:3 * d]
    gb_ref[...] = z[:, 3 * d:4 * d]
    o = 4 * d
    k_ref[...] = z[:, o:o + ATT_W]
    v_ref[...] = z[:, o + ATT_W:o + 2 * ATT_W]
    ikw_ref[...] = z[:, o + 2 * ATT_W:o + 2 * ATT_W + LANES]
    o2 = o + 2 * ATT_W + LANES
    q_ref[...] = (z[:, o2:o2 + ATT_W] * (HEAD_DIM ** -0.5)).astype(BF16)
    iq_ref[...] = z[:, o2 + ATT_W:o2 + 2 * ATT_W].astype(BF16)


def _inproj_s(x, g, wm, tm):
    n, d = x.shape
    row = lambda w: pl.BlockSpec((tm, w), lambda i: (i, 0))
    out_shape = (
        jax.ShapeDtypeStruct((n, d), F32), jax.ShapeDtypeStruct((n, d), F32),
        jax.ShapeDtypeStruct((n, d), F32), jax.ShapeDtypeStruct((n, d), F32),
        jax.ShapeDtypeStruct((n, ATT_W), F32), jax.ShapeDtypeStruct((n, ATT_W), F32),
        jax.ShapeDtypeStruct((n, LANES), F32),
        jax.ShapeDtypeStruct((n, ATT_W), BF16), jax.ShapeDtypeStruct((n, ATT_W), BF16),
    )
    out_specs = (row(d), row(d), row(d), row(d), row(ATT_W), row(ATT_W), row(LANES),
                 row(ATT_W), row(ATT_W))
    return pl.pallas_call(
        _inproj_s_kernel, grid=(n // tm,),
        in_specs=[row(d), _const_spec(g.shape), _const_spec(wm.shape)],
        out_specs=out_specs, out_shape=out_shape, compiler_params=_params(1),
        name="inproj_sample")(x, g, wm)


def _lru_gates(xc, wa_ref, wx_ref, ba, bx, lam):
    xcb = xc.astype(BF16)
    gw = wa_ref.shape[1]
    r_parts, i_parts = [], []
    for g in range(wa_ref.shape[0]):
        xs = xcb[:, g * gw:(g + 1) * gw]
        r_parts.append(_dot(xs, wa_ref[g]))
        i_parts.append(_dot(xs, wx_ref[g]))
    r = _sigmoid(jnp.concatenate(r_parts, axis=1) + ba)
    i = _sigmoid(jnp.concatenate(i_parts, axis=1) + bx)
    log_sig_lam = jnp.minimum(lam, 0.0) - jnp.log(1.0 + jnp.exp(-jnp.abs(lam)))
    log_a = LRU_C * r * log_sig_lam
    a = jnp.exp(log_a)
    u = jnp.sqrt(1.0 - jnp.exp(2.0 * log_a)) * (i * xc)
    return a, u


def _rglru_p_kernel(xr_ref, gr_ref, cw_ref, cb_ref, wa_ref, wx_ref, ba_ref, bx_ref, lam_ref,
                    ya_ref, hlast_ref, buf_ref,
                    xx_ref, a_ref, u_ref, h_ref):
    step = pl.program_id(0)
    nb, tt, d = xr_ref.shape
    cw = cw_ref.shape[0]

    @pl.when(step == 0)
    def _():
        xx_ref[:, 0:8, :] = jnp.zeros((nb, 8, d), F32)
        h_ref[...] = jnp.zeros_like(h_ref)

    for b in range(nb):
        x = xr_ref[b]
        xx_ref[b, 8:8 + tt, :] = x
        xc = cb_ref[...] + x * cw_ref[cw - 1:cw, :]
        for j in range(cw - 1):
            sh = cw - 1 - j
            xc = xc + xx_ref[b, 8 - sh:8 - sh + tt, :] * cw_ref[j:j + 1, :]
        xx_ref[b, 0:8, :] = x[tt - 8:tt, :]
        a, u = _lru_gates(xc, wa_ref, wx_ref, ba_ref[...], bx_ref[...], lam_ref[...])
        a_ref[b] = a
        u_ref[b] = u

    def scan_body(t, hs):
        new = []
        for b in range(nb):
            h = a_ref[b, pl.ds(t, 1), :] * hs[b] + u_ref[b, pl.ds(t, 1), :]
            u_ref[b, pl.ds(t, 1), :] = h
            new.append(h)
        return tuple(new)

    hs = lax.fori_loop(0, tt, scan_body, tuple(h_ref[b:b + 1, :] for b in range(nb)), unroll=8)
    for b in range(nb):
        h_ref[b:b + 1, :] = hs[b]
        ya_ref[b] = (u_ref[b] * _gelu_tanh(gr_ref[b])).astype(BF16)
        buf_ref[b] = xr_ref[b, tt - (cw - 1):tt, :]
    hlast_ref[...] = h_ref[...]


def _rglru_p(xr, gr, cw, cb, wa, wx, ba, bx, lam, tt):
    nb, t, d = xr.shape
    blk = pl.BlockSpec((nb, tt, d), lambda i: (0, i, 0))
    consts = [cw, cb, wa, wx, ba, bx, lam]
    return pl.pallas_call(
        _rglru_p_kernel, grid=(t // tt,),
        in_specs=[blk, blk] + [_const_spec(c.shape) for c in consts],
        out_specs=(blk, _const_spec((nb, d)), _const_spec((nb, cw.shape[0] - 1, d))),
        out_shape=(jax.ShapeDtypeStruct((nb, t, d), BF16), jax.ShapeDtypeStruct((nb, d), F32),
                   jax.ShapeDtypeStruct((nb, cw.shape[0] - 1, d), F32)),
        scratch_shapes=[pltpu.VMEM((nb, tt + 8, d), F32), pltpu.VMEM((nb, tt, d), F32),
                        pltpu.VMEM((nb, tt, d), F32), pltpu.VMEM((nb, d), F32)],
        compiler_params=_params(1), name="rglru_prompt")(xr, gr, *consts)


def _rglru_s_kernel(xr_ref, gr_ref, st_ref, h0_ref, cw_ref, cb_ref, wa_ref, wx_ref, ba_ref,
                    bx_ref, lam_ref, ya_ref, hlast_ref, buf_ref):
    t_len, nb, d = xr_ref.shape
    cw = cw_ref.shape[0]
    rows = [st_ref[j] for j in range(cw - 1)] + [xr_ref[t] for t in range(t_len)]
    h = h0_ref[...]
    for t in range(t_len):
        xc = cb_ref[...] + rows[t + cw - 1] * cw_ref[cw - 1:cw, :]
        for j in range(cw - 1):
            xc = xc + rows[t + j] * cw_ref[j:j + 1, :]
        a, u = _lru_gates(xc, wa_ref, wx_ref, ba_ref[...], bx_ref[...], lam_ref[...])
        h = a * h + u
        ya_ref[t] = (h * _gelu_tanh(gr_ref[t])).astype(BF16)
    hlast_ref[...] = h
    for j in range(cw - 1):
        buf_ref[j] = rows[t_len + j]


def _rglru_s(xr, gr, st, h0, cw, cb, wa, wx, ba, bx, lam):
    t_len, nb, d = xr.shape
    args = [xr, gr, st, h0, cw, cb, wa, wx, ba, bx, lam]
    return pl.pallas_call(
        _rglru_s_kernel, grid=(1,),
        in_specs=[_const_spec(a.shape) for a in args],
        out_specs=(_const_spec((t_len, nb, d)), _const_spec((nb, d)),
                   _const_spec((cw.shape[0] - 1, nb, d))),
        out_shape=(jax.ShapeDtypeStruct((t_len, nb, d), BF16), jax.ShapeDtypeStruct((nb, d), F32),
                   jax.ShapeDtypeStruct((cw.shape[0] - 1, nb, d), F32)),
        compiler_params=_params(1), name="rglru_sample")(*args)


def _sortable_key(score):
    bits = pltpu.bitcast(score, I32)
    return bits ^ ((bits >> 31) & 0x7FFFFFFF)


def _attn_p_kernel(qT_ref, iqT_ref, iwT_ref, kb_ref, vT_ref, ikw_ref, dtab_ref, o_ref,
                   keys_ref, rhs_ref, qbd_ref, m_ref, acc_ref, x_ref, *, k_top):
    qb = pl.program_id(1)
    t0 = qb * TQ
    n_chunks = (qb + 2) >> 1
    n_far = jnp.maximum(qb - 1, 0) >> 1
    half = LANES // 2

    zeros_half = jnp.zeros((half, TQ), BF16)
    for c in range(N_HEADS // 2):
        for hh in range(2):
            h = 2 * c + hh
            iq_h = iqT_ref[h * IDX_DIM:(h + 1) * IDX_DIM, :]
            rhs_ref[c, :, hh * TQ:(hh + 1) * TQ] = jnp.concatenate([iq_h, zeros_half], axis=0)
            q_h = qT_ref[h * HEAD_DIM:(h + 1) * HEAD_DIM, :]
            parts = [zeros_half, q_h] if hh else [q_h, zeros_half]
            qbd_ref[c, :, hh * TQ:(hh + 1) * TQ] = jnp.concatenate(parts, axis=0)

    w = (iwT_ref[...] * (IDX_HEADS ** -0.5)) * (IDX_DIM ** -0.5)
    s_iota = lax.broadcasted_iota(I32, (TK, TQ), 0)
    t_glob = t0 + lax.broadcasted_iota(I32, (TK, TQ), 1)

    def score_body(j, carry):
        off = pl.multiple_of(j * TK, TK)
        ikc = ikw_ref[pl.ds(off, TK), :]
        score = jnp.zeros((TK, TQ), F32)
        for c in range(N_HEADS // 2):
            dd = _dot(ikc, rhs_ref[c])
            for hh in range(2):
                h = 2 * c + hh
                score = score + jnp.maximum(dd[:, hh * TQ:(hh + 1) * TQ], 0.0) * w[h:h + 1, :]
        key = _sortable_key(score)
        key = jnp.where(s_iota + off <= t_glob, key, INT_MIN)
        keys_ref[pl.ds(off, TK), :] = key
        return carry

    _fori_pairs(0, n_chunks, score_body, 0)

    def count(pred):
        def body(j, acc):
            off = pl.multiple_of(j * TK, TK)
            m = pred(keys_ref[pl.ds(off, TK), :]).astype(I32)
            return acc + m.reshape(TK // 64, 64, TQ).sum(axis=0)
        acc = lax.fori_loop(0, n_chunks, body, jnp.zeros((64, TQ), I32))
        return acc.sum(axis=0, keepdims=True)

    def bit_body(p, thr):
        bit = jnp.left_shift(jnp.int32(1), 31 - p)
        cand = thr ^ bit
        cnt = count(lambda kc: kc >= cand)
        return jnp.where(cnt >= k_top, cand, thr)

    thr = lax.fori_loop(0, 32, bit_body, jnp.full((1, TQ), INT_MIN, I32))
    need = (k_top - count(lambda kc: kc > thr)).astype(F32)

    ltri = (lax.broadcasted_iota(I32, (TK, TK), 1) <= lax.broadcasted_iota(I32, (TK, TK), 0)
            ).astype(BF16)
    fold = lambda a, op: op(a.reshape(TK // 8, 8, TQ), axis=0)

    def logits_body(j, carry, band):
        taken, mx = carry
        off = pl.multiple_of(j * TK, TK)
        keyc = keys_ref[pl.ds(off, TK), :]
        eq = keyc == thr
        prefix = _dot(ltri, eq.astype(BF16))
        sel = (keyc > thr) | (eq & (prefix + taken <= need))
        if band:
            sel = sel & (s_iota + off <= t_glob)
            tab = (t0 - off) // TQ
        kc = kb_ref[pl.ds(off, TK), :]
        new_mx = []
        for c in range(N_HEADS // 2):
            lg = _dot(kc[:, c * LANES:(c + 1) * LANES], qbd_ref[c])
            for hh in range(2):
                h = 2 * c + hh
                x = lg[:, hh * TQ:(hh + 1) * TQ]
                if band:
                    x = x + dtab_ref[tab, h]
                x = jnp.where(sel, x, NEG)
                x_ref[h, pl.ds(off, TK), :] = x
                new_mx.append(jnp.maximum(mx[h * 8:(h + 1) * 8, :], fold(x, jnp.max)))
        return taken + prefix[TK - 1:TK, :], jnp.concatenate(new_mx, axis=0)

    carry = (jnp.zeros((1, TQ), F32), jnp.full((N_HEADS * 8, TQ), NEG, F32))
    carry = _fori_pairs(0, n_far, functools.partial(logits_body, band=False), carry)
    _, mx = lax.fori_loop(n_far, n_chunks, functools.partial(logits_body, band=True), carry)
    m_ref[...] = jnp.concatenate(
        [mx[h * 8:(h + 1) * 8, :].max(axis=0, keepdims=True) for h in range(N_HEADS)], axis=0)

    acc_ref[...] = jnp.zeros(acc_ref.shape, F32)

    def pv_body(j, lsum):
        off = pl.multiple_of(j * TK, TK)
        new_lsum = []
        for h in range(N_HEADS):
            p = jnp.exp(x_ref[h, pl.ds(off, TK), :] - m_ref[h:h + 1, :])
            new_lsum.append(lsum[h * 8:(h + 1) * 8, :] + fold(p, jnp.sum))
            rows = slice(h * HEAD_DIM, (h + 1) * HEAD_DIM)
            acc_ref[rows, :] += _dot(vT_ref[j, rows, :], p.astype(BF16))
        return jnp.concatenate(new_lsum, axis=0)

    lsum = _fori_pairs(0, n_chunks, pv_body, jnp.zeros((N_HEADS * 8, TQ), F32))
    for h in range(N_HEADS):
        rows = slice(h * HEAD_DIM, (h + 1) * HEAD_DIM)
        inv = 1.0 / lsum[h * 8:(h + 1) * 8, :].sum(axis=0, keepdims=True)
        acc_ref[rows, :] = acc_ref[rows, :] * inv
    o_ref[...] = acc_ref[...].T.astype(BF16)


def _attn_p(qT, iqT, iwT, kb, vT, ikwb, dtab, nb, t):
    n = nb * t
    nq = t // TQ
    k_top = min(TOPK_MAX, t // 4)
    colT = lambda r: pl.BlockSpec((r, TQ), lambda b, q: (0, b * nq + q))
    return pl.pallas_call(
        functools.partial(_attn_p_kernel, k_top=k_top), grid=(nb, nq),
        in_specs=[colT(ATT_W), colT(ATT_W), colT(IDX_HEADS),
                  pl.BlockSpec((t, ATT_W), lambda b, q: (b, 0)),
                  pl.BlockSpec((t // TK, ATT_W, TK), lambda b, q: (b, 0, 0)),
                  pl.BlockSpec((t, LANES), lambda b, q: (b, 0)),
                  _const_spec(dtab.shape)],
        out_specs=pl.BlockSpec((TQ, ATT_W), lambda b, q: (b * nq + q, 0)),
        out_shape=jax.ShapeDtypeStruct((n, ATT_W), BF16),
        scratch_shapes=[pltpu.VMEM((t, TQ), I32),
                        pltpu.VMEM((N_HEADS // 2, LANES, 2 * TQ), BF16),
                        pltpu.VMEM((N_HEADS // 2, LANES, 2 * TQ), BF16),
                        pltpu.VMEM((N_HEADS, TQ), F32), pltpu.VMEM((ATT_W, TQ), F32),
                        pltpu.VMEM((N_HEADS, t, TQ), F32)],
        compiler_params=_params(2), name="attn_prompt")(qT, iqT, iwT, kb, vT, ikwb, dtab)


def _select_s_kernel(pt_ref, iq_ref, wcol_ref, iknew_ref, pool_ref, mask_ref,
                     ikbuf_ref, score_ref, sem, *, k_top, layer_off, t_len):
    g = pl.program_id(0)
    gs, _, s_pad = ikbuf_ref.shape
    n_pages = pt_ref.shape[1]
    page = pool_ref.shape[2]
    past = n_pages * page

    def page_copy(i, j):
        return pltpu.make_async_copy(pool_ref.at[layer_off + pt_ref[g * gs + i, j]],
                                     ikbuf_ref.at[i, :, j * page:(j + 1) * page], sem)

    def start_body(i, c):
        for j in range(n_pages):
            page_copy(i, j).start()
        return c

    lax.fori_loop(0, gs, start_body, 0)
    ikbuf_ref[:, :, past:s_pad] = jnp.zeros((gs, IDX_DIM, s_pad - past), F32)
    ikbuf_ref[:, :, past:past + t_len] = iknew_ref[...]

    def wait_body(i, c):
        for j in range(n_pages):
            page_copy(i, j).wait()
        return c

    lax.fori_loop(0, gs, wait_body, 0)

    def score_body(i, c):
        dd = _dot(iq_ref[i], ikbuf_ref[i].astype(BF16))
        wv = (wcol_ref[i] * (IDX_HEADS ** -0.5)) * (IDX_DIM ** -0.5)
        sc = (jnp.maximum(dd, 0.0) * wv).reshape(t_len, IDX_HEADS, s_pad).sum(axis=1)
        score_ref[i, 0:t_len, :] = sc
        return c

    lax.fori_loop(0, gs, score_body, 0)

    tile = score_ref.shape[1]
    rows = gs * tile
    s_idx = lax.broadcasted_iota(I32, (rows, s_pad), 1)
    t_idx = lax.broadcasted_iota(I32, (rows, s_pad), 0) % tile
    valid = (s_idx <= past + t_idx) & (t_idx < t_len)
    keys = jnp.where(valid, _sortable_key(score_ref[...].reshape(rows, s_pad)), INT_MIN)

    def bit_body(p, thr):
        bit = jnp.left_shift(jnp.int32(1), 31 - p)
        cand = thr ^ bit
        cnt = jnp.sum((keys >= cand).astype(I32), axis=1, keepdims=True)
        return jnp.where(cnt >= k_top, cand, thr)

    thr = lax.fori_loop(0, 32, bit_body, jnp.full((rows, 1), INT_MIN, I32))
    gt = keys > thr
    eq = keys == thr
    need = (k_top - jnp.sum(gt.astype(I32), axis=1, keepdims=True)).astype(F32)
    utri = (lax.broadcasted_iota(I32, (LANES, LANES), 0) <= lax.broadcasted_iota(I32, (LANES, LANES), 1)
            ).astype(BF16)
    eqb = eq.astype(BF16)
    carry = jnp.zeros((rows, 1), F32)
    for c in range(s_pad // LANES):
        cols = slice(c * LANES, (c + 1) * LANES)
        prefix = _dot(eqb[:, cols], utri) + carry
        take = gt[:, cols] | (eq[:, cols] & (prefix <= need))
        take = take & valid[:, cols]
        m = jnp.where(take, 0.0, NEG)
        mask_ref[:, :, cols] = m.reshape(gs, tile, LANES)[:, 0:t_len, :]
        carry = prefix[:, LANES - 1:LANES]


def _select_s(page_table, iq, wcol, iknew, pool, layer, gs, k_top):
    nb, rows_q, _ = iq.shape
    t_len = rows_q // IDX_HEADS
    n_pages = page_table.shape[1]
    page = pool.shape[2]
    s_pad = n_pages * page + LANES
    kern = functools.partial(_select_s_kernel, k_top=k_top, layer_off=layer, t_len=t_len)
    grid_spec = pltpu.PrefetchScalarGridSpec(
        num_scalar_prefetch=1, grid=(nb // gs,),
        in_specs=[pl.BlockSpec((gs, rows_q, IDX_DIM), lambda g, pt: (g, 0, 0)),
                  pl.BlockSpec((gs, rows_q, 1), lambda g, pt: (g, 0, 0)),
                  pl.BlockSpec((gs, IDX_DIM, t_len), lambda g, pt: (g, 0, 0)),
                  pl.BlockSpec(memory_space=pl.ANY)],
        out_specs=pl.BlockSpec((gs, t_len, s_pad), lambda g, pt: (g, 0, 0)),
        scratch_shapes=[pltpu.VMEM((gs, IDX_DIM, s_pad), F32),
                        pltpu.VMEM((gs, 8, s_pad), F32),
                        pltpu.SemaphoreType.DMA(())])
    return pl.pallas_call(
        kern, grid_spec=grid_spec,
        out_shape=jax.ShapeDtypeStruct((nb, t_len, s_pad), F32),
        compiler_params=_params(1), name="select_sample")(page_table, iq, wcol, iknew, pool)


def _attn_s_kernel(pt_ref, q_ref, knew_ref, vnew_ref, mask_ref, dtab_ref, *rest, n_pages, t_len):
    k_pages = rest[:n_pages]
    v_pages = rest[n_pages:2 * n_pages]
    o_ref, kpad_ref, vpad_ref = rest[2 * n_pages:]
    rows = t_len * N_HEADS
    page = k_pages[0].shape[2]

    def flat_bf16(page_ref):
        return page_ref[...].reshape(ATT_W, page).astype(BF16)

    def per_head_rows(x):
        return jnp.concatenate(
            [jnp.broadcast_to(x[t:t + 1, :], (N_HEADS, x.shape[1])) for t in range(t_len)], axis=0)

    col_head = lax.broadcasted_iota(I32, (rows, ATT_W), 1) // HEAD_DIM
    row_head = lax.broadcasted_iota(I32, (rows, ATT_W), 0) % N_HEADS
    own = col_head == row_head
    qbd = jnp.where(own, per_head_rows(q_ref[...].astype(F32)), 0.0).astype(BF16)
    kpad_ref[...] = jnp.zeros(kpad_ref.shape, F32)
    vpad_ref[...] = jnp.zeros(vpad_ref.shape, F32)
    kpad_ref[0:t_len, :] = knew_ref[...]
    vpad_ref[0:t_len, :] = vnew_ref[...]
    knew = kpad_ref[...].astype(BF16)
    vnew = vpad_ref[...].astype(BF16)
    logits = [_dot(qbd, flat_bf16(kp)) for kp in k_pages] + [_dot_nt(qbd, knew)]
    x = jnp.concatenate(logits, axis=1)
    x = x + dtab_ref[...] + per_head_rows(mask_ref[...])
    m = x.max(axis=1, keepdims=True)
    p = jnp.exp(x - m)
    l = p.sum(axis=1, keepdims=True)
    pb = p.astype(BF16)
    out = _dot(pb[:, n_pages * page:], vnew)
    for j, vp in enumerate(v_pages):
        out = out + _dot_nt(pb[:, j * page:(j + 1) * page], flat_bf16(vp))
    out = out / l
    out = jnp.where(own, out, 0.0).reshape(t_len, N_HEADS, ATT_W).sum(axis=1)
    o_ref[...] = out.astype(BF16)


def _attn_s(page_table, q, knew, vnew, mask, dtab, k_pool, v_pool, layer_off):
    nb, t_len, _ = q.shape
    n_pages = page_table.shape[1]
    page = k_pool.shape[3]
    s_pad = n_pages * page + LANES

    def page_spec(j):
        return pl.BlockSpec((None, N_HEADS, HEAD_DIM, page),
                            lambda b, pt, j=j: (layer_off + pt[b, j], 0, 0, 0))

    seq = lambda w: pl.BlockSpec((None, t_len, w), lambda b, pt: (b, 0, 0))
    grid_spec = pltpu.PrefetchScalarGridSpec(
        num_scalar_prefetch=1, grid=(nb,),
        in_specs=[seq(ATT_W), seq(ATT_W), seq(ATT_W), seq(s_pad),
                  pl.BlockSpec(dtab.shape, lambda b, pt: (0, 0))]
                 + [page_spec(j) for j in range(n_pages)] * 2,
        out_specs=seq(ATT_W),
        scratch_shapes=[pltpu.VMEM((page, ATT_W), F32), pltpu.VMEM((page, ATT_W), F32)])
    kern = functools.partial(_attn_s_kernel, n_pages=n_pages, t_len=t_len)
    return pl.pallas_call(
        kern, grid_spec=grid_spec, out_shape=jax.ShapeDtypeStruct((nb, t_len, ATT_W), BF16),
        compiler_params=_params(1), name="attn_sample")(
            page_table, q, knew, vnew, mask, dtab, *([k_pool] * n_pages), *([v_pool] * n_pages))


def _merge_kernel(x_ref, ya_ref, o_ref, ga_ref, gb_ref, wr_ref, wa_ref, wo_ref, out_ref):
    a = _dot(ya_ref[...], wr_ref[...])
    b = _dot(o_ref[...], wa_ref[...])
    mix = _sigmoid(ga_ref[...]) * a + _sigmoid(gb_ref[...]) * b
    out_ref[...] = x_ref[...] + _dot(mix.astype(BF16), wo_ref[...])


def _merge(x, ya, o, ga, gb, wr, wa, wo, tm):
    n, d = x.shape
    row = lambda w: pl.BlockSpec((tm, w), lambda i: (i, 0))
    return pl.pallas_call(
        _merge_kernel, grid=(n // tm,),
        in_specs=[row(d), row(d), row(ATT_W), row(d), row(d),
                  _const_spec(wr.shape), _const_spec(wa.shape), _const_spec(wo.shape)],
        out_specs=row(d), out_shape=jax.ShapeDtypeStruct((n, d), F32),
        compiler_params=_params(1), name="merge")(x, ya, o, ga, gb, wr, wa, wo)


def _ffn_kernel(x_ref, p_ref, nf_ref, wg_ref, wu_ref, wd_ref, np_ref, wpg_ref, wpp_ref, nfin_ref,
                out_ref, *, final):
    x = x_ref[...]
    h2 = _rms(x, nf_ref[...]).astype(BF16)
    g = _dot(h2, wg_ref[...])
    u = _dot(h2, wu_ref[...])
    act = (g * _sigmoid(g)) * u
    x = x + _dot(act.astype(BF16), wd_ref[...])
    h3 = _rms(x, np_ref[...]).astype(BF16)
    x = x + _sigmoid(_dot(h3, wpg_ref[...])) * _dot(p_ref[...].astype(BF16), wpp_ref[...])
    if final:
        x = _rms(x, nfin_ref[...])
    out_ref[...] = x


def _ffn(x, p, nf, wg, wu, wd, npl, wpg, wpp, nfin, tm, final):
    n, d = x.shape
    row = lambda w: pl.BlockSpec((tm, w), lambda i: (i, 0))
    consts = [nf, wg, wu, wd, npl, wpg, wpp, nfin]
    return pl.pallas_call(
        functools.partial(_ffn_kernel, final=final), grid=(n // tm,),
        in_specs=[row(d), row(p.shape[1])] + [_const_spec(c.shape) for c in consts],
        out_specs=row(d), out_shape=jax.ShapeDtypeStruct((n, d), F32),
        compiler_params=_params(1), name="ffn_ple")(x, p, *consts)


def _rel_bucket(n, n_buckets):
    max_exact = n_buckets // 2
    nf = jnp.maximum(n, 1).astype(F32)
    large = max_exact + (jnp.log(nf / max_exact) / math.log(MAX_DISTANCE / max_exact)
                         * (n_buckets - max_exact)).astype(I32)
    large = jnp.minimum(large, n_buckets - 1)
    return jnp.where(n < max_exact, n, large)


def _bias_delta(rel_bias, n):
    nbk = rel_bias.shape[0]
    b = rel_bias.astype(F32)
    onehot = _rel_bucket(jnp.maximum(n, 0), nbk)[..., None] == jnp.arange(nbk, dtype=I32)
    picked = jnp.where(onehot[..., None], b, 0.0).sum(axis=-2)
    return picked - b[nbk - 1]


def _block_diag_groups(w, group):
    nbk, c, _ = w.shape
    per = group // c
    wg = w.reshape(nbk // per, per, c, c)
    eye = jnp.eye(per, dtype=w.dtype)
    return jnp.einsum('gpcd,pq->gpcqd', wg, eye).reshape(nbk // per, group, group)


def kernel(x_prompt, x_sample, p_prompt, p_sample, cache_k, cache_v, cache_idx_k, state_rglru_h,
           state_conv, page_table, rel_bias, norm_mix, w_in, conv_w, conv_b, w_rg_a, b_rg_a,
           w_rg_x, b_rg_x, lru_lambda, w_rnn_out, w_att_out, w_o, norm_ffn, w_ffn_gate, w_ffn_up,
           w_ffn_down, norm_ple, w_ple_gate, w_ple_proj, norm_final):
    nb, t, d = x_prompt.shape
    sb, st, _ = x_sample.shape
    depth = w_in.shape[0]
    n_pool, page = cache_k.shape[1], cache_k.shape[2]
    n_pages = page_table.shape[1]
    past = n_pages * page
    cw = conv_w.shape[1]
    s_pad = past + LANES
    k_top_s = min(TOPK_MAX, (past + st) // 4)
    row2 = lambda v: v.reshape(1, -1).astype(F32)

    idx3 = jnp.arange(3, dtype=I32)[:, None, None]
    s_rel = jnp.arange(TK, dtype=I32)[None, :, None]
    t_rel = jnp.arange(TQ, dtype=I32)[None, None, :]
    dtab_p = _bias_delta(rel_bias, idx3 * TQ + t_rel - s_rel).transpose(0, 3, 1, 2)
    s_all = jnp.arange(s_pad, dtype=I32)[None, :]
    q_pos = past + jnp.arange(st, dtype=I32)[:, None]
    dtab_s = _bias_delta(rel_bias, q_pos - s_all)
    dtab_s = dtab_s.transpose(0, 2, 1).reshape(st * N_HEADS, s_pad)

    k_pool = cache_k.transpose(0, 1, 3, 4, 2).reshape(depth * n_pool, N_HEADS, HEAD_DIM, page)
    v_pool = cache_v.transpose(0, 1, 3, 4, 2).reshape(depth * n_pool, N_HEADS, HEAD_DIM, page)
    ik_pool = cache_idx_k.transpose(0, 1, 3, 2).reshape(depth * n_pool, IDX_DIM, page)

    xp = x_prompt.reshape(nb * t, d)
    xs = x_sample.transpose(1, 0, 2).reshape(st * sb, d)
    outs_p, outs_s = [], []
    for l in range(depth):
        splits = [d, 2 * d, 2 * d + ATT_W, 2 * d + 2 * ATT_W, 2 * d + 3 * ATT_W,
                  2 * d + 3 * ATT_W + IDX_HEADS * IDX_DIM,
                  2 * d + 3 * ATT_W + IDX_HEADS * IDX_DIM + IDX_DIM,
                  2 * d + 3 * ATT_W + IDX_HEADS * IDX_DIM + IDX_DIM + IDX_HEADS,
                  3 * d + 3 * ATT_W + IDX_HEADS * IDX_DIM + IDX_DIM + IDX_HEADS]
        w_xr, w_gr, w_q, w_k, w_v, w_iq, w_ik, w_iw, w_ga, w_gb = jnp.split(w_in[l], splits, axis=1)
        w_ikw = jnp.concatenate(
            [w_ik, w_iw, jnp.zeros((d, LANES - IDX_DIM - IDX_HEADS), F32)], axis=1)
        wm_p = jnp.concatenate([w_xr, w_gr, w_ga, w_gb, w_k, w_ikw], axis=1).astype(BF16)
        wm_s = jnp.concatenate([w_xr, w_gr, w_ga, w_gb, w_k, w_v, w_ikw, w_q, w_iq], axis=1).astype(BF16)
        wt_p = jnp.concatenate([w_q, w_iq, w_k, w_v, w_ik, w_iw], axis=1).T.astype(BF16)
        gw = 2 * LANES
        wa_bd = _block_diag_groups(w_rg_a[l], gw).astype(BF16)
        wx_bd = _block_diag_groups(w_rg_x[l], gw).astype(BF16)
        lru_consts = (conv_w[l], row2(conv_b[l]), wa_bd, wx_bd, row2(b_rg_a[l]), row2(b_rg_x[l]),
                      row2(lru_lambda[l]))
        w_r, w_a, w_oo = (w_rnn_out[l].astype(BF16), w_att_out[l].astype(BF16), w_o[l].astype(BF16))
        ffn_consts = (row2(norm_ffn[l]), w_ffn_gate[l].astype(BF16), w_ffn_up[l].astype(BF16),
                      w_ffn_down[l].astype(BF16), row2(norm_ple[l]), w_ple_gate[l].astype(BF16),
                      w_ple_proj[l].astype(BF16), row2(norm_final))
        final = l == depth - 1

        (xr, gr, ga, gb, kb, ikwb, qT, iqT, vTb, iwT, kT, vT, ikT) = _inproj_p(
            xp, row2(norm_mix[l]), wm_p, wt_p, tm=256, nb=nb)
        ya, h_last, new_buf = _rglru_p(xr.reshape(nb, t, d), gr.reshape(nb, t, d), *lru_consts, tt=128)
        o = _attn_p(qT, iqT, iwT, kb, vTb, ikwb, dtab_p, nb, t)
        x1 = _merge(xp, ya.reshape(nb * t, d), o, ga, gb, w_r, w_a, w_oo, tm=512)
        xp = _ffn(x1, p_prompt[l].reshape(nb * t, -1), *ffn_consts, tm=256, final=final)
        heads_last = lambda aT: aT.reshape(N_HEADS, HEAD_DIM, nb, t).transpose(2, 3, 0, 1)
        outs_p.append((heads_last(kT), heads_last(vT), ikT.transpose(0, 2, 1), h_last, new_buf))

        (xr, gr, ga, gb, k, v, ikw, q, iq) = _inproj_s(xs, row2(norm_mix[l]), wm_s, tm=st * sb)
        tmaj = lambda a2: a2.reshape(st, sb, -1)
        ya, h_last, new_buf = _rglru_s(tmaj(xr), tmaj(gr), state_conv[l].transpose(1, 0, 2),
                                       state_rglru_h[l], *lru_consts)
        bmaj = lambda a2: a2.reshape(st, sb, -1).transpose(1, 0, 2)
        k_b, v_b, ikw_b, q_b, iq_b = bmaj(k), bmaj(v), bmaj(ikw), bmaj(q), bmaj(iq)
        iq_rows = iq_b.reshape(sb, st * IDX_HEADS, IDX_DIM)
        w_col = ikw_b[:, :, IDX_DIM:IDX_DIM + IDX_HEADS].reshape(sb, st * IDX_HEADS, 1)
        mask = _select_s(page_table, iq_rows, w_col, ikw_b[:, :, :IDX_DIM].transpose(0, 2, 1),
                         ik_pool, l * n_pool, gs=min(sb, 16), k_top=k_top_s)
        o = _attn_s(page_table, q_b, k_b, v_b, mask, dtab_s, k_pool, v_pool, l * n_pool)
        o = o.transpose(1, 0, 2).reshape(st * sb, ATT_W)
        x1 = _merge(xs, ya.reshape(st * sb, d), o, ga, gb, w_r, w_a, w_oo, tm=st * sb)
        xs = _ffn(x1, p_sample[l].transpose(1, 0, 2).reshape(st * sb, -1), *ffn_consts,
                  tm=min(256, st * sb), final=final)
        outs_s.append((k_b.reshape(sb, st, N_HEADS, HEAD_DIM), v_b.reshape(sb, st, N_HEADS, HEAD_DIM),
                       ikw_b[:, :, :IDX_DIM], h_last, new_buf.transpose(1, 0, 2)))

    y_prompt = xp.reshape(nb, t, d)
    y_sample = xs.reshape(st, sb, d).transpose(1, 0, 2)
    stack = lambda outs, i: jnp.stack([o_[i] for o_ in outs])
    return (y_prompt, y_sample,
            stack(outs_p, 0), stack(outs_p, 1), stack(outs_p, 2), stack(outs_p, 3), stack(outs_p, 4),
            stack(outs_s, 0), stack(outs_s, 1), stack(outs_s, 2), stack(outs_s, 3), stack(outs_s, 4))
```

```python
import functools
import math

import jax
import jax.numpy as jnp
from jax import lax
from jax.experimental import pallas as pl
from jax.experimental.pallas import tpu as pltpu

F32 = jnp.float32
BF16 = jnp.bfloat16
I32 = jnp.int32

N_HEADS = 8
HEAD_DIM = 64
ATT_W = N_HEADS * HEAD_DIM
IDX_HEADS = 8
IDX_DIM = 64
TOPK_MAX = 256
LRU_C = 8.0
MAX_DISTANCE = 128
EPS = 1e-6

LANES = 128
TQ = 128
TK = 256
NEG = -1e30
LOG2E = math.log2(math.e)
INT_MIN = -2147483648
VMEM_LIMIT = 56 * 1024 * 1024


def _dot(a, b):
    return jnp.dot(a, b, preferred_element_type=F32)


def _dot_nt(a, b):
    return lax.dot_general(a, b, (((1,), (1,)), ((), ())), preferred_element_type=F32)


def _sigmoid(x):
    return 0.5 * jnp.tanh(0.5 * x) + 0.5


def _gelu_tanh(x):
    c = math.sqrt(2.0 / math.pi)
    return 0.5 * x * (1.0 + jnp.tanh(c * (x + 0.044715 * (x * x * x))))


def _rms(x, g):
    y = x * lax.rsqrt(jnp.mean(x * x, axis=-1, keepdims=True) + EPS)
    return y * g


def _params(n_axes=1):
    return pltpu.CompilerParams(dimension_semantics=("arbitrary",) * n_axes,
                                vmem_limit_bytes=VMEM_LIMIT)


def _fori_pairs(lo, hi, body, carry):
    n = hi - lo

    def pair(i, c):
        j = lo + 2 * i
        return body(j + 1, body(j, c))

    carry = lax.fori_loop(0, n >> 1, pair, carry)
    return lax.cond((n & 1) == 1, lambda c: body(hi - 1, c), lambda c: c, carry)


def _const_spec(shape):
    nd = len(shape)
    return pl.BlockSpec(shape, lambda *_: (0,) * nd)


def _inproj_p_kernel(x_ref, g_ref, wm_ref, wt_ref,
                     xr_ref, gr_ref, ga_ref, gb_ref, kb_ref, ikwb_ref,
                     qT_ref, iqT_ref, vTb_ref, iwT_ref, kT_ref, vT_ref, ikT_ref):
    d = x_ref.shape[1]
    hb = _rms(x_ref[...], g_ref[...]).astype(BF16)
    z = _dot(hb, wm_ref[...])
    xr_ref[...] = z[:, 0:d]
    gr_ref[...] = z[:, d:2 * d]
    ga_ref[...] = z[:, 2 * d:3 * d]
    gb_ref[...] = z[:, 3 * d:4 * d]
    o = 4 * d
    kb_ref[...] = z[:, o:o + ATT_W].astype(BF16)
    ikwb_ref[...] = z[:, o + ATT_W:o + ATT_W + LANES].astype(BF16)
    zt = _dot_nt(wt_ref[...], hb)
    qT_ref[...] = (zt[0:ATT_W] * (HEAD_DIM ** -0.5)).astype(BF16)
    iqT_ref[...] = zt[ATT_W:2 * ATT_W].astype(BF16)
    kT_ref[...] = zt[2 * ATT_W:3 * ATT_W]
    vt = zt[3 * ATT_W:4 * ATT_W]
    vT_ref[...] = vt
    for c in range(vTb_ref.shape[0]):
        vTb_ref[c] = vt[:, c * TK:(c + 1) * TK].astype(BF16)
    ikT_ref[...] = zt[4 * ATT_W:4 * ATT_W + IDX_DIM]
    iwT_ref[...] = zt[4 * ATT_W + IDX_DIM:4 * ATT_W + IDX_DIM + IDX_HEADS]


def _inproj_p(x, g, wm, wt, tm, nb):
    n, d = x.shape
    t = n // nb
    per_b = t // tm
    row = lambda w: pl.BlockSpec((tm, w), lambda i: (i, 0))
    out_shape = (
        jax.ShapeDtypeStruct((n, d), F32), jax.ShapeDtypeStruct((n, d), F32),
        jax.ShapeDtypeStruct((n, d), F32), jax.ShapeDtypeStruct((n, d), F32),
        jax.ShapeDtypeStruct((n, ATT_W), BF16), jax.ShapeDtypeStruct((n, LANES), BF16),
        jax.ShapeDtypeStruct((ATT_W, n), BF16), jax.ShapeDtypeStruct((ATT_W, n), BF16),
        jax.ShapeDtypeStruct((n // TK, ATT_W, TK), BF16),
        jax.ShapeDtypeStruct((IDX_HEADS, n), F32),
        jax.ShapeDtypeStruct((nb, ATT_W, t), F32), jax.ShapeDtypeStruct((nb, ATT_W, t), F32),
        jax.ShapeDtypeStruct((nb, IDX_DIM, t), F32),
    )
    colT = lambda r: pl.BlockSpec((r, tm), lambda i: (0, i))
    per_seq = lambda r: pl.BlockSpec((None, r, tm), lambda i: (i // per_b, 0, i % per_b))
    out_specs = (row(d), row(d), row(d), row(d), row(ATT_W), row(LANES), colT(ATT_W), colT(ATT_W),
                 pl.BlockSpec((tm // TK, ATT_W, TK), lambda i: (i, 0, 0)), colT(IDX_HEADS),
                 per_seq(ATT_W), per_seq(ATT_W), per_seq(IDX_DIM))
    return pl.pallas_call(
        _inproj_p_kernel, grid=(n // tm,),
        in_specs=[row(d), _const_spec(g.shape), _const_spec(wm.shape), _const_spec(wt.shape)],
        out_specs=out_specs, out_shape=out_shape, compiler_params=_params(1),
        name="inproj_prompt")(x, g, wm, wt)


def _inproj_s_kernel(x_ref, g_ref, wm_ref,
                     xr_ref, gr_ref, ga_ref, gb_ref, k_ref, v_ref, ikw_ref, q_ref, iq_ref):
    d = x_ref.shape[1]
    hb = _rms(x_ref[...], g_ref[...]).astype(BF16)
    z = _dot(hb, wm_ref[...])
    xr_ref[...] = z[:, 0:d]
    gr_ref[...] = z[:, d:2 * d]
    ga_ref[...] = z[:, 2 * d:3 * d]
    gb_ref[...] = z[:, 3 * d:4 * d]
    o = 4 * d
    k_ref[...] = z[:, o:o + ATT_W]
    v_ref[...] = z[:, o + ATT_W:o + 2 * ATT_W]
    ikw_ref[...] = z[:, o + 2 * ATT_W:o + 2 * ATT_W + LANES]
    o2 = o + 2 * ATT_W + LANES
    q_ref[...] = (z[:, o2:o2 + ATT_W] * (HEAD_DIM ** -0.5)).astype(BF16)
    iq_ref[...] = z[:, o2 + ATT_W:o2 + 2 * ATT_W].astype(BF16)


def _inproj_s(x, g, wm, tm):
    n, d = x.shape
    row = lambda w: pl.BlockSpec((tm, w), lambda i: (i, 0))
    out_shape = (
        jax.ShapeDtypeStruct((n, d), F32), jax.ShapeDtypeStruct((n, d), F32),
        jax.ShapeDtypeStruct((n, d), F32), jax.ShapeDtypeStruct((n, d), F32),
        jax.ShapeDtypeStruct((n, ATT_W), F32), jax.ShapeDtypeStruct((n, ATT_W), F32),
        jax.ShapeDtypeStruct((n, LANES), F32),
        jax.ShapeDtypeStruct((n, ATT_W), BF16), jax.ShapeDtypeStruct((n, ATT_W), BF16),
    )
    out_specs = (row(d), row(d), row(d), row(d), row(ATT_W), row(ATT_W), row(LANES),
                 row(ATT_W), row(ATT_W))
    return pl.pallas_call(
        _inproj_s_kernel, grid=(n // tm,),
        in_specs=[row(d), _const_spec(g.shape), _const_spec(wm.shape)],
        out_specs=out_specs, out_shape=out_shape, compiler_params=_params(1),
        name="inproj_sample")(x, g, wm)


def _lru_gates(xc, wa_ref, wx_ref, ba, bx, lam):
    xcb = xc.astype(BF16)
    gw = wa_ref.shape[1]
    r_parts, i_parts = [], []
    for g in range(wa_ref.shape[0]):
        xs = xcb[:, g * gw:(g + 1) * gw]
        r_parts.append(_dot(xs, wa_ref[g]))
        i_parts.append(_dot(xs, wx_ref[g]))
    r = _sigmoid(jnp.concatenate(r_parts, axis=1) + ba)
    i = _sigmoid(jnp.concatenate(i_parts, axis=1) + bx)
    log_sig_lam = jnp.minimum(lam, 0.0) - jnp.log(1.0 + jnp.exp(-jnp.abs(lam)))
    log_a = LRU_C * r * log_sig_lam
    a = jnp.exp(log_a)
    u = jnp.sqrt(1.0 - a * a) * (i * xc)
    return a, u


def _rglru_p_kernel(xr_ref, gr_ref, cw_ref, cb_ref, wa_ref, wx_ref, ba_ref, bx_ref, lam_ref,
                    ya_ref, hlast_ref, buf_ref,
                    xx_ref, a_ref, u_ref, h_ref):
    step = pl.program_id(0)
    nb, tt, d = xr_ref.shape
    cw = cw_ref.shape[0]

    @pl.when(step == 0)
    def _():
        xx_ref[:, 0:8, :] = jnp.zeros((nb, 8, d), F32)
        h_ref[...] = jnp.zeros_like(h_ref)

    for b in range(nb):
        x = xr_ref[b]
        xx_ref[b, 8:8 + tt, :] = x
        xc = cb_ref[...] + x * cw_ref[cw - 1:cw, :]
        for j in range(cw - 1):
            sh = cw - 1 - j
            xc = xc + xx_ref[b, 8 - sh:8 - sh + tt, :] * cw_ref[j:j + 1, :]
        xx_ref[b, 0:8, :] = x[tt - 8:tt, :]
        a, u = _lru_gates(xc, wa_ref, wx_ref, ba_ref[...], bx_ref[...], lam_ref[...])
        a_ref[b] = a
        u_ref[b] = u

    def scan_body(t, hs):
        new = []
        for b in range(nb):
            h = a_ref[b, pl.ds(t, 1), :] * hs[b] + u_ref[b, pl.ds(t, 1), :]
            u_ref[b, pl.ds(t, 1), :] = h
            new.append(h)
        return tuple(new)

    hs = lax.fori_loop(0, tt, scan_body, tuple(h_ref[b:b + 1, :] for b in range(nb)), unroll=8)
    for b in range(nb):
        h_ref[b:b + 1, :] = hs[b]
        ya_ref[b] = (u_ref[b] * _gelu_tanh(gr_ref[b])).astype(BF16)
        buf_ref[b] = xr_ref[b, tt - (cw - 1):tt, :]
    hlast_ref[...] = h_ref[...]


def _rglru_p(xr, gr, cw, cb, wa, wx, ba, bx, lam, tt):
    nb, t, d = xr.shape
    blk = pl.BlockSpec((nb, tt, d), lambda i: (0, i, 0))
    consts = [cw, cb, wa, wx, ba, bx, lam]
    return pl.pallas_call(
        _rglru_p_kernel, grid=(t // tt,),
        in_specs=[blk, blk] + [_const_spec(c.shape) for c in consts],
        out_specs=(blk, _const_spec((nb, d)), _const_spec((nb, cw.shape[0] - 1, d))),
        out_shape=(jax.ShapeDtypeStruct((nb, t, d), BF16), jax.ShapeDtypeStruct((nb, d), F32),
                   jax.ShapeDtypeStruct((nb, cw.shape[0] - 1, d), F32)),
        scratch_shapes=[pltpu.VMEM((nb, tt + 8, d), F32), pltpu.VMEM((nb, tt, d), F32),
                        pltpu.VMEM((nb, tt, d), F32), pltpu.VMEM((nb, d), F32)],
        compiler_params=_params(1), name="rglru_prompt")(xr, gr, *consts)


def _rglru_s_kernel(xr_ref, gr_ref, st_ref, h0_ref, cw_ref, cb_ref, wa_ref, wx_ref, ba_ref,
                    bx_ref, lam_ref, ya_ref, hlast_ref, buf_ref):
    t_len, nb, d = xr_ref.shape
    cw = cw_ref.shape[0]
    rows = [st_ref[j] for j in range(cw - 1)] + [xr_ref[t] for t in range(t_len)]
    h = h0_ref[...]
    for t in range(t_len):
        xc = cb_ref[...] + rows[t + cw - 1] * cw_ref[cw - 1:cw, :]
        for j in range(cw - 1):
            xc = xc + rows[t + j] * cw_ref[j:j + 1, :]
        a, u = _lru_gates(xc, wa_ref, wx_ref, ba_ref[...], bx_ref[...], lam_ref[...])
        h = a * h + u
        ya_ref[t] = (h * _gelu_tanh(gr_ref[t])).astype(BF16)
    hlast_ref[...] = h
    for j in range(cw - 1):
        buf_ref[j] = rows[t_len + j]


def _rglru_s(xr, gr, st, h0, cw, cb, wa, wx, ba, bx, lam):
    t_len, nb, d = xr.shape
    args = [xr, gr, st, h0, cw, cb, wa, wx, ba, bx, lam]
    return pl.pallas_call(
        _rglru_s_kernel, grid=(1,),
        in_specs=[_const_spec(a.shape) for a in args],
        out_specs=(_const_spec((t_len, nb, d)), _const_spec((nb, d)),
                   _const_spec((cw.shape[0] - 1, nb, d))),
        out_shape=(jax.ShapeDtypeStruct((t_len, nb, d), BF16), jax.ShapeDtypeStruct((nb, d), F32),
                   jax.ShapeDtypeStruct((cw.shape[0] - 1, nb, d), F32)),
        compiler_params=_params(1), name="rglru_sample")(*args)


def _sortable_key(score):
    bits = pltpu.bitcast(score, I32)
    return bits ^ ((bits >> 31) & 0x7FFFFFFF)


GROUP = 16


def _sort_network(n):
    pairs = []
    p = 1
    while p < n:
        k = p
        while k >= 1:
            for j in range(k % p, n - k, 2 * k):
                for i in range(min(k, n - j - k)):
                    if (i + j) // (2 * p) == (i + j + k) // (2 * p):
                        pairs.append((i + j, i + j + k))
            k //= 2
        p *= 2
    return pairs


def _count_in_sorted_group(v, cmp):
    one = lambda m, w: jnp.where(m, w, 0)
    m1 = cmp(v(7))
    m2 = cmp(jnp.where(m1, v(11), v(3)))
    m3 = cmp(jnp.where(m1, jnp.where(m2, v(13), v(9)), jnp.where(m2, v(5), v(1))))
    hi = jnp.where(m2, jnp.where(m3, v(14), v(12)), jnp.where(m3, v(10), v(8)))
    lo = jnp.where(m2, jnp.where(m3, v(6), v(4)), jnp.where(m3, v(2), v(0)))
    m4 = cmp(jnp.where(m1, hi, lo))
    m5 = cmp(v(15))
    return one(m1, 8) + one(m2, 4) + one(m3, 2) + one(m4, 1) + one(m5, 1)


def _attn_p_kernel(qT_ref, iqT_ref, iwT_ref, kb_ref, vT_ref, ikw_ref, dtab_ref, o_ref,
                   keys_ref, sorted_ref, rhs_ref, qbd_ref, m_ref, acc_ref, x_ref, *, k_top):
    qb = pl.program_id(1)
    t0 = qb * TQ
    n_chunks = (qb + 2) >> 1
    n_far = jnp.maximum(qb - 1, 0) >> 1
    half = LANES // 2

    zeros_half = jnp.zeros((half, TQ), BF16)
    for c in range(N_HEADS // 2):
        for hh in range(2):
            h = 2 * c + hh
            iq_h = iqT_ref[h * IDX_DIM:(h + 1) * IDX_DIM, :]
            rhs_ref[c, :, hh * TQ:(hh + 1) * TQ] = jnp.concatenate([iq_h, zeros_half], axis=0)
            q_h = qT_ref[h * HEAD_DIM:(h + 1) * HEAD_DIM, :]
            parts = [zeros_half, q_h] if hh else [q_h, zeros_half]
            qbd_ref[c, :, hh * TQ:(hh + 1) * TQ] = jnp.concatenate(parts, axis=0)

    w = (iwT_ref[...] * (IDX_HEADS ** -0.5)) * (IDX_DIM ** -0.5)
    s_iota = lax.broadcasted_iota(I32, (TK, TQ), 0)
    t_glob = t0 + lax.broadcasted_iota(I32, (TK, TQ), 1)

    def score_body(j, carry):
        off = pl.multiple_of(j * TK, TK)
        ikc = ikw_ref[pl.ds(off, TK), :]
        score = jnp.zeros((TK, TQ), F32)
        for c in range(N_HEADS // 2):
            dd = _dot(ikc, rhs_ref[c])
            for hh in range(2):
                h = 2 * c + hh
                score = score + jnp.maximum(dd[:, hh * TQ:(hh + 1) * TQ], 0.0) * w[h:h + 1, :]
        key = _sortable_key(score)
        key = jnp.where(s_iota + off <= t_glob, key, INT_MIN)
        keys_ref[pl.ds(off, TK), :] = key
        per = TK // GROUP
        v = [key[i * per:(i + 1) * per, :] for i in range(GROUP)]
        for a, b in _sort_network(GROUP):
            v[a], v[b] = jnp.maximum(v[a], v[b]), jnp.minimum(v[a], v[b])
        for i in range(GROUP):
            sorted_ref[j, i] = v[i]
        return carry

    _fori_pairs(0, n_chunks, score_body, 0)

    def count(cmp):
        def body(j, acc):
            return acc + _count_in_sorted_group(lambda i: sorted_ref[j, i], cmp)
        acc = _fori_pairs(0, n_chunks, body, jnp.zeros((TK // GROUP, TQ), I32))
        return acc.sum(axis=0, keepdims=True)

    def bit_body(p, thr):
        bit = jnp.left_shift(jnp.int32(1), 31 - p)
        cand = thr ^ bit
        cnt = count(lambda kc: kc >= cand)
        return jnp.where(cnt >= k_top, cand, thr)

    thr = lax.fori_loop(0, 32, bit_body, jnp.full((1, TQ), INT_MIN, I32))
    need = (k_top - count(lambda kc: kc > thr)).astype(F32)

    ltri = (lax.broadcasted_iota(I32, (TK, TK), 1) <= lax.broadcasted_iota(I32, (TK, TK), 0)
            ).astype(BF16)
    fold = lambda a, op: op(a.reshape(TK // 8, 8, TQ), axis=0)

    def logits_body(j, carry, band):
        taken, mx = carry
        off = pl.multiple_of(j * TK, TK)
        keyc = keys_ref[pl.ds(off, TK), :]
        eq = keyc == thr
        prefix = _dot(ltri, eq.astype(BF16))
        sel = (keyc > thr) | (eq & (prefix + taken <= need))
        if band:
            sel = sel & (s_iota + off <= t_glob)
            tab = (t0 - off) // TQ
        kc = kb_ref[pl.ds(off, TK), :]
        new_mx = []
        for c in range(N_HEADS // 2):
            lg = _dot(kc[:, c * LANES:(c + 1) * LANES], qbd_ref[c])
            for hh in range(2):
                h = 2 * c + hh
                x = lg[:, hh * TQ:(hh + 1) * TQ]
                if band:
                    x = x + dtab_ref[tab, h]
                x = jnp.where(sel, x * LOG2E, NEG)
                x_ref[h, pl.ds(off, TK), :] = x
                new_mx.append(jnp.maximum(mx[h * 8:(h + 1) * 8, :], fold(x, jnp.max)))
        return taken + prefix[TK - 1:TK, :], jnp.concatenate(new_mx, axis=0)

    carry = (jnp.zeros((1, TQ), F32), jnp.full((N_HEADS * 8, TQ), NEG, F32))
    carry = _fori_pairs(0, n_far, functools.partial(logits_body, band=False), carry)
    _, mx = lax.fori_loop(n_far, n_chunks, functools.partial(logits_body, band=True), carry)
    m_ref[...] = jnp.concatenate(
        [mx[h * 8:(h + 1) * 8, :].max(axis=0, keepdims=True) for h in range(N_HEADS)], axis=0)

    acc_ref[...] = jnp.zeros(acc_ref.shape, F32)

    def pv_body(j, lsum):
        off = pl.multiple_of(j * TK, TK)
        new_lsum = []
        for h in range(N_HEADS):
            p = jnp.exp2(x_ref[h, pl.ds(off, TK), :] - m_ref[h:h + 1, :])
            new_lsum.append(lsum[h * 8:(h + 1) * 8, :] + fold(p, jnp.sum))
            rows = slice(h * HEAD_DIM, (h + 1) * HEAD_DIM)
            acc_ref[rows, :] += _dot(vT_ref[j, rows, :], p.astype(BF16))
        return jnp.concatenate(new_lsum, axis=0)

    lsum = _fori_pairs(0, n_chunks, pv_body, jnp.zeros((N_HEADS * 8, TQ), F32))
    for h in range(N_HEADS):
        rows = slice(h * HEAD_DIM, (h + 1) * HEAD_DIM)
        inv = 1.0 / lsum[h * 8:(h + 1) * 8, :].sum(axis=0, keepdims=True)
        acc_ref[rows, :] = acc_ref[rows, :] * inv
    o_ref[...] = acc_ref[...].T.astype(BF16)


def _attn_p(qT, iqT, iwT, kb, vT, ikwb, dtab, nb, t):
    n = nb * t
    nq = t // TQ
    k_top = min(TOPK_MAX, t // 4)
    colT = lambda r: pl.BlockSpec((r, TQ), lambda b, q: (0, b * nq + q))
    return pl.pallas_call(
        functools.partial(_attn_p_kernel, k_top=k_top), grid=(nb, nq),
        in_specs=[colT(ATT_W), colT(ATT_W), colT(IDX_HEADS),
                  pl.BlockSpec((t, ATT_W), lambda b, q: (b, 0)),
                  pl.BlockSpec((t // TK, ATT_W, TK), lambda b, q: (b, 0, 0)),
                  pl.BlockSpec((t, LANES), lambda b, q: (b, 0)),
                  _const_spec(dtab.shape)],
        out_specs=pl.BlockSpec((TQ, ATT_W), lambda b, q: (b * nq + q, 0)),
        out_shape=jax.ShapeDtypeStruct((n, ATT_W), BF16),
        scratch_shapes=[pltpu.VMEM((t, TQ), I32),
                        pltpu.VMEM((t // TK, GROUP, TK // GROUP, TQ), I32),
                        pltpu.VMEM((N_HEADS // 2, LANES, 2 * TQ), BF16),
                        pltpu.VMEM((N_HEADS // 2, LANES, 2 * TQ), BF16),
                        pltpu.VMEM((N_HEADS, TQ), F32), pltpu.VMEM((ATT_W, TQ), F32),
                        pltpu.VMEM((N_HEADS, t, TQ), F32)],
        compiler_params=_params(2), name="attn_prompt")(qT, iqT, iwT, kb, vT, ikwb, dtab)


def _select_s_kernel(pt_ref, iq_ref, wcol_ref, iknew_ref, pool_ref, mask_ref,
                     ikbuf_ref, score_ref, sem, *, k_top, layer_off, t_len):
    g = pl.program_id(0)
    gs, _, s_pad = ikbuf_ref.shape
    n_pages = pt_ref.shape[1]
    page = pool_ref.shape[2]
    past = n_pages * page

    def page_copy(i, j):
        return pltpu.make_async_copy(pool_ref.at[layer_off + pt_ref[g * gs + i, j]],
                                     ikbuf_ref.at[i, :, j * page:(j + 1) * page], sem)

    def start_body(i, c):
        for j in range(n_pages):
            page_copy(i, j).start()
        return c

    lax.fori_loop(0, gs, start_body, 0)
    ikbuf_ref[:, :, past:s_pad] = jnp.zeros((gs, IDX_DIM, s_pad - past), F32)
    ikbuf_ref[:, :, past:past + t_len] = iknew_ref[...]

    def wait_body(i, c):
        for j in range(n_pages):
            page_copy(i, j).wait()
        return c

    lax.fori_loop(0, gs, wait_body, 0)

    def score_body(i, c):
        dd = _dot(iq_ref[i], ikbuf_ref[i].astype(BF16))
        wv = (wcol_ref[i] * (IDX_HEADS ** -0.5)) * (IDX_DIM ** -0.5)
        sc = (jnp.maximum(dd, 0.0) * wv).reshape(t_len, IDX_HEADS, s_pad).sum(axis=1)
        score_ref[i, 0:t_len, :] = sc
        return c

    lax.fori_loop(0, gs, score_body, 0)

    tile = score_ref.shape[1]
    rows = gs * tile
    s_idx = lax.broadcasted_iota(I32, (rows, s_pad), 1)
    t_idx = lax.broadcasted_iota(I32, (rows, s_pad), 0) % tile
    valid = (s_idx <= past + t_idx) & (t_idx < t_len)
    keys = jnp.where(valid, _sortable_key(score_ref[...].reshape(rows, s_pad)), INT_MIN)

    def bit_body(p, thr):
        bit = jnp.left_shift(jnp.int32(1), 31 - p)
        cand = thr ^ bit
        cnt = jnp.sum((keys >= cand).astype(I32), axis=1, keepdims=True)
        return jnp.where(cnt >= k_top, cand, thr)

    thr = lax.fori_loop(0, 32, bit_body, jnp.full((rows, 1), INT_MIN, I32))
    gt = keys > thr
    eq = keys == thr
    need = (k_top - jnp.sum(gt.astype(I32), axis=1, keepdims=True)).astype(F32)
    utri = (lax.broadcasted_iota(I32, (LANES, LANES), 0) <= lax.broadcasted_iota(I32, (LANES, LANES), 1)
            ).astype(BF16)
    eqb = eq.astype(BF16)
    carry = jnp.zeros((rows, 1), F32)
    for c in range(s_pad // LANES):
        cols = slice(c * LANES, (c + 1) * LANES)
        prefix = _dot(eqb[:, cols], utri) + carry
        take = gt[:, cols] | (eq[:, cols] & (prefix <= need))
        take = take & valid[:, cols]
        m = jnp.where(take, 0.0, NEG)
        mask_ref[:, :, cols] = m.reshape(gs, tile, LANES)[:, 0:t_len, :]
        carry = prefix[:, LANES - 1:LANES]


def _select_s(page_table, iq, wcol, iknew, pool, layer, gs, k_top):
    nb, rows_q, _ = iq.shape
    t_len = rows_q // IDX_HEADS
    n_pages = page_table.shape[1]
    page = pool.shape[2]
    s_pad = n_pages * page + LANES
    kern = functools.partial(_select_s_kernel, k_top=k_top, layer_off=layer, t_len=t_len)
    grid_spec = pltpu.PrefetchScalarGridSpec(
        num_scalar_prefetch=1, grid=(nb // gs,),
        in_specs=[pl.BlockSpec((gs, rows_q, IDX_DIM), lambda g, pt: (g, 0, 0)),
                  pl.BlockSpec((gs, rows_q, 1), lambda g, pt: (g, 0, 0)),
                  pl.BlockSpec((gs, IDX_DIM, t_len), lambda g, pt: (g, 0, 0)),
                  pl.BlockSpec(memory_space=pl.ANY)],
        out_specs=pl.BlockSpec((gs, t_len, s_pad), lambda g, pt: (g, 0, 0)),
        scratch_shapes=[pltpu.VMEM((gs, IDX_DIM, s_pad), F32),
                        pltpu.VMEM((gs, 8, s_pad), F32),
                        pltpu.SemaphoreType.DMA(())])
    return pl.pallas_call(
        kern, grid_spec=grid_spec,
        out_shape=jax.ShapeDtypeStruct((nb, t_len, s_pad), F32),
        compiler_params=_params(1), name="select_sample")(page_table, iq, wcol, iknew, pool)


def _attn_s_kernel(pt_ref, q_ref, knew_ref, vnew_ref, mask_ref, dtab_ref, *rest, n_pages, t_len):
    k_pages = rest[:n_pages]
    v_pages = rest[n_pages:2 * n_pages]
    o_ref, kpad_ref, vpad_ref = rest[2 * n_pages:]
    rows = t_len * N_HEADS
    page = k_pages[0].shape[2]

    def flat_bf16(page_ref):
        return page_ref[...].reshape(ATT_W, page).astype(BF16)

    def per_head_rows(x):
        return jnp.concatenate(
            [jnp.broadcast_to(x[t:t + 1, :], (N_HEADS, x.shape[1])) for t in range(t_len)], axis=0)

    col_head = lax.broadcasted_iota(I32, (rows, ATT_W), 1) // HEAD_DIM
    row_head = lax.broadcasted_iota(I32, (rows, ATT_W), 0) % N_HEADS
    own = col_head == row_head
    qbd = jnp.where(own, per_head_rows(q_ref[...].astype(F32)), 0.0).astype(BF16)
    kpad_ref[...] = jnp.zeros(kpad_ref.shape, F32)
    vpad_ref[...] = jnp.zeros(vpad_ref.shape, F32)
    kpad_ref[0:t_len, :] = knew_ref[...]
    vpad_ref[0:t_len, :] = vnew_ref[...]
    knew = kpad_ref[...].astype(BF16)
    vnew = vpad_ref[...].astype(BF16)
    logits = [_dot(qbd, flat_bf16(kp)) for kp in k_pages] + [_dot_nt(qbd, knew)]
    x = jnp.concatenate(logits, axis=1)
    x = x + dtab_ref[...] + per_head_rows(mask_ref[...])
    m = x.max(axis=1, keepdims=True)
    p = jnp.exp(x - m)
    l = p.sum(axis=1, keepdims=True)
    pb = p.astype(BF16)
    out = _dot(pb[:, n_pages * page:], vnew)
    for j, vp in enumerate(v_pages):
        out = out + _dot_nt(pb[:, j * page:(j + 1) * page], flat_bf16(vp))
    out = out / l
    out = jnp.where(own, out, 0.0).reshape(t_len, N_HEADS, ATT_W).sum(axis=1)
    o_ref[...] = out.astype(BF16)


def _attn_s(page_table, q, knew, vnew, mask, dtab, k_pool, v_pool, layer_off):
    nb, t_len, _ = q.shape
    n_pages = page_table.shape[1]
    page = k_pool.shape[3]
    s_pad = n_pages * page + LANES

    def page_spec(j):
        return pl.BlockSpec((None, N_HEADS, HEAD_DIM, page),
                            lambda b, pt, j=j: (layer_off + pt[b, j], 0, 0, 0))

    seq = lambda w: pl.BlockSpec((None, t_len, w), lambda b, pt: (b, 0, 0))
    grid_spec = pltpu.PrefetchScalarGridSpec(
        num_scalar_prefetch=1, grid=(nb,),
        in_specs=[seq(ATT_W), seq(ATT_W), seq(ATT_W), seq(s_pad),
                  pl.BlockSpec(dtab.shape, lambda b, pt: (0, 0))]
                 + [page_spec(j) for j in range(n_pages)] * 2,
        out_specs=seq(ATT_W),
        scratch_shapes=[pltpu.VMEM((page, ATT_W), F32), pltpu.VMEM((page, ATT_W), F32)])
    kern = functools.partial(_attn_s_kernel, n_pages=n_pages, t_len=t_len)
    return pl.pallas_call(
        kern, grid_spec=grid_spec, out_shape=jax.ShapeDtypeStruct((nb, t_len, ATT_W), BF16),
        compiler_params=_params(1), name="attn_sample")(
            page_table, q, knew, vnew, mask, dtab, *([k_pool] * n_pages), *([v_pool] * n_pages))


def _merge_kernel(x_ref, ya_ref, o_ref, ga_ref, gb_ref, wr_ref, wa_ref, wo_ref, out_ref):
    a = _dot(ya_ref[...], wr_ref[...])
    b = _dot(o_ref[...], wa_ref[...])
    mix = _sigmoid(ga_ref[...]) * a + _sigmoid(gb_ref[...]) * b
    out_ref[...] = x_ref[...] + _dot(mix.astype(BF16), wo_ref[...])


def _layer_spec(w, l):
    tail = (0,) * (w.ndim - 1)
    return pl.BlockSpec((None,) + w.shape[1:], lambda *_: (l,) + tail)


def _merge(x, ya, o, ga, gb, wr, wa, wo, tm, l):
    n, d = x.shape
    row = lambda w: pl.BlockSpec((tm, w), lambda i: (i, 0))
    return pl.pallas_call(
        _merge_kernel, grid=(n // tm,),
        in_specs=[row(d), row(d), row(ATT_W), row(d), row(d),
                  _layer_spec(wr, l), _layer_spec(wa, l), _layer_spec(wo, l)],
        out_specs=row(d), out_shape=jax.ShapeDtypeStruct((n, d), F32),
        compiler_params=_params(1), name="merge")(x, ya, o, ga, gb, wr, wa, wo)


def _ffn_kernel(x_ref, p_ref, nf_ref, wg_ref, wu_ref, wd_ref, np_ref, wpg_ref, wpp_ref, nfin_ref,
                out_ref, *, final):
    x = x_ref[...]
    h2 = _rms(x, nf_ref[...]).astype(BF16)
    g = _dot(h2, wg_ref[...])
    u = _dot(h2, wu_ref[...])
    act = (g * _sigmoid(g)) * u
    x = x + _dot(act.astype(BF16), wd_ref[...])
    h3 = _rms(x, np_ref[...]).astype(BF16)
    x = x + _sigmoid(_dot(h3, wpg_ref[...])) * _dot(p_ref[...].astype(BF16), wpp_ref[...])
    if final:
        x = _rms(x, nfin_ref[...])
    out_ref[...] = x


def _ffn(x, p, nf, wg, wu, wd, npl, wpg, wpp, nfin, tm, final, l, p_row0=0):
    n, d = x.shape
    row = lambda w: pl.BlockSpec((tm, w), lambda i: (i, 0))
    p_blk0 = p_row0 // tm
    consts = [nf, wg, wu, wd, npl, wpg, wpp, nfin]
    return pl.pallas_call(
        functools.partial(_ffn_kernel, final=final), grid=(n // tm,),
        in_specs=[row(d), pl.BlockSpec((tm, p.shape[1]), lambda i: (i + p_blk0, 0))]
                 + [_layer_spec(c, l) if c.ndim == 3 else _const_spec(c.shape) for c in consts],
        out_specs=row(d), out_shape=jax.ShapeDtypeStruct((n, d), F32),
        compiler_params=_params(1), name="ffn_ple")(x, p, *consts)


def _rel_bucket(n, n_buckets):
    max_exact = n_buckets // 2
    nf = jnp.maximum(n, 1).astype(F32)
    large = max_exact + (jnp.log(nf / max_exact) / math.log(MAX_DISTANCE / max_exact)
                         * (n_buckets - max_exact)).astype(I32)
    large = jnp.minimum(large, n_buckets - 1)
    return jnp.where(n < max_exact, n, large)


def _bias_delta(rel_bias, n):
    nbk = rel_bias.shape[0]
    b = rel_bias.astype(F32)
    onehot = _rel_bucket(jnp.maximum(n, 0), nbk)[..., None] == jnp.arange(nbk, dtype=I32)
    picked = jnp.where(onehot[..., None], b, 0.0).sum(axis=-2)
    return picked - b[nbk - 1]


def _block_diag_groups(w, group):
    nbk, c, _ = w.shape
    per = group // c
    wg = w.reshape(nbk // per, per, c, c)
    eye = jnp.eye(per, dtype=w.dtype)
    return jnp.einsum('gpcd,pq->gpcqd', wg, eye).reshape(nbk // per, group, group)


def kernel(x_prompt, x_sample, p_prompt, p_sample, cache_k, cache_v, cache_idx_k, state_rglru_h,
           state_conv, page_table, rel_bias, norm_mix, w_in, conv_w, conv_b, w_rg_a, b_rg_a,
           w_rg_x, b_rg_x, lru_lambda, w_rnn_out, w_att_out, w_o, norm_ffn, w_ffn_gate, w_ffn_up,
           w_ffn_down, norm_ple, w_ple_gate, w_ple_proj, norm_final):
    nb, t, d = x_prompt.shape
    sb, st, _ = x_sample.shape
    depth = w_in.shape[0]
    n_pool, page = cache_k.shape[1], cache_k.shape[2]
    n_pages = page_table.shape[1]
    past = n_pages * page
    cw = conv_w.shape[1]
    s_pad = past + LANES
    k_top_s = min(TOPK_MAX, (past + st) // 4)
    row2 = lambda v: v.reshape(1, -1).astype(F32)

    idx3 = jnp.arange(3, dtype=I32)[:, None, None]
    s_rel = jnp.arange(TK, dtype=I32)[None, :, None]
    t_rel = jnp.arange(TQ, dtype=I32)[None, None, :]
    dtab_p = _bias_delta(rel_bias, idx3 * TQ + t_rel - s_rel).transpose(0, 3, 1, 2)
    s_all = jnp.arange(s_pad, dtype=I32)[None, :]
    q_pos = past + jnp.arange(st, dtype=I32)[:, None]
    dtab_s = _bias_delta(rel_bias, q_pos - s_all)
    dtab_s = dtab_s.transpose(0, 2, 1).reshape(st * N_HEADS, s_pad)

    k_pool = cache_k.transpose(0, 1, 3, 4, 2).reshape(depth * n_pool, N_HEADS, HEAD_DIM, page)
    v_pool = cache_v.transpose(0, 1, 3, 4, 2).reshape(depth * n_pool, N_HEADS, HEAD_DIM, page)
    ik_pool = cache_idx_k.transpose(0, 1, 3, 2).reshape(depth * n_pool, IDX_DIM, page)

    w_r, w_a, w_oo = w_rnn_out.astype(BF16), w_att_out.astype(BF16), w_o.astype(BF16)
    w_fg, w_fu, w_fd = w_ffn_gate.astype(BF16), w_ffn_up.astype(BF16), w_ffn_down.astype(BF16)
    w_pg, w_pp = w_ple_gate.astype(BF16), w_ple_proj.astype(BF16)
    p_rows = p_prompt.reshape(depth * nb * t, -1)

    xp = x_prompt.reshape(nb * t, d)
    xs = x_sample.transpose(1, 0, 2).reshape(st * sb, d)
    outs_p, outs_s = [], []
    for l in range(depth):
        splits = [d, 2 * d, 2 * d + ATT_W, 2 * d + 2 * ATT_W, 2 * d + 3 * ATT_W,
                  2 * d + 3 * ATT_W + IDX_HEADS * IDX_DIM,
                  2 * d + 3 * ATT_W + IDX_HEADS * IDX_DIM + IDX_DIM,
                  2 * d + 3 * ATT_W + IDX_HEADS * IDX_DIM + IDX_DIM + IDX_HEADS,
                  3 * d + 3 * ATT_W + IDX_HEADS * IDX_DIM + IDX_DIM + IDX_HEADS]
        w_xr, w_gr, w_q, w_k, w_v, w_iq, w_ik, w_iw, w_ga, w_gb = jnp.split(w_in[l], splits, axis=1)
        w_ikw = jnp.concatenate(
            [w_ik, w_iw, jnp.zeros((d, LANES - IDX_DIM - IDX_HEADS), F32)], axis=1)
        wm_p = jnp.concatenate([w_xr, w_gr, w_ga, w_gb, w_k, w_ikw], axis=1).astype(BF16)
        wm_s = jnp.concatenate([w_xr, w_gr, w_ga, w_gb, w_k, w_v, w_ikw, w_q, w_iq], axis=1).astype(BF16)
        wt_p = jnp.concatenate([w_q, w_iq, w_k, w_v, w_ik, w_iw], axis=1).T.astype(BF16)
        gw = 2 * LANES
        wa_bd = _block_diag_groups(w_rg_a[l], gw).astype(BF16)
        wx_bd = _block_diag_groups(w_rg_x[l], gw).astype(BF16)
        lru_consts = (conv_w[l], row2(conv_b[l]), wa_bd, wx_bd, row2(b_rg_a[l]), row2(b_rg_x[l]),
                      row2(lru_lambda[l]))
        ffn_consts = (row2(norm_ffn[l]), w_fg, w_fu, w_fd, row2(norm_ple[l]), w_pg, w_pp,
                      row2(norm_final))
        final = l == depth - 1

        (xr, gr, ga, gb, kb, ikwb, qT, iqT, vTb, iwT, kT, vT, ikT) = _inproj_p(
            xp, row2(norm_mix[l]), wm_p, wt_p, tm=256, nb=nb)
        ya, h_last, new_buf = _rglru_p(xr.reshape(nb, t, d), gr.reshape(nb, t, d), *lru_consts, tt=128)
        o = _attn_p(qT, iqT, iwT, kb, vTb, ikwb, dtab_p, nb, t)
        x1 = _merge(xp, ya.reshape(nb * t, d), o, ga, gb, w_r, w_a, w_oo, tm=512, l=l)
        xp = _ffn(x1, p_rows, *ffn_consts, tm=256, final=final, l=l, p_row0=l * nb * t)
        heads_last = lambda aT: aT.reshape(nb, N_HEADS, HEAD_DIM, t).transpose(0, 3, 1, 2)
        outs_p.append((heads_last(kT), heads_last(vT), ikT.transpose(0, 2, 1), h_last, new_buf))

        (xr, gr, ga, gb, k, v, ikw, q, iq) = _inproj_s(xs, row2(norm_mix[l]), wm_s, tm=st * sb)
        tmaj = lambda a2: a2.reshape(st, sb, -1)
        ya, h_last, new_buf = _rglru_s(tmaj(xr), tmaj(gr), state_conv[l].transpose(1, 0, 2),
                                       state_rglru_h[l], *lru_consts)
        bmaj = lambda a2: a2.reshape(st, sb, -1).transpose(1, 0, 2)
        k_b, v_b, ikw_b, q_b, iq_b = bmaj(k), bmaj(v), bmaj(ikw), bmaj(q), bmaj(iq)
        iq_rows = iq_b.reshape(sb, st * IDX_HEADS, IDX_DIM)
        w_col = ikw_b[:, :, IDX_DIM:IDX_DIM + IDX_HEADS].reshape(sb, st * IDX_HEADS, 1)
        mask = _select_s(page_table, iq_rows, w_col, ikw_b[:, :, :IDX_DIM].transpose(0, 2, 1),
                         ik_pool, l * n_pool, gs=min(sb, 16), k_top=k_top_s)
        o = _attn_s(page_table, q_b, k_b, v_b, mask, dtab_s, k_pool, v_pool, l * n_pool)
        o = o.transpose(1, 0, 2).reshape(st * sb, ATT_W)
        x1 = _merge(xs, ya.reshape(st * sb, d), o, ga, gb, w_r, w_a, w_oo, tm=st * sb, l=l)
        xs = _ffn(x1, p_sample[l].transpose(1, 0, 2).reshape(st * sb, -1), *ffn_consts,
                  tm=min(256, st * sb), final=final, l=l)
        outs_s.append((k_b.reshape(sb, st, N_HEADS, HEAD_DIM), v_b.reshape(sb, st, N_HEADS, HEAD_DIM),
                       ikw_b[:, :, :IDX_DIM], h_last, new_buf.transpose(1, 0, 2)))

    y_prompt = xp.reshape(nb, t, d)
    y_sample = xs.reshape(st, sb, d).transpose(1, 0, 2)
    stack = lambda outs, i: jnp.stack([o_[i] for o_ in outs])
    return (y_prompt, y_sample,
            stack(outs_p, 0), stack(outs_p, 1), stack(outs_p, 2), stack(outs_p, 3), stack(outs_p, 4),
            stack(outs_s, 0), stack(outs_s, 1), stack(outs_s, 2), stack(outs_s, 3), stack(outs_s, 4))
```

```python
import functools
import math

import jax
import jax.numpy as jnp
from jax import lax
from jax.experimental import pallas as pl
from jax.experimental.pallas import tpu as pltpu

F32 = jnp.float32
BF16 = jnp.bfloat16
I32 = jnp.int32

N_HEADS = 8
HEAD_DIM = 64
ATT_W = N_HEADS * HEAD_DIM
IDX_HEADS = 8
IDX_DIM = 64
TOPK_MAX = 256
LRU_C = 8.0
MAX_DISTANCE = 128
EPS = 1e-6

LANES = 128
TQ = 128
TK = 256
NEG = -1e30
LOG2E = math.log2(math.e)
INT_MIN = -2147483648
VMEM_LIMIT = 56 * 1024 * 1024


def _dot(a, b):
    return jnp.dot(a, b, preferred_element_type=F32)


def _dot_nt(a, b):
    return lax.dot_general(a, b, (((1,), (1,)), ((), ())), preferred_element_type=F32)


def _sigmoid(x):
    return 0.5 * jnp.tanh(0.5 * x) + 0.5


def _gelu_tanh(x):
    c = math.sqrt(2.0 / math.pi)
    return 0.5 * x * (1.0 + jnp.tanh(c * (x + 0.044715 * (x * x * x))))


def _rms(x, g):
    y = x * lax.rsqrt(jnp.mean(x * x, axis=-1, keepdims=True) + EPS)
    return y * g


def _params(n_axes=1):
    return pltpu.CompilerParams(dimension_semantics=("arbitrary",) * n_axes,
                                vmem_limit_bytes=VMEM_LIMIT)


def _fori_pairs(lo, hi, body, carry):
    n = hi - lo

    def pair(i, c):
        j = lo + 2 * i
        return body(j + 1, body(j, c))

    carry = lax.fori_loop(0, n >> 1, pair, carry)
    return lax.cond((n & 1) == 1, lambda c: body(hi - 1, c), lambda c: c, carry)


def _const_spec(shape):
    nd = len(shape)
    return pl.BlockSpec(shape, lambda *_: (0,) * nd, pipeline_mode=pl.Buffered(1))


def _whole_out_spec(shape):
    nd = len(shape)
    return pl.BlockSpec(shape, lambda *_: (0,) * nd)


def _inproj_p_kernel(x_ref, g_ref, wm_ref, wt_ref,
                     xr_ref, gr_ref, ga_ref, gb_ref, kb_ref, ikwb_ref,
                     qT_ref, iqT_ref, vTb_ref, iwT_ref, kT_ref, vT_ref, ikT_ref):
    d = x_ref.shape[1]
    hb = _rms(x_ref[...], g_ref[...]).astype(BF16)
    z = _dot(hb, wm_ref[...])
    xr_ref[...] = z[:, 0:d]
    gr_ref[...] = z[:, d:2 * d]
    ga_ref[...] = z[:, 2 * d:3 * d]
    gb_ref[...] = z[:, 3 * d:4 * d]
    o = 4 * d
    kb_ref[...] = z[:, o:o + ATT_W].astype(BF16)
    ikwb_ref[...] = z[:, o + ATT_W:o + ATT_W + LANES].astype(BF16)
    zt = _dot_nt(wt_ref[...], hb)
    qT_ref[...] = (zt[0:ATT_W] * (HEAD_DIM ** -0.5)).astype(BF16)
    iqT_ref[...] = zt[ATT_W:2 * ATT_W].astype(BF16)
    kT_ref[...] = zt[2 * ATT_W:3 * ATT_W]
    vt = zt[3 * ATT_W:4 * ATT_W]
    vT_ref[...] = vt
    for c in range(vTb_ref.shape[0]):
        vTb_ref[c] = vt[:, c * TK:(c + 1) * TK].astype(BF16)
    ikT_ref[...] = zt[4 * ATT_W:4 * ATT_W + IDX_DIM]
    iwT_ref[...] = zt[4 * ATT_W + IDX_DIM:4 * ATT_W + IDX_DIM + IDX_HEADS]


def _inproj_p(x, g, wm, wt, tm, nb):
    n, d = x.shape
    t = n // nb
    per_b = t // tm
    row = lambda w: pl.BlockSpec((tm, w), lambda i: (i, 0))
    out_shape = (
        jax.ShapeDtypeStruct((n, d), F32), jax.ShapeDtypeStruct((n, d), F32),
        jax.ShapeDtypeStruct((n, d), F32), jax.ShapeDtypeStruct((n, d), F32),
        jax.ShapeDtypeStruct((n, ATT_W), BF16), jax.ShapeDtypeStruct((n, LANES), BF16),
        jax.ShapeDtypeStruct((ATT_W, n), BF16), jax.ShapeDtypeStruct((ATT_W, n), BF16),
        jax.ShapeDtypeStruct((n // TK, ATT_W, TK), BF16),
        jax.ShapeDtypeStruct((IDX_HEADS, n), F32),
        jax.ShapeDtypeStruct((nb, ATT_W, t), F32), jax.ShapeDtypeStruct((nb, ATT_W, t), F32),
        jax.ShapeDtypeStruct((nb, IDX_DIM, t), F32),
    )
    colT = lambda r: pl.BlockSpec((r, tm), lambda i: (0, i))
    per_seq = lambda r: pl.BlockSpec((None, r, tm), lambda i: (i // per_b, 0, i % per_b))
    out_specs = (row(d), row(d), row(d), row(d), row(ATT_W), row(LANES), colT(ATT_W), colT(ATT_W),
                 pl.BlockSpec((tm // TK, ATT_W, TK), lambda i: (i, 0, 0)), colT(IDX_HEADS),
                 per_seq(ATT_W), per_seq(ATT_W), per_seq(IDX_DIM))
    return pl.pallas_call(
        _inproj_p_kernel, grid=(n // tm,),
        in_specs=[row(d), _const_spec(g.shape), _const_spec(wm.shape), _const_spec(wt.shape)],
        out_specs=out_specs, out_shape=out_shape, compiler_params=_params(1),
        name="inproj_prompt")(x, g, wm, wt)


def _inproj_s_kernel(x_ref, g_ref, wm_ref,
                     xr_ref, gr_ref, ga_ref, gb_ref, k_ref, v_ref, ikw_ref, q_ref, iq_ref):
    d = x_ref.shape[1]
    hb = _rms(x_ref[...], g_ref[...]).astype(BF16)
    z = _dot(hb, wm_ref[...])
    xr_ref[...] = z[:, 0:d]
    gr_ref[...] = z[:, d:2 * d]
    ga_ref[...] = z[:, 2 * d:3 * d]
    gb_ref[...] = z[:, 3 * d:4 * d]
    o = 4 * d
    k_ref[...] = z[:, o:o + ATT_W]
    v_ref[...] = z[:, o + ATT_W:o + 2 * ATT_W]
    ikw_ref[...] = z[:, o + 2 * ATT_W:o + 2 * ATT_W + LANES]
    o2 = o + 2 * ATT_W + LANES
    q_ref[...] = (z[:, o2:o2 + ATT_W] * (HEAD_DIM ** -0.5)).astype(BF16)
    iq_ref[...] = z[:, o2 + ATT_W:o2 + 2 * ATT_W].astype(BF16)


def _inproj_s(x, g, wm, tm):
    n, d = x.shape
    row = lambda w: pl.BlockSpec((tm, w), lambda i: (i, 0))
    out_shape = (
        jax.ShapeDtypeStruct((n, d), F32), jax.ShapeDtypeStruct((n, d), F32),
        jax.ShapeDtypeStruct((n, d), F32), jax.ShapeDtypeStruct((n, d), F32),
        jax.ShapeDtypeStruct((n, ATT_W), F32), jax.ShapeDtypeStruct((n, ATT_W), F32),
        jax.ShapeDtypeStruct((n, LANES), F32),
        jax.ShapeDtypeStruct((n, ATT_W), BF16), jax.ShapeDtypeStruct((n, ATT_W), BF16),
    )
    out_specs = (row(d), row(d), row(d), row(d), row(ATT_W), row(ATT_W), row(LANES),
                 row(ATT_W), row(ATT_W))
    return pl.pallas_call(
        _inproj_s_kernel, grid=(n // tm,),
        in_specs=[row(d), _const_spec(g.shape), _const_spec(wm.shape)],
        out_specs=out_specs, out_shape=out_shape, compiler_params=_params(1),
        name="inproj_sample")(x, g, wm)


def _lru_gates(xc, wa_ref, wx_ref, ba, bx, lam):
    xcb = xc.astype(BF16)
    gw = wa_ref.shape[1]
    r_parts, i_parts = [], []
    for g in range(wa_ref.shape[0]):
        xs = xcb[:, g * gw:(g + 1) * gw]
        r_parts.append(_dot(xs, wa_ref[g]))
        i_parts.append(_dot(xs, wx_ref[g]))
    r = _sigmoid(jnp.concatenate(r_parts, axis=1) + ba)
    i = _sigmoid(jnp.concatenate(i_parts, axis=1) + bx)
    log_sig_lam = jnp.minimum(lam, 0.0) - jnp.log(1.0 + jnp.exp(-jnp.abs(lam)))
    log_a = LRU_C * r * log_sig_lam
    a = jnp.exp(log_a)
    u = jnp.sqrt(1.0 - a * a) * (i * xc)
    return a, u


def _rglru_p_kernel(xr_ref, gr_ref, cw_ref, cb_ref, wa_ref, wx_ref, ba_ref, bx_ref, lam_ref,
                    ya_ref, hlast_ref, buf_ref,
                    xx_ref, a_ref, u_ref, h_ref):
    step = pl.program_id(0)
    nb, tt, d = xr_ref.shape
    cw = cw_ref.shape[0]

    @pl.when(step == 0)
    def _():
        xx_ref[:, 0:8, :] = jnp.zeros((nb, 8, d), F32)
        h_ref[...] = jnp.zeros_like(h_ref)

    for b in range(nb):
        x = xr_ref[b]
        xx_ref[b, 8:8 + tt, :] = x
        xc = cb_ref[...] + x * cw_ref[cw - 1:cw, :]
        for j in range(cw - 1):
            sh = cw - 1 - j
            xc = xc + xx_ref[b, 8 - sh:8 - sh + tt, :] * cw_ref[j:j + 1, :]
        xx_ref[b, 0:8, :] = x[tt - 8:tt, :]
        a, u = _lru_gates(xc, wa_ref, wx_ref, ba_ref[...], bx_ref[...], lam_ref[...])
        a_ref[b] = a
        u_ref[b] = u

    def scan_body(t, hs):
        new = []
        for b in range(nb):
            h = a_ref[b, pl.ds(t, 1), :] * hs[b] + u_ref[b, pl.ds(t, 1), :]
            u_ref[b, pl.ds(t, 1), :] = h
            new.append(h)
        return tuple(new)

    hs = lax.fori_loop(0, tt, scan_body, tuple(h_ref[b:b + 1, :] for b in range(nb)), unroll=8)
    for b in range(nb):
        h_ref[b:b + 1, :] = hs[b]
        ya_ref[b] = (u_ref[b] * _gelu_tanh(gr_ref[b])).astype(BF16)
        buf_ref[b] = xr_ref[b, tt - (cw - 1):tt, :]
    hlast_ref[...] = h_ref[...]


def _rglru_p(xr, gr, cw, cb, wa, wx, ba, bx, lam, tt):
    nb, t, d = xr.shape
    blk = pl.BlockSpec((nb, tt, d), lambda i: (0, i, 0))
    consts = [cw, cb, wa, wx, ba, bx, lam]
    return pl.pallas_call(
        _rglru_p_kernel, grid=(t // tt,),
        in_specs=[blk, blk] + [_const_spec(c.shape) for c in consts],
        out_specs=(blk, _whole_out_spec((nb, d)), _whole_out_spec((nb, cw.shape[0] - 1, d))),
        out_shape=(jax.ShapeDtypeStruct((nb, t, d), BF16), jax.ShapeDtypeStruct((nb, d), F32),
                   jax.ShapeDtypeStruct((nb, cw.shape[0] - 1, d), F32)),
        scratch_shapes=[pltpu.VMEM((nb, tt + 8, d), F32), pltpu.VMEM((nb, tt, d), F32),
                        pltpu.VMEM((nb, tt, d), F32), pltpu.VMEM((nb, d), F32)],
        compiler_params=_params(1), name="rglru_prompt")(xr, gr, *consts)


def _rglru_s_kernel(xr_ref, gr_ref, st_ref, h0_ref, cw_ref, cb_ref, wa_ref, wx_ref, ba_ref,
                    bx_ref, lam_ref, ya_ref, hlast_ref, buf_ref):
    t_len, nb, d = xr_ref.shape
    cw = cw_ref.shape[0]
    rows = [st_ref[j] for j in range(cw - 1)] + [xr_ref[t] for t in range(t_len)]
    h = h0_ref[...]
    for t in range(t_len):
        xc = cb_ref[...] + rows[t + cw - 1] * cw_ref[cw - 1:cw, :]
        for j in range(cw - 1):
            xc = xc + rows[t + j] * cw_ref[j:j + 1, :]
        a, u = _lru_gates(xc, wa_ref, wx_ref, ba_ref[...], bx_ref[...], lam_ref[...])
        h = a * h + u
        ya_ref[t] = (h * _gelu_tanh(gr_ref[t])).astype(BF16)
    hlast_ref[...] = h
    for j in range(cw - 1):
        buf_ref[j] = rows[t_len + j]


def _rglru_s(xr, gr, st, h0, cw, cb, wa, wx, ba, bx, lam):
    t_len, nb, d = xr.shape
    args = [xr, gr, st, h0, cw, cb, wa, wx, ba, bx, lam]
    return pl.pallas_call(
        _rglru_s_kernel, grid=(1,),
        in_specs=[_const_spec(a.shape) for a in args],
        out_specs=(_whole_out_spec((t_len, nb, d)), _whole_out_spec((nb, d)),
                   _whole_out_spec((cw.shape[0] - 1, nb, d))),
        out_shape=(jax.ShapeDtypeStruct((t_len, nb, d), BF16), jax.ShapeDtypeStruct((nb, d), F32),
                   jax.ShapeDtypeStruct((cw.shape[0] - 1, nb, d), F32)),
        compiler_params=_params(1), name="rglru_sample")(*args)


def _sortable_key(score):
    bits = pltpu.bitcast(score, I32)
    return bits ^ ((bits >> 31) & 0x7FFFFFFF)


GROUP = 16


def _sort_network(n):
    pairs = []
    p = 1
    while p < n:
        k = p
        while k >= 1:
            for j in range(k % p, n - k, 2 * k):
                for i in range(min(k, n - j - k)):
                    if (i + j) // (2 * p) == (i + j + k) // (2 * p):
                        pairs.append((i + j, i + j + k))
            k //= 2
        p *= 2
    return pairs


def _count_in_sorted_group(v, cmp):
    one = lambda m, w: jnp.where(m, w, 0)
    m1 = cmp(v(7))
    m2 = cmp(jnp.where(m1, v(11), v(3)))
    m3 = cmp(jnp.where(m1, jnp.where(m2, v(13), v(9)), jnp.where(m2, v(5), v(1))))
    hi = jnp.where(m2, jnp.where(m3, v(14), v(12)), jnp.where(m3, v(10), v(8)))
    lo = jnp.where(m2, jnp.where(m3, v(6), v(4)), jnp.where(m3, v(2), v(0)))
    m4 = cmp(jnp.where(m1, hi, lo))
    m5 = cmp(v(15))
    return one(m1, 8) + one(m2, 4) + one(m3, 2) + one(m4, 1) + one(m5, 1)


def _attn_p_kernel(qT_ref, iqT_ref, iwT_ref, kb_ref, vT_ref, ikw_ref, dtab_ref, o_ref,
                   keys_ref, sorted_ref, rhs_ref, qbd_ref, m_ref, acc_ref, x_ref, *, k_top):
    qb = pl.program_id(1)
    t0 = qb * TQ
    n_chunks = (qb + 2) >> 1
    n_far = jnp.maximum(qb - 1, 0) >> 1
    half = LANES // 2

    zeros_half = jnp.zeros((half, TQ), BF16)
    for c in range(N_HEADS // 2):
        for hh in range(2):
            h = 2 * c + hh
            iq_h = iqT_ref[h * IDX_DIM:(h + 1) * IDX_DIM, :]
            rhs_ref[c, :, hh * TQ:(hh + 1) * TQ] = jnp.concatenate([iq_h, zeros_half], axis=0)
            q_h = qT_ref[h * HEAD_DIM:(h + 1) * HEAD_DIM, :]
            parts = [zeros_half, q_h] if hh else [q_h, zeros_half]
            qbd_ref[c, :, hh * TQ:(hh + 1) * TQ] = jnp.concatenate(parts, axis=0)

    w = (iwT_ref[...] * (IDX_HEADS ** -0.5)) * (IDX_DIM ** -0.5)
    s_iota = lax.broadcasted_iota(I32, (TK, TQ), 0)
    t_glob = t0 + lax.broadcasted_iota(I32, (TK, TQ), 1)

    def score_body(j, carry):
        off = pl.multiple_of(j * TK, TK)
        ikc = ikw_ref[pl.ds(off, TK), :]
        score = jnp.zeros((TK, TQ), F32)
        for c in range(N_HEADS // 2):
            dd = _dot(ikc, rhs_ref[c])
            for hh in range(2):
                h = 2 * c + hh
                score = score + jnp.maximum(dd[:, hh * TQ:(hh + 1) * TQ], 0.0) * w[h:h + 1, :]
        key = _sortable_key(score)
        key = jnp.where(s_iota + off <= t_glob, key, INT_MIN)
        keys_ref[pl.ds(off, TK), :] = key
        per = TK // GROUP
        v = [key[i * per:(i + 1) * per, :] for i in range(GROUP)]
        for a, b in _sort_network(GROUP):
            v[a], v[b] = jnp.maximum(v[a], v[b]), jnp.minimum(v[a], v[b])
        for i in range(GROUP):
            sorted_ref[j, i] = v[i]
        return carry

    _fori_pairs(0, n_chunks, score_body, 0)

    def count(cmp):
        def body(j, acc):
            return acc + _count_in_sorted_group(lambda i: sorted_ref[j, i], cmp)
        acc = _fori_pairs(0, n_chunks, body, jnp.zeros((TK // GROUP, TQ), I32))
        return acc.sum(axis=0, keepdims=True)

    def bit_body(p, thr):
        bit = jnp.left_shift(jnp.int32(1), 31 - p)
        cand = thr ^ bit
        cnt = count(lambda kc: kc >= cand)
        return jnp.where(cnt >= k_top, cand, thr)

    thr = lax.fori_loop(0, 32, bit_body, jnp.full((1, TQ), INT_MIN, I32))
    need = (k_top - count(lambda kc: kc > thr)).astype(F32)

    ltri = (lax.broadcasted_iota(I32, (TK, TK), 1) <= lax.broadcasted_iota(I32, (TK, TK), 0)
            ).astype(BF16)
    fold = lambda a, op: op(a.reshape(TK // 8, 8, TQ), axis=0)

    def logits_body(j, carry, band):
        taken, mx = carry
        off = pl.multiple_of(j * TK, TK)
        keyc = keys_ref[pl.ds(off, TK), :]
        eq = keyc == thr
        prefix = _dot(ltri, eq.astype(BF16))
        sel = (keyc > thr) | (eq & (prefix + taken <= need))
        if band:
            sel = sel & (s_iota + off <= t_glob)
            tab = (t0 - off) // TQ
        kc = kb_ref[pl.ds(off, TK), :]
        new_mx = []
        for c in range(N_HEADS // 2):
            lg = _dot(kc[:, c * LANES:(c + 1) * LANES], qbd_ref[c])
            for hh in range(2):
                h = 2 * c + hh
                x = lg[:, hh * TQ:(hh + 1) * TQ]
                if band:
                    x = x + dtab_ref[tab, h]
                x = jnp.where(sel, x * LOG2E, NEG)
                x_ref[h, pl.ds(off, TK), :] = x
                new_mx.append(jnp.maximum(mx[h * 8:(h + 1) * 8, :], fold(x, jnp.max)))
        return taken + prefix[TK - 1:TK, :], jnp.concatenate(new_mx, axis=0)

    carry = (jnp.zeros((1, TQ), F32), jnp.full((N_HEADS * 8, TQ), NEG, F32))
    carry = _fori_pairs(0, n_far, functools.partial(logits_body, band=False), carry)
    _, mx = _fori_pairs(n_far, n_chunks, functools.partial(logits_body, band=True), carry)
    m_ref[...] = jnp.concatenate(
        [mx[h * 8:(h + 1) * 8, :].max(axis=0, keepdims=True) for h in range(N_HEADS)], axis=0)

    acc_ref[...] = jnp.zeros(acc_ref.shape, F32)

    def pv_body(j, lsum):
        off = pl.multiple_of(j * TK, TK)
        new_lsum = []
        for h in range(N_HEADS):
            p = jnp.exp2(x_ref[h, pl.ds(off, TK), :] - m_ref[h:h + 1, :])
            new_lsum.append(lsum[h * 8:(h + 1) * 8, :] + fold(p, jnp.sum))
            rows = slice(h * HEAD_DIM, (h + 1) * HEAD_DIM)
            acc_ref[rows, :] += _dot(vT_ref[j, rows, :], p.astype(BF16))
        return jnp.concatenate(new_lsum, axis=0)

    lsum = _fori_pairs(0, n_chunks, pv_body, jnp.zeros((N_HEADS * 8, TQ), F32))
    for h in range(N_HEADS):
        rows = slice(h * HEAD_DIM, (h + 1) * HEAD_DIM)
        inv = 1.0 / lsum[h * 8:(h + 1) * 8, :].sum(axis=0, keepdims=True)
        acc_ref[rows, :] = acc_ref[rows, :] * inv
    o_ref[...] = acc_ref[...].T.astype(BF16)


def _attn_p(qT, iqT, iwT, kb, vT, ikwb, dtab, nb, t):
    n = nb * t
    nq = t // TQ
    k_top = min(TOPK_MAX, t // 4)
    colT = lambda r: pl.BlockSpec((r, TQ), lambda b, q: (0, b * nq + q))
    return pl.pallas_call(
        functools.partial(_attn_p_kernel, k_top=k_top), grid=(nb, nq),
        in_specs=[colT(ATT_W), colT(ATT_W), colT(IDX_HEADS),
                  pl.BlockSpec((t, ATT_W), lambda b, q: (b, 0)),
                  pl.BlockSpec((t // TK, ATT_W, TK), lambda b, q: (b, 0, 0)),
                  pl.BlockSpec((t, LANES), lambda b, q: (b, 0)),
                  _const_spec(dtab.shape)],
        out_specs=pl.BlockSpec((TQ, ATT_W), lambda b, q: (b * nq + q, 0)),
        out_shape=jax.ShapeDtypeStruct((n, ATT_W), BF16),
        scratch_shapes=[pltpu.VMEM((t, TQ), I32),
                        pltpu.VMEM((t // TK, GROUP, TK // GROUP, TQ), I32),
                        pltpu.VMEM((N_HEADS // 2, LANES, 2 * TQ), BF16),
                        pltpu.VMEM((N_HEADS // 2, LANES, 2 * TQ), BF16),
                        pltpu.VMEM((N_HEADS, TQ), F32), pltpu.VMEM((ATT_W, TQ), F32),
                        pltpu.VMEM((N_HEADS, t, TQ), F32)],
        compiler_params=_params(2), name="attn_prompt")(qT, iqT, iwT, kb, vT, ikwb, dtab)


def _select_s_kernel(pt_ref, iq_ref, wcol_ref, iknew_ref, pool_ref, mask_ref,
                     ikbuf_ref, score_ref, sem, *, k_top, layer_off, t_len):
    g = pl.program_id(0)
    gs, _, s_pad = ikbuf_ref.shape
    n_pages = pt_ref.shape[1]
    page = pool_ref.shape[2]
    past = n_pages * page

    def page_copy(i, j):
        return pltpu.make_async_copy(pool_ref.at[layer_off + pt_ref[g * gs + i, j]],
                                     ikbuf_ref.at[i, :, j * page:(j + 1) * page], sem)

    def start_body(i, c):
        for j in range(n_pages):
            page_copy(i, j).start()
        return c

    lax.fori_loop(0, gs, start_body, 0)
    ikbuf_ref[:, :, past:s_pad] = jnp.zeros((gs, IDX_DIM, s_pad - past), F32)
    ikbuf_ref[:, :, past:past + t_len] = iknew_ref[...]

    def wait_body(i, c):
        for j in range(n_pages):
            page_copy(i, j).wait()
        return c

    lax.fori_loop(0, gs, wait_body, 0)

    tile = score_ref.shape[1]
    per_tile = tile // t_len

    def score_body(it, c):
        for u in range(per_tile):
            i = it * per_tile + u
            dd = _dot(iq_ref[i], ikbuf_ref[i].astype(BF16))
            wv = (wcol_ref[i] * (IDX_HEADS ** -0.5)) * (IDX_DIM ** -0.5)
            sc = (jnp.maximum(dd, 0.0) * wv).reshape(t_len, IDX_HEADS, s_pad).sum(axis=1)
            score_ref[it, u * t_len:(u + 1) * t_len, :] = sc
        return c

    lax.fori_loop(0, gs // per_tile, score_body, 0)

    rows = gs * t_len
    s_idx = lax.broadcasted_iota(I32, (rows, s_pad), 1)
    t_idx = lax.broadcasted_iota(I32, (rows, s_pad), 0) % t_len
    valid = s_idx <= past + t_idx
    keys = jnp.where(valid, _sortable_key(score_ref[...].reshape(rows, s_pad)), INT_MIN)

    def bit_body(p, thr):
        bit = jnp.left_shift(jnp.int32(1), 31 - p)
        cand = thr ^ bit
        cnt = jnp.sum((keys >= cand).astype(I32), axis=1, keepdims=True)
        return jnp.where(cnt >= k_top, cand, thr)

    thr = lax.fori_loop(0, 32, bit_body, jnp.full((rows, 1), INT_MIN, I32))
    gt = keys > thr
    eq = keys == thr
    need = (k_top - jnp.sum(gt.astype(I32), axis=1, keepdims=True)).astype(F32)
    utri = (lax.broadcasted_iota(I32, (LANES, LANES), 0) <= lax.broadcasted_iota(I32, (LANES, LANES), 1)
            ).astype(BF16)
    eqb = eq.astype(BF16)
    carry = jnp.zeros((rows, 1), F32)
    for c in range(s_pad // LANES):
        cols = slice(c * LANES, (c + 1) * LANES)
        prefix = _dot(eqb[:, cols], utri) + carry
        take = gt[:, cols] | (eq[:, cols] & (prefix <= need))
        take = take & valid[:, cols]
        m = jnp.where(take, 0.0, NEG).reshape(gs // per_tile, tile, LANES)
        for u in range(per_tile):
            mask_ref[:, u, :, cols] = m[:, u * t_len:(u + 1) * t_len, :]
        carry = prefix[:, LANES - 1:LANES]


def _select_s(page_table, iq, wcol, iknew, pool, layer, gs, k_top):
    nb, rows_q, _ = iq.shape
    t_len = rows_q // IDX_HEADS
    n_pages = page_table.shape[1]
    page = pool.shape[2]
    s_pad = n_pages * page + LANES
    sublanes = 8
    assert sublanes % t_len == 0
    per_tile = sublanes // t_len
    assert nb % gs == 0 and gs % per_tile == 0
    kern = functools.partial(_select_s_kernel, k_top=k_top, layer_off=layer, t_len=t_len)
    grid_spec = pltpu.PrefetchScalarGridSpec(
        num_scalar_prefetch=1, grid=(nb // gs,),
        in_specs=[pl.BlockSpec((gs, rows_q, IDX_DIM), lambda g, pt: (g, 0, 0)),
                  pl.BlockSpec((gs, rows_q, 1), lambda g, pt: (g, 0, 0)),
                  pl.BlockSpec((gs, IDX_DIM, t_len), lambda g, pt: (g, 0, 0)),
                  pl.BlockSpec(memory_space=pl.ANY)],
        out_specs=pl.BlockSpec((gs // per_tile, per_tile, t_len, s_pad), lambda g, pt: (g, 0, 0, 0)),
        scratch_shapes=[pltpu.VMEM((gs, IDX_DIM, s_pad), F32),
                        pltpu.VMEM((gs // per_tile, sublanes, s_pad), F32),
                        pltpu.SemaphoreType.DMA(())])
    return pl.pallas_call(
        kern, grid_spec=grid_spec,
        out_shape=jax.ShapeDtypeStruct((nb // per_tile, per_tile, t_len, s_pad), F32),
        compiler_params=_params(1), name="select_sample")(page_table, iq, wcol, iknew, pool)


def _attn_s_kernel(pt_ref, q_ref, knew_ref, vnew_ref, mask_ref, dtab_ref, *rest, n_pages, t_len):
    k_pages = rest[:n_pages]
    v_pages = rest[n_pages:2 * n_pages]
    o_ref, kpad_ref, vpad_ref = rest[2 * n_pages:]
    rows = t_len * N_HEADS
    page = k_pages[0].shape[2]

    def flat_bf16(page_ref):
        return page_ref[...].reshape(ATT_W, page).astype(BF16)

    def per_head_rows(x):
        return jnp.concatenate(
            [jnp.broadcast_to(x[t:t + 1, :], (N_HEADS, x.shape[1])) for t in range(t_len)], axis=0)

    col_head = lax.broadcasted_iota(I32, (rows, ATT_W), 1) // HEAD_DIM
    row_head = lax.broadcasted_iota(I32, (rows, ATT_W), 0) % N_HEADS
    own = col_head == row_head
    qbd = jnp.where(own, per_head_rows(q_ref[...].astype(F32)), 0.0).astype(BF16)
    kpad_ref[...] = jnp.zeros(kpad_ref.shape, F32)
    vpad_ref[...] = jnp.zeros(vpad_ref.shape, F32)
    kpad_ref[0:t_len, :] = knew_ref[...]
    vpad_ref[0:t_len, :] = vnew_ref[...]
    knew = kpad_ref[...].astype(BF16)
    vnew = vpad_ref[...].astype(BF16)
    logits = [_dot(qbd, flat_bf16(kp)) for kp in k_pages] + [_dot_nt(qbd, knew)]
    x = jnp.concatenate(logits, axis=1)
    x = x + dtab_ref[...] + per_head_rows(mask_ref[...])
    m = x.max(axis=1, keepdims=True)
    p = jnp.exp(x - m)
    l = p.sum(axis=1, keepdims=True)
    pb = p.astype(BF16)
    out = _dot(pb[:, n_pages * page:], vnew)
    for j, vp in enumerate(v_pages):
        out = out + _dot_nt(pb[:, j * page:(j + 1) * page], flat_bf16(vp))
    out = out / l
    out = jnp.where(own, out, 0.0).reshape(t_len, N_HEADS, ATT_W).sum(axis=1)
    o_ref[...] = out.astype(BF16)


def _attn_s(page_table, q, knew, vnew, mask, dtab, k_pool, v_pool, layer_off):
    nb, t_len, _ = q.shape
    n_pages = page_table.shape[1]
    page = k_pool.shape[3]
    s_pad = n_pages * page + LANES

    def page_spec(j):
        return pl.BlockSpec((None, N_HEADS, HEAD_DIM, page),
                            lambda b, pt, j=j: (layer_off + pt[b, j], 0, 0, 0))

    seq = lambda w: pl.BlockSpec((None, t_len, w), lambda b, pt: (b, 0, 0))
    per_tile = mask.shape[1]
    mask_spec = pl.BlockSpec((None, None, t_len, s_pad),
                             lambda b, pt: (b // per_tile, b % per_tile, 0, 0))
    grid_spec = pltpu.PrefetchScalarGridSpec(
        num_scalar_prefetch=1, grid=(nb,),
        in_specs=[seq(ATT_W), seq(ATT_W), seq(ATT_W), mask_spec,
                  pl.BlockSpec(dtab.shape, lambda b, pt: (0, 0))]
                 + [page_spec(j) for j in range(n_pages)] * 2,
        out_specs=seq(ATT_W),
        scratch_shapes=[pltpu.VMEM((page, ATT_W), F32), pltpu.VMEM((page, ATT_W), F32)])
    kern = functools.partial(_attn_s_kernel, n_pages=n_pages, t_len=t_len)
    return pl.pallas_call(
        kern, grid_spec=grid_spec, out_shape=jax.ShapeDtypeStruct((nb, t_len, ATT_W), BF16),
        compiler_params=_params(1), name="attn_sample")(
            page_table, q, knew, vnew, mask, dtab, *([k_pool] * n_pages), *([v_pool] * n_pages))


def _post_kernel(x_ref, ya_ref, o_ref, ga_ref, gb_ref, p_ref, wr_ref, wa_ref, wo_ref,
                 nf_ref, wg_ref, wu_ref, wd_ref, np_ref, wpg_ref, wpp_ref, nfin_ref,
                 out_ref, *, final):
    a = _dot(ya_ref[...], wr_ref[...])
    b = _dot(o_ref[...], wa_ref[...])
    mix = _sigmoid(ga_ref[...]) * a + _sigmoid(gb_ref[...]) * b
    x = x_ref[...] + _dot(mix.astype(BF16), wo_ref[...])
    h2 = _rms(x, nf_ref[...]).astype(BF16)
    g = _dot(h2, wg_ref[...])
    u = _dot(h2, wu_ref[...])
    act = (g * _sigmoid(g)) * u
    x = x + _dot(act.astype(BF16), wd_ref[...])
    h3 = _rms(x, np_ref[...]).astype(BF16)
    x = x + _sigmoid(_dot(h3, wpg_ref[...])) * _dot(p_ref[...].astype(BF16), wpp_ref[...])
    if final:
        x = _rms(x, nfin_ref[...])
    out_ref[...] = x


def _layer_spec(w, l):
    tail = (0,) * (w.ndim - 1)
    return pl.BlockSpec((None,) + w.shape[1:], lambda *_: (l,) + tail,
                        pipeline_mode=pl.Buffered(1))


def _post(x, ya, o, ga, gb, p, weights, tm, final, l, p_row0=0):
    n, d = x.shape
    row = lambda w: pl.BlockSpec((tm, w), lambda i: (i, 0))
    p_blk0 = p_row0 // tm
    return pl.pallas_call(
        functools.partial(_post_kernel, final=final), grid=(n // tm,),
        in_specs=[row(d), row(d), row(ATT_W), row(d), row(d),
                  pl.BlockSpec((tm, p.shape[1]), lambda i: (i + p_blk0, 0))]
                 + [_layer_spec(c, l) if c.ndim == 3 else _const_spec(c.shape) for c in weights],
        out_specs=row(d), out_shape=jax.ShapeDtypeStruct((n, d), F32),
        compiler_params=_params(1), name="post")(x, ya, o, ga, gb, p, *weights)


def _rel_bucket(n, n_buckets):
    max_exact = n_buckets // 2
    nf = jnp.maximum(n, 1).astype(F32)
    large = max_exact + (jnp.log(nf / max_exact) / math.log(MAX_DISTANCE / max_exact)
                         * (n_buckets - max_exact)).astype(I32)
    large = jnp.minimum(large, n_buckets - 1)
    return jnp.where(n < max_exact, n, large)


def _bias_delta(rel_bias, n):
    nbk = rel_bias.shape[0]
    b = rel_bias.astype(F32)
    onehot = _rel_bucket(jnp.maximum(n, 0), nbk)[..., None] == jnp.arange(nbk, dtype=I32)
    picked = jnp.where(onehot[..., None], b, 0.0).sum(axis=-2)
    return picked - b[nbk - 1]


def _block_diag_groups(w, group):
    nbk, c, _ = w.shape
    per = group // c
    wg = w.reshape(nbk // per, per, c, c)
    eye = jnp.eye(per, dtype=w.dtype)
    return jnp.einsum('gpcd,pq->gpcqd', wg, eye).reshape(nbk // per, group, group)


def kernel(x_prompt, x_sample, p_prompt, p_sample, cache_k, cache_v, cache_idx_k, state_rglru_h,
           state_conv, page_table, rel_bias, norm_mix, w_in, conv_w, conv_b, w_rg_a, b_rg_a,
           w_rg_x, b_rg_x, lru_lambda, w_rnn_out, w_att_out, w_o, norm_ffn, w_ffn_gate, w_ffn_up,
           w_ffn_down, norm_ple, w_ple_gate, w_ple_proj, norm_final):
    nb, t, d = x_prompt.shape
    sb, st, _ = x_sample.shape
    depth = w_in.shape[0]
    n_pool, page = cache_k.shape[1], cache_k.shape[2]
    n_pages = page_table.shape[1]
    past = n_pages * page
    cw = conv_w.shape[1]
    s_pad = past + LANES
    k_top_s = min(TOPK_MAX, (past + st) // 4)
    row2 = lambda v: v.reshape(1, -1).astype(F32)

    idx3 = jnp.arange(3, dtype=I32)[:, None, None]
    s_rel = jnp.arange(TK, dtype=I32)[None, :, None]
    t_rel = jnp.arange(TQ, dtype=I32)[None, None, :]
    dtab_p = _bias_delta(rel_bias, idx3 * TQ + t_rel - s_rel).transpose(0, 3, 1, 2)
    s_all = jnp.arange(s_pad, dtype=I32)[None, :]
    q_pos = past + jnp.arange(st, dtype=I32)[:, None]
    dtab_s = _bias_delta(rel_bias, q_pos - s_all)
    dtab_s = dtab_s.transpose(0, 2, 1).reshape(st * N_HEADS, s_pad)

    k_pool = cache_k.transpose(0, 1, 3, 4, 2).reshape(depth * n_pool, N_HEADS, HEAD_DIM, page)
    v_pool = cache_v.transpose(0, 1, 3, 4, 2).reshape(depth * n_pool, N_HEADS, HEAD_DIM, page)
    ik_pool = cache_idx_k.transpose(0, 1, 3, 2).reshape(depth * n_pool, IDX_DIM, page)

    w_r, w_a, w_oo = w_rnn_out.astype(BF16), w_att_out.astype(BF16), w_o.astype(BF16)
    w_fg, w_fu, w_fd = w_ffn_gate.astype(BF16), w_ffn_up.astype(BF16), w_ffn_down.astype(BF16)
    w_pg, w_pp = w_ple_gate.astype(BF16), w_ple_proj.astype(BF16)
    p_rows = p_prompt.reshape(depth * nb * t, -1)

    xp = x_prompt.reshape(nb * t, d)
    xs = x_sample.transpose(1, 0, 2).reshape(st * sb, d)
    outs_p, outs_s = [], []
    for l in range(depth):
        splits = [d, 2 * d, 2 * d + ATT_W, 2 * d + 2 * ATT_W, 2 * d + 3 * ATT_W,
                  2 * d + 3 * ATT_W + IDX_HEADS * IDX_DIM,
                  2 * d + 3 * ATT_W + IDX_HEADS * IDX_DIM + IDX_DIM,
                  2 * d + 3 * ATT_W + IDX_HEADS * IDX_DIM + IDX_DIM + IDX_HEADS,
                  3 * d + 3 * ATT_W + IDX_HEADS * IDX_DIM + IDX_DIM + IDX_HEADS]
        w_xr, w_gr, w_q, w_k, w_v, w_iq, w_ik, w_iw, w_ga, w_gb = jnp.split(w_in[l], splits, axis=1)
        w_ikw = jnp.concatenate(
            [w_ik, w_iw, jnp.zeros((d, LANES - IDX_DIM - IDX_HEADS), F32)], axis=1)
        wm_p = jnp.concatenate([w_xr, w_gr, w_ga, w_gb, w_k, w_ikw], axis=1).astype(BF16)
        wm_s = jnp.concatenate([w_xr, w_gr, w_ga, w_gb, w_k, w_v, w_ikw, w_q, w_iq], axis=1).astype(BF16)
        wt_p = jnp.concatenate([w_q, w_iq, w_k, w_v, w_ik, w_iw], axis=1).T.astype(BF16)
        gw = 2 * LANES
        wa_bd = _block_diag_groups(w_rg_a[l], gw).astype(BF16)
        wx_bd = _block_diag_groups(w_rg_x[l], gw).astype(BF16)
        lru_consts = (conv_w[l], row2(conv_b[l]), wa_bd, wx_bd, row2(b_rg_a[l]), row2(b_rg_x[l]),
                      row2(lru_lambda[l]))
        post_w = (w_r, w_a, w_oo, row2(norm_ffn[l]), w_fg, w_fu, w_fd, row2(norm_ple[l]), w_pg, w_pp,
                  row2(norm_final))
        final = l == depth - 1

        (xr, gr, ga, gb, kb, ikwb, qT, iqT, vTb, iwT, kT, vT, ikT) = _inproj_p(
            xp, row2(norm_mix[l]), wm_p, wt_p, tm=256, nb=nb)
        ya, h_last, new_buf = _rglru_p(xr.reshape(nb, t, d), gr.reshape(nb, t, d), *lru_consts, tt=128)
        o = _attn_p(qT, iqT, iwT, kb, vTb, ikwb, dtab_p, nb, t)
        xp = _post(xp, ya.reshape(nb * t, d), o, ga, gb, p_rows, post_w, tm=256, final=final, l=l,
                   p_row0=l * nb * t)
        heads_last = lambda aT: aT.reshape(nb, N_HEADS, HEAD_DIM, t).transpose(0, 3, 1, 2)
        outs_p.append((heads_last(kT), heads_last(vT), ikT.transpose(0, 2, 1), h_last, new_buf))

        (xr, gr, ga, gb, k, v, ikw, q, iq) = _inproj_s(xs, row2(norm_mix[l]), wm_s, tm=st * sb)
        tmaj = lambda a2: a2.reshape(st, sb, -1)
        ya, h_last, new_buf = _rglru_s(tmaj(xr), tmaj(gr), state_conv[l].transpose(1, 0, 2),
                                       state_rglru_h[l], *lru_consts)
        bmaj = lambda a2: a2.reshape(st, sb, -1).transpose(1, 0, 2)
        k_b, v_b, ikw_b, q_b, iq_b = bmaj(k), bmaj(v), bmaj(ikw), bmaj(q), bmaj(iq)
        iq_rows = iq_b.reshape(sb, st * IDX_HEADS, IDX_DIM)
        w_col = ikw_b[:, :, IDX_DIM:IDX_DIM + IDX_HEADS].reshape(sb, st * IDX_HEADS, 1)
        mask = _select_s(page_table, iq_rows, w_col, ikw_b[:, :, :IDX_DIM].transpose(0, 2, 1),
                         ik_pool, l * n_pool, gs=min(sb, 32), k_top=k_top_s)
        o = _attn_s(page_table, q_b, k_b, v_b, mask, dtab_s, k_pool, v_pool, l * n_pool)
        o = o.transpose(1, 0, 2).reshape(st * sb, ATT_W)
        xs = _post(xs, ya.reshape(st * sb, d), o, ga, gb,
                   p_sample[l].transpose(1, 0, 2).reshape(st * sb, -1), post_w,
                   tm=min(256, st * sb), final=final, l=l)
        outs_s.append((k_b.reshape(sb, st, N_HEADS, HEAD_DIM), v_b.reshape(sb, st, N_HEADS, HEAD_DIM),
                       ikw_b[:, :, :IDX_DIM], h_last, new_buf.transpose(1, 0, 2)))

    y_prompt = xp.reshape(nb, t, d)
    y_sample = xs.reshape(st, sb, d).transpose(1, 0, 2)
    stack = lambda outs, i: jnp.stack([o_[i] for o_ in outs])
    return (y_prompt, y_sample,
            stack(outs_p, 0), stack(outs_p, 1), stack(outs_p, 2), stack(outs_p, 3), stack(outs_p, 4),
            stack(outs_s, 0), stack(outs_s, 1), stack(outs_s, 2), stack(outs_s, 3), stack(outs_s, 4))
```

```python
import functools
import math

import jax
import jax.numpy as jnp
from jax import lax
from jax.experimental import pallas as pl
from jax.experimental.pallas import tpu as pltpu

F32 = jnp.float32
BF16 = jnp.bfloat16
I32 = jnp.int32

N_HEADS = 8
HEAD_DIM = 64
ATT_W = N_HEADS * HEAD_DIM
IDX_HEADS = 8
IDX_DIM = 64
TOPK_MAX = 256
LRU_C = 8.0
MAX_DISTANCE = 128
EPS = 1e-6

LANES = 128
TQ = 128
TK = 256
NEG = -1e30
LOG2E = math.log2(math.e)
INT_MIN = -2147483648
VMEM_LIMIT = 56 * 1024 * 1024


def _dot(a, b):
    return jnp.dot(a, b, preferred_element_type=F32)


def _dot_nt(a, b):
    return lax.dot_general(a, b, (((1,), (1,)), ((), ())), preferred_element_type=F32)


def _sigmoid(x):
    return 0.5 * jnp.tanh(0.5 * x) + 0.5


def _gelu_tanh(x):
    c = math.sqrt(2.0 / math.pi)
    return 0.5 * x * (1.0 + jnp.tanh(c * (x + 0.044715 * (x * x * x))))


def _rms(x, g):
    y = x * lax.rsqrt(jnp.mean(x * x, axis=-1, keepdims=True) + EPS)
    return y * g


def _params(n_axes=1):
    return pltpu.CompilerParams(dimension_semantics=("arbitrary",) * n_axes,
                                vmem_limit_bytes=VMEM_LIMIT)


def _fori_pairs(lo, hi, body, carry):
    n = hi - lo

    def pair(i, c):
        j = lo + 2 * i
        return body(j + 1, body(j, c))

    carry = lax.fori_loop(0, n >> 1, pair, carry)
    return lax.cond((n & 1) == 1, lambda c: body(hi - 1, c), lambda c: c, carry)


def _const_spec(shape):
    nd = len(shape)
    return pl.BlockSpec(shape, lambda *_: (0,) * nd, pipeline_mode=pl.Buffered(1))


def _whole_out_spec(shape):
    nd = len(shape)
    return pl.BlockSpec(shape, lambda *_: (0,) * nd)


def _inproj_p_kernel(x_ref, g_ref, wm_ref, wt_ref,
                     xr_ref, gr_ref, ga_ref, gb_ref, kb_ref, ikwb_ref,
                     qT_ref, iqT_ref, vTb_ref, iwT_ref, kT_ref, vT_ref, ikT_ref):
    d = x_ref.shape[1]
    hb = _rms(x_ref[...], g_ref[...]).astype(BF16)
    z = _dot(hb, wm_ref[...])
    xr_ref[...] = z[:, 0:d]
    gr_ref[...] = z[:, d:2 * d]
    ga_ref[...] = z[:, 2 * d:3 * d]
    gb_ref[...] = z[:, 3 * d:4 * d]
    o = 4 * d
    kb_ref[...] = z[:, o:o + ATT_W].astype(BF16)
    ikwb_ref[...] = z[:, o + ATT_W:o + ATT_W + LANES].astype(BF16)
    zt = _dot_nt(wt_ref[...], hb)
    qT_ref[...] = (zt[0:ATT_W] * (HEAD_DIM ** -0.5)).astype(BF16)
    iqT_ref[...] = zt[ATT_W:2 * ATT_W].astype(BF16)
    kT_ref[...] = zt[2 * ATT_W:3 * ATT_W]
    vt = zt[3 * ATT_W:4 * ATT_W]
    vT_ref[...] = vt
    for c in range(vTb_ref.shape[0]):
        vTb_ref[c] = vt[:, c * TK:(c + 1) * TK].astype(BF16)
    ikT_ref[...] = zt[4 * ATT_W:4 * ATT_W + IDX_DIM]
    iwT_ref[...] = zt[4 * ATT_W + IDX_DIM:4 * ATT_W + IDX_DIM + IDX_HEADS]


def _inproj_p(x, g, wm, wt, tm, nb):
    n, d = x.shape
    t = n // nb
    per_b = t // tm
    row = lambda w: pl.BlockSpec((tm, w), lambda i: (i, 0))
    out_shape = (
        jax.ShapeDtypeStruct((n, d), F32), jax.ShapeDtypeStruct((n, d), F32),
        jax.ShapeDtypeStruct((n, d), F32), jax.ShapeDtypeStruct((n, d), F32),
        jax.ShapeDtypeStruct((n, ATT_W), BF16), jax.ShapeDtypeStruct((n, LANES), BF16),
        jax.ShapeDtypeStruct((ATT_W, n), BF16), jax.ShapeDtypeStruct((ATT_W, n), BF16),
        jax.ShapeDtypeStruct((n // TK, ATT_W, TK), BF16),
        jax.ShapeDtypeStruct((IDX_HEADS, n), F32),
        jax.ShapeDtypeStruct((nb, ATT_W, t), F32), jax.ShapeDtypeStruct((nb, ATT_W, t), F32),
        jax.ShapeDtypeStruct((nb, IDX_DIM, t), F32),
    )
    colT = lambda r: pl.BlockSpec((r, tm), lambda i: (0, i))
    per_seq = lambda r: pl.BlockSpec((None, r, tm), lambda i: (i // per_b, 0, i % per_b))
    out_specs = (row(d), row(d), row(d), row(d), row(ATT_W), row(LANES), colT(ATT_W), colT(ATT_W),
                 pl.BlockSpec((tm // TK, ATT_W, TK), lambda i: (i, 0, 0)), colT(IDX_HEADS),
                 per_seq(ATT_W), per_seq(ATT_W), per_seq(IDX_DIM))
    return pl.pallas_call(
        _inproj_p_kernel, grid=(n // tm,),
        in_specs=[row(d), _const_spec(g.shape), _const_spec(wm.shape), _const_spec(wt.shape)],
        out_specs=out_specs, out_shape=out_shape, compiler_params=_params(1),
        name="inproj_prompt")(x, g, wm, wt)


def _inproj_s_kernel(x_ref, g_ref, wm_ref,
                     xr_ref, gr_ref, ga_ref, gb_ref, k_ref, v_ref, ikw_ref, q_ref, iq_ref):
    d = x_ref.shape[1]
    hb = _rms(x_ref[...], g_ref[...]).astype(BF16)
    z = _dot(hb, wm_ref[...])
    xr_ref[...] = z[:, 0:d]
    gr_ref[...] = z[:, d:2 * d]
    ga_ref[...] = z[:, 2 * d:3 * d]
    gb_ref[...] = z[:, 3 * d:4 * d]
    o = 4 * d
    k_ref[...] = z[:, o:o + ATT_W]
    v_ref[...] = z[:, o + ATT_W:o + 2 * ATT_W]
    ikw_ref[...] = z[:, o + 2 * ATT_W:o + 2 * ATT_W + LANES]
    o2 = o + 2 * ATT_W + LANES
    q_ref[...] = (z[:, o2:o2 + ATT_W] * (HEAD_DIM ** -0.5)).astype(BF16)
    iq_ref[...] = z[:, o2 + ATT_W:o2 + 2 * ATT_W].astype(BF16)


def _inproj_s(x, g, wm, tm):
    n, d = x.shape
    row = lambda w: pl.BlockSpec((tm, w), lambda i: (i, 0))
    out_shape = (
        jax.ShapeDtypeStruct((n, d), F32), jax.ShapeDtypeStruct((n, d), F32),
        jax.ShapeDtypeStruct((n, d), F32), jax.ShapeDtypeStruct((n, d), F32),
        jax.ShapeDtypeStruct((n, ATT_W), F32), jax.ShapeDtypeStruct((n, ATT_W), F32),
        jax.ShapeDtypeStruct((n, LANES), F32),
        jax.ShapeDtypeStruct((n, ATT_W), BF16), jax.ShapeDtypeStruct((n, ATT_W), BF16),
    )
    out_specs = (row(d), row(d), row(d), row(d), row(ATT_W), row(ATT_W), row(LANES),
                 row(ATT_W), row(ATT_W))
    return pl.pallas_call(
        _inproj_s_kernel, grid=(n // tm,),
        in_specs=[row(d), _const_spec(g.shape), _const_spec(wm.shape)],
        out_specs=out_specs, out_shape=out_shape, compiler_params=_params(1),
        name="inproj_sample")(x, g, wm)


def _lru_gates(xc, wa_ref, wx_ref, ba, bx, lam):
    xcb = xc.astype(BF16)
    gw = wa_ref.shape[1]
    r_parts, i_parts = [], []
    for g in range(wa_ref.shape[0]):
        xs = xcb[:, g * gw:(g + 1) * gw]
        r_parts.append(_dot(xs, wa_ref[g]))
        i_parts.append(_dot(xs, wx_ref[g]))
    r = _sigmoid(jnp.concatenate(r_parts, axis=1) + ba)
    i = _sigmoid(jnp.concatenate(i_parts, axis=1) + bx)
    log_sig_lam = jnp.minimum(lam, 0.0) - jnp.log(1.0 + jnp.exp(-jnp.abs(lam)))
    log_a = LRU_C * r * log_sig_lam
    a = jnp.exp(log_a)
    u = jnp.sqrt(1.0 - a * a) * (i * xc)
    return a, u


def _rglru_p_kernel(xr_ref, gr_ref, cw_ref, cb_ref, wa_ref, wx_ref, ba_ref, bx_ref, lam_ref,
                    ya_ref, hlast_ref, buf_ref,
                    xx_ref, a_ref, u_ref, h_ref):
    step = pl.program_id(0)
    nb, tt, d = xr_ref.shape
    cw = cw_ref.shape[0]

    @pl.when(step == 0)
    def _():
        xx_ref[:, 0:8, :] = jnp.zeros((nb, 8, d), F32)
        h_ref[...] = jnp.zeros_like(h_ref)

    for b in range(nb):
        x = xr_ref[b]
        xx_ref[b, 8:8 + tt, :] = x
        xc = cb_ref[...] + x * cw_ref[cw - 1:cw, :]
        for j in range(cw - 1):
            sh = cw - 1 - j
            xc = xc + xx_ref[b, 8 - sh:8 - sh + tt, :] * cw_ref[j:j + 1, :]
        xx_ref[b, 0:8, :] = x[tt - 8:tt, :]
        a, u = _lru_gates(xc, wa_ref, wx_ref, ba_ref[...], bx_ref[...], lam_ref[...])
        a_ref[b] = a
        u_ref[b] = u

    def scan_body(t, hs):
        new = []
        for b in range(nb):
            h = a_ref[b, pl.ds(t, 1), :] * hs[b] + u_ref[b, pl.ds(t, 1), :]
            u_ref[b, pl.ds(t, 1), :] = h
            new.append(h)
        return tuple(new)

    hs = lax.fori_loop(0, tt, scan_body, tuple(h_ref[b:b + 1, :] for b in range(nb)), unroll=8)
    for b in range(nb):
        h_ref[b:b + 1, :] = hs[b]
        ya_ref[b] = (u_ref[b] * _gelu_tanh(gr_ref[b])).astype(BF16)
        buf_ref[b] = xr_ref[b, tt - (cw - 1):tt, :]
    hlast_ref[...] = h_ref[...]


def _rglru_p(xr, gr, cw, cb, wa, wx, ba, bx, lam, tt):
    nb, t, d = xr.shape
    blk = pl.BlockSpec((nb, tt, d), lambda i: (0, i, 0))
    consts = [cw, cb, wa, wx, ba, bx, lam]
    return pl.pallas_call(
        _rglru_p_kernel, grid=(t // tt,),
        in_specs=[blk, blk] + [_const_spec(c.shape) for c in consts],
        out_specs=(blk, _whole_out_spec((nb, d)), _whole_out_spec((nb, cw.shape[0] - 1, d))),
        out_shape=(jax.ShapeDtypeStruct((nb, t, d), BF16), jax.ShapeDtypeStruct((nb, d), F32),
                   jax.ShapeDtypeStruct((nb, cw.shape[0] - 1, d), F32)),
        scratch_shapes=[pltpu.VMEM((nb, tt + 8, d), F32), pltpu.VMEM((nb, tt, d), F32),
                        pltpu.VMEM((nb, tt, d), F32), pltpu.VMEM((nb, d), F32)],
        compiler_params=_params(1), name="rglru_prompt")(xr, gr, *consts)


def _rglru_s_kernel(xr_ref, gr_ref, st_ref, h0_ref, cw_ref, cb_ref, wa_ref, wx_ref, ba_ref,
                    bx_ref, lam_ref, ya_ref, hlast_ref, buf_ref):
    t_len, nb, d = xr_ref.shape
    cw = cw_ref.shape[0]
    rows = [st_ref[j] for j in range(cw - 1)] + [xr_ref[t] for t in range(t_len)]
    h = h0_ref[...]
    for t in range(t_len):
        xc = cb_ref[...] + rows[t + cw - 1] * cw_ref[cw - 1:cw, :]
        for j in range(cw - 1):
            xc = xc + rows[t + j] * cw_ref[j:j + 1, :]
        a, u = _lru_gates(xc, wa_ref, wx_ref, ba_ref[...], bx_ref[...], lam_ref[...])
        h = a * h + u
        ya_ref[t] = (h * _gelu_tanh(gr_ref[t])).astype(BF16)
    hlast_ref[...] = h
    for j in range(cw - 1):
        buf_ref[j] = rows[t_len + j]


def _rglru_s(xr, gr, st, h0, cw, cb, wa, wx, ba, bx, lam):
    t_len, nb, d = xr.shape
    args = [xr, gr, st, h0, cw, cb, wa, wx, ba, bx, lam]
    return pl.pallas_call(
        _rglru_s_kernel, grid=(1,),
        in_specs=[_const_spec(a.shape) for a in args],
        out_specs=(_whole_out_spec((t_len, nb, d)), _whole_out_spec((nb, d)),
                   _whole_out_spec((cw.shape[0] - 1, nb, d))),
        out_shape=(jax.ShapeDtypeStruct((t_len, nb, d), BF16), jax.ShapeDtypeStruct((nb, d), F32),
                   jax.ShapeDtypeStruct((cw.shape[0] - 1, nb, d), F32)),
        compiler_params=_params(1), name="rglru_sample")(*args)


def _sortable_key(score):
    bits = pltpu.bitcast(score, I32)
    return bits ^ ((bits >> 31) & 0x7FFFFFFF)


GROUP = 16


def _sort_network(n):
    pairs = []
    p = 1
    while p < n:
        k = p
        while k >= 1:
            for j in range(k % p, n - k, 2 * k):
                for i in range(min(k, n - j - k)):
                    if (i + j) // (2 * p) == (i + j + k) // (2 * p):
                        pairs.append((i + j, i + j + k))
            k //= 2
        p *= 2
    return pairs


def _count_in_sorted_group(v, cmp):
    one = lambda m, w: jnp.where(m, w, 0)
    m1 = cmp(v(7))
    m2 = cmp(jnp.where(m1, v(11), v(3)))
    m3 = cmp(jnp.where(m1, jnp.where(m2, v(13), v(9)), jnp.where(m2, v(5), v(1))))
    hi = jnp.where(m2, jnp.where(m3, v(14), v(12)), jnp.where(m3, v(10), v(8)))
    lo = jnp.where(m2, jnp.where(m3, v(6), v(4)), jnp.where(m3, v(2), v(0)))
    m4 = cmp(jnp.where(m1, hi, lo))
    m5 = cmp(v(15))
    return one(m1, 8) + one(m2, 4) + one(m3, 2) + one(m4, 1) + one(m5, 1)


def _attn_p_kernel(qT_ref, iqT_ref, iwT_ref, kb_ref, vT_ref, ikw_ref, dtab_ref, o_ref,
                   keys_ref, sorted_ref, rhs_ref, qbd_ref, m_ref, acc_ref, x_ref, *, k_top):
    qb = pl.program_id(1)
    t0 = qb * TQ
    n_chunks = (qb + 2) >> 1
    n_far = jnp.maximum(qb - 1, 0) >> 1
    half = LANES // 2

    zeros_half = jnp.zeros((half, TQ), BF16)
    for c in range(N_HEADS // 2):
        for hh in range(2):
            h = 2 * c + hh
            iq_h = iqT_ref[h * IDX_DIM:(h + 1) * IDX_DIM, :]
            rhs_ref[c, :, hh * TQ:(hh + 1) * TQ] = jnp.concatenate([iq_h, zeros_half], axis=0)
            q_h = qT_ref[h * HEAD_DIM:(h + 1) * HEAD_DIM, :]
            parts = [zeros_half, q_h] if hh else [q_h, zeros_half]
            qbd_ref[c, :, hh * TQ:(hh + 1) * TQ] = jnp.concatenate(parts, axis=0)

    w = (iwT_ref[...] * (IDX_HEADS ** -0.5)) * (IDX_DIM ** -0.5)
    s_iota = lax.broadcasted_iota(I32, (TK, TQ), 0)
    t_glob = t0 + lax.broadcasted_iota(I32, (TK, TQ), 1)

    def score_body(j, carry, band):
        off = pl.multiple_of(j * TK, TK)
        ikc = ikw_ref[pl.ds(off, TK), :]
        score = jnp.zeros((TK, TQ), F32)
        for c in range(N_HEADS // 2):
            dd = _dot(ikc, rhs_ref[c])
            for hh in range(2):
                h = 2 * c + hh
                score = score + jnp.maximum(dd[:, hh * TQ:(hh + 1) * TQ], 0.0) * w[h:h + 1, :]
        key = _sortable_key(score)
        if band:
            key = jnp.where(s_iota + off <= t_glob, key, INT_MIN)
        keys_ref[pl.ds(off, TK), :] = key
        per = TK // GROUP
        v = [key[i * per:(i + 1) * per, :] for i in range(GROUP)]
        for a, b in _sort_network(GROUP):
            v[a], v[b] = jnp.maximum(v[a], v[b]), jnp.minimum(v[a], v[b])
        for i in range(GROUP):
            sorted_ref[j, i] = v[i]
        return carry

    _fori_pairs(0, n_far, functools.partial(score_body, band=False), 0)
    _fori_pairs(n_far, n_chunks, functools.partial(score_body, band=True), 0)

    @pl.when((n_chunks & 1) == 1)
    def _():
        sorted_ref[n_chunks] = jnp.full(sorted_ref.shape[1:], INT_MIN, I32)

    def count(cmp):
        def body(i, acc):
            for u in range(2):
                acc = acc + _count_in_sorted_group(lambda r: sorted_ref[2 * i + u, r], cmp)
            return acc
        acc = lax.fori_loop(0, (n_chunks + 1) >> 1, body, jnp.zeros((TK // GROUP, TQ), I32))
        return acc.sum(axis=0, keepdims=True)

    def bit_body(p, carry):
        thr, n_ge = carry
        bit = jnp.left_shift(jnp.int32(1), 31 - p)
        cand = thr ^ bit
        cnt = count(lambda kc: kc >= cand)
        ok = cnt >= k_top
        return jnp.where(ok, cand, thr), jnp.where(ok, cnt, n_ge)

    thr, n_ge = lax.fori_loop(0, 32, bit_body, (jnp.full((1, TQ), INT_MIN, I32),
                                                jnp.zeros((1, TQ), I32)))
    any_ties = jnp.max(jnp.where((thr != INT_MIN) & (n_ge != k_top), 1, 0))

    fold = lambda a, op: op(a.reshape(TK // 8, 8, TQ), axis=0)

    def logits_body(j, carry, band, ties, need=None, ltri=None):
        taken, mx = carry
        off = pl.multiple_of(j * TK, TK)
        keyc = keys_ref[pl.ds(off, TK), :]
        if ties:
            eq = keyc == thr
            prefix = _dot(ltri, eq.astype(BF16))
            sel = (keyc > thr) | (eq & (prefix + taken <= need))
            taken = taken + prefix[TK - 1:TK, :]
        else:
            sel = keyc >= thr
        if band:
            sel = sel & (s_iota + off <= t_glob)
            tab = (t0 - off) // TQ
        kc = kb_ref[pl.ds(off, TK), :]
        new_mx = []
        for c in range(N_HEADS // 2):
            lg = _dot(kc[:, c * LANES:(c + 1) * LANES], qbd_ref[c])
            for hh in range(2):
                h = 2 * c + hh
                x = lg[:, hh * TQ:(hh + 1) * TQ]
                if band:
                    x = x + dtab_ref[tab, h]
                x = jnp.where(sel, x * LOG2E, NEG)
                x_ref[h, pl.ds(off, TK), :] = x
                new_mx.append(jnp.maximum(mx[h * 8:(h + 1) * 8, :], fold(x, jnp.max)))
        return taken, jnp.concatenate(new_mx, axis=0)

    def all_logits(ties):
        extra = {}
        if ties:
            extra["need"] = (k_top - count(lambda kc: kc > thr)).astype(F32)
            extra["ltri"] = (lax.broadcasted_iota(I32, (TK, TK), 1)
                             <= lax.broadcasted_iota(I32, (TK, TK), 0)).astype(BF16)
        body = functools.partial(logits_body, ties=ties, **extra)
        carry = (jnp.zeros((1, TQ), F32), jnp.full((N_HEADS * 8, TQ), NEG, F32))
        carry = _fori_pairs(0, n_far, functools.partial(body, band=False), carry)
        return _fori_pairs(n_far, n_chunks, functools.partial(body, band=True), carry)[1]

    mx = lax.cond(any_ties > 0, lambda: all_logits(True), lambda: all_logits(False))
    m_ref[...] = jnp.concatenate(
        [mx[h * 8:(h + 1) * 8, :].max(axis=0, keepdims=True) for h in range(N_HEADS)], axis=0)

    acc_ref[...] = jnp.zeros(acc_ref.shape, F32)

    def pv_body(j, lsum):
        off = pl.multiple_of(j * TK, TK)
        new_lsum = []
        for h in range(N_HEADS):
            p = jnp.exp2(x_ref[h, pl.ds(off, TK), :] - m_ref[h:h + 1, :])
            new_lsum.append(lsum[h * 8:(h + 1) * 8, :] + fold(p, jnp.sum))
            rows = slice(h * HEAD_DIM, (h + 1) * HEAD_DIM)
            acc_ref[rows, :] += _dot(vT_ref[j, rows, :], p.astype(BF16))
        return jnp.concatenate(new_lsum, axis=0)

    lsum = _fori_pairs(0, n_chunks, pv_body, jnp.zeros((N_HEADS * 8, TQ), F32))
    for h in range(N_HEADS):
        rows = slice(h * HEAD_DIM, (h + 1) * HEAD_DIM)
        inv = 1.0 / lsum[h * 8:(h + 1) * 8, :].sum(axis=0, keepdims=True)
        acc_ref[rows, :] = acc_ref[rows, :] * inv
    o_ref[...] = acc_ref[...].T.astype(BF16)


def _attn_p(qT, iqT, iwT, kb, vT, ikwb, dtab, nb, t):
    n = nb * t
    nq = t // TQ
    k_top = min(TOPK_MAX, t // 4)
    colT = lambda r: pl.BlockSpec((r, TQ), lambda b, q: (0, b * nq + q))
    return pl.pallas_call(
        functools.partial(_attn_p_kernel, k_top=k_top), grid=(nb, nq),
        in_specs=[colT(ATT_W), colT(ATT_W), colT(IDX_HEADS),
                  pl.BlockSpec((t, ATT_W), lambda b, q: (b, 0)),
                  pl.BlockSpec((t // TK, ATT_W, TK), lambda b, q: (b, 0, 0)),
                  pl.BlockSpec((t, LANES), lambda b, q: (b, 0)),
                  _const_spec(dtab.shape)],
        out_specs=pl.BlockSpec((TQ, ATT_W), lambda b, q: (b * nq + q, 0)),
        out_shape=jax.ShapeDtypeStruct((n, ATT_W), BF16),
        scratch_shapes=[pltpu.VMEM((t, TQ), I32),
                        pltpu.VMEM((t // TK + 1, GROUP, TK // GROUP, TQ), I32),
                        pltpu.VMEM((N_HEADS // 2, LANES, 2 * TQ), BF16),
                        pltpu.VMEM((N_HEADS // 2, LANES, 2 * TQ), BF16),
                        pltpu.VMEM((N_HEADS, TQ), F32), pltpu.VMEM((ATT_W, TQ), F32),
                        pltpu.VMEM((N_HEADS, t, TQ), F32)],
        compiler_params=_params(2), name="attn_prompt")(qT, iqT, iwT, kb, vT, ikwb, dtab)


def _select_s_kernel(pt_ref, iq_ref, wcol_ref, iknew_ref, pool_ref, mask_ref,
                     ikbuf_ref, score_ref, sem, *, k_top, layer_off, t_len):
    g = pl.program_id(0)
    gs, _, s_pad = ikbuf_ref.shape
    n_pages = pt_ref.shape[1]
    page = pool_ref.shape[2]
    past = n_pages * page

    def page_copy(i, j):
        return pltpu.make_async_copy(pool_ref.at[layer_off + pt_ref[g * gs + i, j]],
                                     ikbuf_ref.at[i, :, j * page:(j + 1) * page], sem)

    def start_body(i, c):
        for j in range(n_pages):
            page_copy(i, j).start()
        return c

    lax.fori_loop(0, gs, start_body, 0)
    ikbuf_ref[:, :, past:s_pad] = jnp.zeros((gs, IDX_DIM, s_pad - past), F32)
    ikbuf_ref[:, :, past:past + t_len] = iknew_ref[...]

    def wait_body(i, c):
        for j in range(n_pages):
            page_copy(i, j).wait()
        return c

    lax.fori_loop(0, gs, wait_body, 0)

    tile = score_ref.shape[1]
    per_tile = tile // t_len

    def score_body(it, c):
        for u in range(per_tile):
            i = it * per_tile + u
            dd = _dot(iq_ref[i], ikbuf_ref[i].astype(BF16))
            wv = (wcol_ref[i] * (IDX_HEADS ** -0.5)) * (IDX_DIM ** -0.5)
            sc = (jnp.maximum(dd, 0.0) * wv).reshape(t_len, IDX_HEADS, s_pad).sum(axis=1)
            score_ref[it, u * t_len:(u + 1) * t_len, :] = sc
        return c

    lax.fori_loop(0, gs // per_tile, score_body, 0)

    rows = gs * t_len
    s_idx = lax.broadcasted_iota(I32, (rows, s_pad), 1)
    t_idx = lax.broadcasted_iota(I32, (rows, s_pad), 0) % t_len
    valid = s_idx <= past + t_idx
    keys = jnp.where(valid, _sortable_key(score_ref[...].reshape(rows, s_pad)), INT_MIN)

    def bit_body(p, thr):
        bit = jnp.left_shift(jnp.int32(1), 31 - p)
        cand = thr ^ bit
        cnt = jnp.sum((keys >= cand).astype(I32), axis=1, keepdims=True)
        return jnp.where(cnt >= k_top, cand, thr)

    thr = lax.fori_loop(0, 32, bit_body, jnp.full((rows, 1), INT_MIN, I32))
    gt = keys > thr
    eq = keys == thr
    need = (k_top - jnp.sum(gt.astype(I32), axis=1, keepdims=True)).astype(F32)
    utri = (lax.broadcasted_iota(I32, (LANES, LANES), 0) <= lax.broadcasted_iota(I32, (LANES, LANES), 1)
            ).astype(BF16)
    eqb = eq.astype(BF16)
    carry = jnp.zeros((rows, 1), F32)
    for c in range(s_pad // LANES):
        cols = slice(c * LANES, (c + 1) * LANES)
        prefix = _dot(eqb[:, cols], utri) + carry
        take = gt[:, cols] | (eq[:, cols] & (prefix <= need))
        take = take & valid[:, cols]
        m = jnp.where(take, 0.0, NEG).reshape(gs // per_tile, tile, LANES)
        for u in range(per_tile):
            mask_ref[:, u, :, cols] = m[:, u * t_len:(u + 1) * t_len, :]
        carry = prefix[:, LANES - 1:LANES]


def _select_s(page_table, iq, wcol, iknew, pool, layer, gs, k_top):
    nb, rows_q, _ = iq.shape
    t_len = rows_q // IDX_HEADS
    n_pages = page_table.shape[1]
    page = pool.shape[2]
    s_pad = n_pages * page + LANES
    sublanes = 8
    assert sublanes % t_len == 0
    per_tile = sublanes // t_len
    assert nb % gs == 0 and gs % per_tile == 0
    kern = functools.partial(_select_s_kernel, k_top=k_top, layer_off=layer, t_len=t_len)
    grid_spec = pltpu.PrefetchScalarGridSpec(
        num_scalar_prefetch=1, grid=(nb // gs,),
        in_specs=[pl.BlockSpec((gs, rows_q, IDX_DIM), lambda g, pt: (g, 0, 0)),
                  pl.BlockSpec((gs, rows_q, 1), lambda g, pt: (g, 0, 0)),
                  pl.BlockSpec((gs, IDX_DIM, t_len), lambda g, pt: (g, 0, 0)),
                  pl.BlockSpec(memory_space=pl.ANY)],
        out_specs=pl.BlockSpec((gs // per_tile, per_tile, t_len, s_pad), lambda g, pt: (g, 0, 0, 0)),
        scratch_shapes=[pltpu.VMEM((gs, IDX_DIM, s_pad), F32),
                        pltpu.VMEM((gs // per_tile, sublanes, s_pad), F32),
                        pltpu.SemaphoreType.DMA(())])
    return pl.pallas_call(
        kern, grid_spec=grid_spec,
        out_shape=jax.ShapeDtypeStruct((nb // per_tile, per_tile, t_len, s_pad), F32),
        compiler_params=_params(1), name="select_sample")(page_table, iq, wcol, iknew, pool)


def _attn_s_kernel(pt_ref, q_ref, knew_ref, vnew_ref, mask_ref, dtab_ref, *rest, n_pages, t_len):
    k_pages = rest[:n_pages]
    v_pages = rest[n_pages:2 * n_pages]
    o_ref, kpad_ref, vpad_ref = rest[2 * n_pages:]
    rows = t_len * N_HEADS
    page = k_pages[0].shape[2]

    def flat_bf16(page_ref):
        return page_ref[...].reshape(ATT_W, page).astype(BF16)

    def per_head_rows(x):
        return jnp.concatenate(
            [jnp.broadcast_to(x[t:t + 1, :], (N_HEADS, x.shape[1])) for t in range(t_len)], axis=0)

    col_head = lax.broadcasted_iota(I32, (rows, ATT_W), 1) // HEAD_DIM
    row_head = lax.broadcasted_iota(I32, (rows, ATT_W), 0) % N_HEADS
    own = col_head == row_head
    qbd = jnp.where(own, per_head_rows(q_ref[...].astype(F32)), 0.0).astype(BF16)
    kpad_ref[...] = jnp.zeros(kpad_ref.shape, F32)
    vpad_ref[...] = jnp.zeros(vpad_ref.shape, F32)
    kpad_ref[0:t_len, :] = knew_ref[...]
    vpad_ref[0:t_len, :] = vnew_ref[...]
    knew = kpad_ref[...].astype(BF16)
    vnew = vpad_ref[...].astype(BF16)
    logits = [_dot(qbd, flat_bf16(kp)) for kp in k_pages] + [_dot_nt(qbd, knew)]
    x = jnp.concatenate(logits, axis=1)
    x = x + dtab_ref[...] + per_head_rows(mask_ref[...])
    m = x.max(axis=1, keepdims=True)
    p = jnp.exp(x - m)
    l = p.sum(axis=1, keepdims=True)
    pb = p.astype(BF16)
    out = _dot(pb[:, n_pages * page:], vnew)
    for j, vp in enumerate(v_pages):
        out = out + _dot_nt(pb[:, j * page:(j + 1) * page], flat_bf16(vp))
    out = out / l
    out = jnp.where(own, out, 0.0).reshape(t_len, N_HEADS, ATT_W).sum(axis=1)
    o_ref[...] = out.astype(BF16)


def _attn_s(page_table, q, knew, vnew, mask, dtab, k_pool, v_pool, layer_off):
    nb, t_len, _ = q.shape
    n_pages = page_table.shape[1]
    page = k_pool.shape[3]
    s_pad = n_pages * page + LANES

    def page_spec(j):
        return pl.BlockSpec((None, N_HEADS, HEAD_DIM, page),
                            lambda b, pt, j=j: (layer_off + pt[b, j], 0, 0, 0))

    seq = lambda w: pl.BlockSpec((None, t_len, w), lambda b, pt: (b, 0, 0))
    per_tile = mask.shape[1]
    mask_spec = pl.BlockSpec((None, None, t_len, s_pad),
                             lambda b, pt: (b // per_tile, b % per_tile, 0, 0))
    grid_spec = pltpu.PrefetchScalarGridSpec(
        num_scalar_prefetch=1, grid=(nb,),
        in_specs=[seq(ATT_W), seq(ATT_W), seq(ATT_W), mask_spec,
                  pl.BlockSpec(dtab.shape, lambda b, pt: (0, 0))]
                 + [page_spec(j) for j in range(n_pages)] * 2,
        out_specs=seq(ATT_W),
        scratch_shapes=[pltpu.VMEM((page, ATT_W), F32), pltpu.VMEM((page, ATT_W), F32)])
    kern = functools.partial(_attn_s_kernel, n_pages=n_pages, t_len=t_len)
    return pl.pallas_call(
        kern, grid_spec=grid_spec, out_shape=jax.ShapeDtypeStruct((nb, t_len, ATT_W), BF16),
        compiler_params=_params(1), name="attn_sample")(
            page_table, q, knew, vnew, mask, dtab, *([k_pool] * n_pages), *([v_pool] * n_pages))


def _post_kernel(x_ref, ya_ref, o_ref, ga_ref, gb_ref, p_ref, wr_ref, wa_ref, wo_ref,
                 nf_ref, wg_ref, wu_ref, wd_ref, np_ref, wpg_ref, wpp_ref, nfin_ref,
                 out_ref, *, final):
    a = _dot(ya_ref[...], wr_ref[...])
    b = _dot(o_ref[...], wa_ref[...])
    mix = _sigmoid(ga_ref[...]) * a + _sigmoid(gb_ref[...]) * b
    x = x_ref[...] + _dot(mix.astype(BF16), wo_ref[...])
    h2 = _rms(x, nf_ref[...]).astype(BF16)
    g = _dot(h2, wg_ref[...])
    u = _dot(h2, wu_ref[...])
    act = (g * _sigmoid(g)) * u
    x = x + _dot(act.astype(BF16), wd_ref[...])
    h3 = _rms(x, np_ref[...]).astype(BF16)
    x = x + _sigmoid(_dot(h3, wpg_ref[...])) * _dot(p_ref[...].astype(BF16), wpp_ref[...])
    if final:
        x = _rms(x, nfin_ref[...])
    out_ref[...] = x


def _layer_spec(w, l):
    tail = (0,) * (w.ndim - 1)
    return pl.BlockSpec((None,) + w.shape[1:], lambda *_: (l,) + tail,
                        pipeline_mode=pl.Buffered(1))


def _post(x, ya, o, ga, gb, p, weights, tm, final, l, p_row0=0):
    n, d = x.shape
    row = lambda w: pl.BlockSpec((tm, w), lambda i: (i, 0))
    p_blk0 = p_row0 // tm
    return pl.pallas_call(
        functools.partial(_post_kernel, final=final), grid=(n // tm,),
        in_specs=[row(d), row(d), row(ATT_W), row(d), row(d),
                  pl.BlockSpec((tm, p.shape[1]), lambda i: (i + p_blk0, 0))]
                 + [_layer_spec(c, l) if c.ndim == 3 else _const_spec(c.shape) for c in weights],
        out_specs=row(d), out_shape=jax.ShapeDtypeStruct((n, d), F32),
        compiler_params=_params(1), name="post")(x, ya, o, ga, gb, p, *weights)


def _rel_bucket(n, n_buckets):
    max_exact = n_buckets // 2
    nf = jnp.maximum(n, 1).astype(F32)
    large = max_exact + (jnp.log(nf / max_exact) / math.log(MAX_DISTANCE / max_exact)
                         * (n_buckets - max_exact)).astype(I32)
    large = jnp.minimum(large, n_buckets - 1)
    return jnp.where(n < max_exact, n, large)


def _bias_delta(rel_bias, n):
    nbk = rel_bias.shape[0]
    b = rel_bias.astype(F32)
    onehot = _rel_bucket(jnp.maximum(n, 0), nbk)[..., None] == jnp.arange(nbk, dtype=I32)
    picked = jnp.where(onehot[..., None], b, 0.0).sum(axis=-2)
    return picked - b[nbk - 1]


def _block_diag_groups(w, group):
    nbk, c, _ = w.shape
    per = group // c
    wg = w.reshape(nbk // per, per, c, c)
    eye = jnp.eye(per, dtype=w.dtype)
    return jnp.einsum('gpcd,pq->gpcqd', wg, eye).reshape(nbk // per, group, group)


def kernel(x_prompt, x_sample, p_prompt, p_sample, cache_k, cache_v, cache_idx_k, state_rglru_h,
           state_conv, page_table, rel_bias, norm_mix, w_in, conv_w, conv_b, w_rg_a, b_rg_a,
           w_rg_x, b_rg_x, lru_lambda, w_rnn_out, w_att_out, w_o, norm_ffn, w_ffn_gate, w_ffn_up,
           w_ffn_down, norm_ple, w_ple_gate, w_ple_proj, norm_final):
    nb, t, d = x_prompt.shape
    sb, st, _ = x_sample.shape
    depth = w_in.shape[0]
    n_pool, page = cache_k.shape[1], cache_k.shape[2]
    n_pages = page_table.shape[1]
    past = n_pages * page
    cw = conv_w.shape[1]
    s_pad = past + LANES
    k_top_s = min(TOPK_MAX, (past + st) // 4)
    row2 = lambda v: v.reshape(1, -1).astype(F32)

    idx3 = jnp.arange(3, dtype=I32)[:, None, None]
    s_rel = jnp.arange(TK, dtype=I32)[None, :, None]
    t_rel = jnp.arange(TQ, dtype=I32)[None, None, :]
    dtab_p = _bias_delta(rel_bias, idx3 * TQ + t_rel - s_rel).transpose(0, 3, 1, 2)
    s_all = jnp.arange(s_pad, dtype=I32)[None, :]
    q_pos = past + jnp.arange(st, dtype=I32)[:, None]
    dtab_s = _bias_delta(rel_bias, q_pos - s_all)
    dtab_s = dtab_s.transpose(0, 2, 1).reshape(st * N_HEADS, s_pad)

    k_pool = cache_k.transpose(0, 1, 3, 4, 2).reshape(depth * n_pool, N_HEADS, HEAD_DIM, page)
    v_pool = cache_v.transpose(0, 1, 3, 4, 2).reshape(depth * n_pool, N_HEADS, HEAD_DIM, page)
    ik_pool = cache_idx_k.transpose(0, 1, 3, 2).reshape(depth * n_pool, IDX_DIM, page)

    w_r, w_a, w_oo = w_rnn_out.astype(BF16), w_att_out.astype(BF16), w_o.astype(BF16)
    w_fg, w_fu, w_fd = w_ffn_gate.astype(BF16), w_ffn_up.astype(BF16), w_ffn_down.astype(BF16)
    w_pg, w_pp = w_ple_gate.astype(BF16), w_ple_proj.astype(BF16)
    p_rows = p_prompt.reshape(depth * nb * t, -1)

    xp = x_prompt.reshape(nb * t, d)
    xs = x_sample.transpose(1, 0, 2).reshape(st * sb, d)
    outs_p, outs_s = [], []
    for l in range(depth):
        splits = [d, 2 * d, 2 * d + ATT_W, 2 * d + 2 * ATT_W, 2 * d + 3 * ATT_W,
                  2 * d + 3 * ATT_W + IDX_HEADS * IDX_DIM,
                  2 * d + 3 * ATT_W + IDX_HEADS * IDX_DIM + IDX_DIM,
                  2 * d + 3 * ATT_W + IDX_HEADS * IDX_DIM + IDX_DIM + IDX_HEADS,
                  3 * d + 3 * ATT_W + IDX_HEADS * IDX_DIM + IDX_DIM + IDX_HEADS]
        w_xr, w_gr, w_q, w_k, w_v, w_iq, w_ik, w_iw, w_ga, w_gb = jnp.split(w_in[l], splits, axis=1)
        w_ikw = jnp.concatenate(
            [w_ik, w_iw, jnp.zeros((d, LANES - IDX_DIM - IDX_HEADS), F32)], axis=1)
        wm_p = jnp.concatenate([w_xr, w_gr, w_ga, w_gb, w_k, w_ikw], axis=1).astype(BF16)
        wm_s = jnp.concatenate([w_xr, w_gr, w_ga, w_gb, w_k, w_v, w_ikw, w_q, w_iq], axis=1).astype(BF16)
        wt_p = jnp.concatenate([w_q, w_iq, w_k, w_v, w_ik, w_iw], axis=1).T.astype(BF16)
        gw = 2 * LANES
        wa_bd = _block_diag_groups(w_rg_a[l], gw).astype(BF16)
        wx_bd = _block_diag_groups(w_rg_x[l], gw).astype(BF16)
        lru_consts = (conv_w[l], row2(conv_b[l]), wa_bd, wx_bd, row2(b_rg_a[l]), row2(b_rg_x[l]),
                      row2(lru_lambda[l]))
        post_w = (w_r, w_a, w_oo, row2(norm_ffn[l]), w_fg, w_fu, w_fd, row2(norm_ple[l]), w_pg, w_pp,
                  row2(norm_final))
        final = l == depth - 1

        (xr, gr, ga, gb, kb, ikwb, qT, iqT, vTb, iwT, kT, vT, ikT) = _inproj_p(
            xp, row2(norm_mix[l]), wm_p, wt_p, tm=256, nb=nb)
        ya, h_last, new_buf = _rglru_p(xr.reshape(nb, t, d), gr.reshape(nb, t, d), *lru_consts, tt=128)
        o = _attn_p(qT, iqT, iwT, kb, vTb, ikwb, dtab_p, nb, t)
        xp = _post(xp, ya.reshape(nb * t, d), o, ga, gb, p_rows, post_w, tm=256, final=final, l=l,
                   p_row0=l * nb * t)
        heads_last = lambda aT: aT.reshape(nb, N_HEADS, HEAD_DIM, t).transpose(0, 3, 1, 2)
        outs_p.append((heads_last(kT), heads_last(vT), ikT.transpose(0, 2, 1), h_last, new_buf))

        (xr, gr, ga, gb, k, v, ikw, q, iq) = _inproj_s(xs, row2(norm_mix[l]), wm_s, tm=st * sb)
        tmaj = lambda a2: a2.reshape(st, sb, -1)
        ya, h_last, new_buf = _rglru_s(tmaj(xr), tmaj(gr), state_conv[l].transpose(1, 0, 2),
                                       state_rglru_h[l], *lru_consts)
        bmaj = lambda a2: a2.reshape(st, sb, -1).transpose(1, 0, 2)
        k_b, v_b, ikw_b, q_b, iq_b = bmaj(k), bmaj(v), bmaj(ikw), bmaj(q), bmaj(iq)
        iq_rows = iq_b.reshape(sb, st * IDX_HEADS, IDX_DIM)
        w_col = ikw_b[:, :, IDX_DIM:IDX_DIM + IDX_HEADS].reshape(sb, st * IDX_HEADS, 1)
        mask = _select_s(page_table, iq_rows, w_col, ikw_b[:, :, :IDX_DIM].transpose(0, 2, 1),
                         ik_pool, l * n_pool, gs=min(sb, 32), k_top=k_top_s)
        o = _attn_s(page_table, q_b, k_b, v_b, mask, dtab_s, k_pool, v_pool, l * n_pool)
        o = o.transpose(1, 0, 2).reshape(st * sb, ATT_W)
        xs = _post(xs, ya.reshape(st * sb, d), o, ga, gb,
                   p_sample[l].transpose(1, 0, 2).reshape(st * sb, -1), post_w,
                   tm=min(256, st * sb), final=final, l=l)
        outs_s.append((k_b.reshape(sb, st, N_HEADS, HEAD_DIM), v_b.reshape(sb, st, N_HEADS, HEAD_DIM),
                       ikw_b[:, :, :IDX_DIM], h_last, new_buf.transpose(1, 0, 2)))

    y_prompt = xp.reshape(nb, t, d)
    y_sample = xs.reshape(st, sb, d).transpose(1, 0, 2)
    stack = lambda outs, i: jnp.stack([o_[i] for o_ in outs])
    return (y_prompt, y_sample,
            stack(outs_p, 0), stack(outs_p, 1), stack(outs_p, 2), stack(outs_p, 3), stack(outs_p, 4),
            stack(outs_s, 0), stack(outs_s, 1), stack(outs_s, 2), stack(outs_s, 3), stack(outs_s, 4))
```

```python
import functools
import math

import jax
import jax.numpy as jnp
from jax import lax
from jax.experimental import pallas as pl
from jax.experimental.pallas import tpu as pltpu

F32 = jnp.float32
BF16 = jnp.bfloat16
I32 = jnp.int32

N_HEADS = 8
HEAD_DIM = 64
ATT_W = N_HEADS * HEAD_DIM
IDX_HEADS = 8
IDX_DIM = 64
TOPK_MAX = 256
LRU_C = 8.0
MAX_DISTANCE = 128
EPS = 1e-6

LANES = 128
TQ = 128
TK = 256
NEG = -1e30
LOG2E = math.log2(math.e)
INT_MIN = -2147483648
VMEM_LIMIT = 56 * 1024 * 1024


def _dot(a, b):
    return jnp.dot(a, b, preferred_element_type=F32)


def _dot_nt(a, b):
    return lax.dot_general(a, b, (((1,), (1,)), ((), ())), preferred_element_type=F32)


def _sigmoid(x):
    return 0.5 * jnp.tanh(0.5 * x) + 0.5


def _gelu_tanh(x):
    c = math.sqrt(2.0 / math.pi)
    return 0.5 * x * (1.0 + jnp.tanh(c * (x + 0.044715 * (x * x * x))))


def _rms(x, g):
    y = x * lax.rsqrt(jnp.mean(x * x, axis=-1, keepdims=True) + EPS)
    return y * g


def _params(n_axes=1):
    return pltpu.CompilerParams(dimension_semantics=("arbitrary",) * n_axes,
                                vmem_limit_bytes=VMEM_LIMIT)


def _fori_pairs(lo, hi, body, carry):
    n = hi - lo

    def pair(i, c):
        j = lo + 2 * i
        return body(j + 1, body(j, c))

    carry = lax.fori_loop(0, n >> 1, pair, carry)
    return lax.cond((n & 1) == 1, lambda c: body(hi - 1, c), lambda c: c, carry)


def _const_spec(shape):
    nd = len(shape)
    return pl.BlockSpec(shape, lambda *_: (0,) * nd, pipeline_mode=pl.Buffered(1))


def _whole_out_spec(shape):
    nd = len(shape)
    return pl.BlockSpec(shape, lambda *_: (0,) * nd)


def _inproj_p_kernel(x_ref, g_ref, wm_ref, wt_ref, kT_all_ref, vT_all_ref, ikT_all_ref,
                     xr_ref, gr_ref, ga_ref, gb_ref, kb_ref, ikwb_ref,
                     qT_ref, iqT_ref, vTb_ref, iwT_ref, kT_ref, vT_ref, ikT_ref):
    del kT_all_ref, vT_all_ref, ikT_all_ref
    d = x_ref.shape[1]
    hb = _rms(x_ref[...], g_ref[...]).astype(BF16)
    z = _dot(hb, wm_ref[...])
    xr_ref[...] = z[:, 0:d]
    gr_ref[...] = z[:, d:2 * d]
    ga_ref[...] = z[:, 2 * d:3 * d]
    gb_ref[...] = z[:, 3 * d:4 * d]
    o = 4 * d
    kb_ref[...] = z[:, o:o + ATT_W].astype(BF16)
    ikwb_ref[...] = z[:, o + ATT_W:o + ATT_W + LANES].astype(BF16)
    zt = _dot_nt(wt_ref[...], hb)
    qT_ref[...] = (zt[0:ATT_W] * (HEAD_DIM ** -0.5)).astype(BF16)
    iqT_ref[...] = zt[ATT_W:2 * ATT_W].astype(BF16)
    kT_ref[...] = zt[2 * ATT_W:3 * ATT_W]
    vt = zt[3 * ATT_W:4 * ATT_W]
    vT_ref[...] = vt
    for c in range(vTb_ref.shape[0]):
        vTb_ref[c] = vt[:, c * TK:(c + 1) * TK].astype(BF16)
    ikT_ref[...] = zt[4 * ATT_W:4 * ATT_W + IDX_DIM]
    iwT_ref[...] = zt[4 * ATT_W + IDX_DIM:4 * ATT_W + IDX_DIM + IDX_HEADS]


def _inproj_p(x, g, wm, wt, caches, tm, nb, l):
    n, d = x.shape
    t = n // nb
    per_b = t // tm
    row = lambda w: pl.BlockSpec((tm, w), lambda i: (i, 0))
    out_shape = (
        jax.ShapeDtypeStruct((n, d), F32), jax.ShapeDtypeStruct((n, d), F32),
        jax.ShapeDtypeStruct((n, d), F32), jax.ShapeDtypeStruct((n, d), F32),
        jax.ShapeDtypeStruct((n, ATT_W), BF16), jax.ShapeDtypeStruct((n, LANES), BF16),
        jax.ShapeDtypeStruct((ATT_W, n), BF16), jax.ShapeDtypeStruct((ATT_W, n), BF16),
        jax.ShapeDtypeStruct((n // TK, ATT_W, TK), BF16),
        jax.ShapeDtypeStruct((IDX_HEADS, n), F32),
    ) + tuple(jax.ShapeDtypeStruct(c.shape, c.dtype) for c in caches)
    colT = lambda r: pl.BlockSpec((r, tm), lambda i: (0, i))
    layer_seq = lambda r: pl.BlockSpec((None, None, r, tm), lambda i: (l, i // per_b, 0, i % per_b))
    out_specs = (row(d), row(d), row(d), row(d), row(ATT_W), row(LANES), colT(ATT_W), colT(ATT_W),
                 pl.BlockSpec((tm // TK, ATT_W, TK), lambda i: (i, 0, 0)), colT(IDX_HEADS),
                 layer_seq(ATT_W), layer_seq(ATT_W), layer_seq(IDX_DIM))
    n_plain_in, n_plain_out = 4, 10
    return pl.pallas_call(
        _inproj_p_kernel, grid=(n // tm,),
        in_specs=[row(d), _const_spec(g.shape), _const_spec(wm.shape), _const_spec(wt.shape)]
                 + [pl.BlockSpec(memory_space=pl.ANY)] * len(caches),
        out_specs=out_specs, out_shape=out_shape,
        input_output_aliases={n_plain_in + i: n_plain_out + i for i in range(len(caches))},
        compiler_params=_params(1), name="inproj_prompt")(x, g, wm, wt, *caches)


def _inproj_s_kernel(x_ref, g_ref, wm_ref, wt_ref,
                     xr_ref, gr_ref, ga_ref, gb_ref, k_ref, v_ref, ikw_ref, q_ref, iq_ref):
    d = x_ref.shape[1]
    hb = _rms(x_ref[...], g_ref[...]).astype(BF16)
    z = _dot(hb, wm_ref[...])
    xr_ref[...] = z[:, 0:d]
    gr_ref[...] = z[:, d:2 * d]
    ga_ref[...] = z[:, 2 * d:3 * d]
    gb_ref[...] = z[:, 3 * d:4 * d]
    o = 4 * d
    k_ref[...] = z[:, o:o + ATT_W]
    ikw_ref[...] = z[:, o + ATT_W:o + ATT_W + LANES]
    qiq = _dot_nt(hb, wt_ref[0:2 * ATT_W, :])
    q_ref[...] = (qiq[:, 0:ATT_W] * (HEAD_DIM ** -0.5)).astype(BF16)
    iq_ref[...] = qiq[:, ATT_W:2 * ATT_W].astype(BF16)
    v_ref[...] = _dot_nt(hb, wt_ref[3 * ATT_W:4 * ATT_W, :])


def _inproj_s(x, g, wm, wt, tm):
    n, d = x.shape
    row = lambda w: pl.BlockSpec((tm, w), lambda i: (i, 0))
    out_shape = (
        jax.ShapeDtypeStruct((n, d), F32), jax.ShapeDtypeStruct((n, d), F32),
        jax.ShapeDtypeStruct((n, d), F32), jax.ShapeDtypeStruct((n, d), F32),
        jax.ShapeDtypeStruct((n, ATT_W), F32), jax.ShapeDtypeStruct((n, ATT_W), F32),
        jax.ShapeDtypeStruct((n, LANES), F32),
        jax.ShapeDtypeStruct((n, ATT_W), BF16), jax.ShapeDtypeStruct((n, ATT_W), BF16),
    )
    out_specs = (row(d), row(d), row(d), row(d), row(ATT_W), row(ATT_W), row(LANES),
                 row(ATT_W), row(ATT_W))
    return pl.pallas_call(
        _inproj_s_kernel, grid=(n // tm,),
        in_specs=[row(d), _const_spec(g.shape), _const_spec(wm.shape), _const_spec(wt.shape)],
        out_specs=out_specs, out_shape=out_shape, compiler_params=_params(1),
        name="inproj_sample")(x, g, wm, wt)


def _lru_gates(xc, wa_ref, wx_ref, ba, bx, lam):
    xcb = xc.astype(BF16)
    gw = wa_ref.shape[1]
    r_parts, i_parts = [], []
    for g in range(wa_ref.shape[0]):
        xs = xcb[:, g * gw:(g + 1) * gw]
        r_parts.append(_dot(xs, wa_ref[g]))
        i_parts.append(_dot(xs, wx_ref[g]))
    r = _sigmoid(jnp.concatenate(r_parts, axis=1) + ba)
    i = _sigmoid(jnp.concatenate(i_parts, axis=1) + bx)
    log_sig_lam = jnp.minimum(lam, 0.0) - jnp.log(1.0 + jnp.exp(-jnp.abs(lam)))
    log_a = LRU_C * r * log_sig_lam
    a = jnp.exp(log_a)
    u = jnp.sqrt(1.0 - a * a) * (i * xc)
    return a, u


def _rglru_p_kernel(xr_ref, gr_ref, cw_ref, cb_ref, wa_ref, wx_ref, ba_ref, bx_ref, lam_ref,
                    ya_ref, hlast_ref, buf_ref,
                    xx_ref, a_ref, u_ref, h_ref):
    step = pl.program_id(0)
    nb, tt, d = xr_ref.shape
    cw = cw_ref.shape[0]

    @pl.when(step == 0)
    def _():
        xx_ref[:, 0:8, :] = jnp.zeros((nb, 8, d), F32)
        h_ref[...] = jnp.zeros_like(h_ref)

    for b in range(nb):
        x = xr_ref[b]
        xx_ref[b, 8:8 + tt, :] = x
        xc = cb_ref[...] + x * cw_ref[cw - 1:cw, :]
        for j in range(cw - 1):
            sh = cw - 1 - j
            xc = xc + xx_ref[b, 8 - sh:8 - sh + tt, :] * cw_ref[j:j + 1, :]
        xx_ref[b, 0:8, :] = x[tt - 8:tt, :]
        a, u = _lru_gates(xc, wa_ref, wx_ref, ba_ref[...], bx_ref[...], lam_ref[...])
        a_ref[b] = a
        u_ref[b] = u

    def scan_body(t, hs):
        new = []
        for b in range(nb):
            h = a_ref[b, pl.ds(t, 1), :] * hs[b] + u_ref[b, pl.ds(t, 1), :]
            u_ref[b, pl.ds(t, 1), :] = h
            new.append(h)
        return tuple(new)

    hs = lax.fori_loop(0, tt, scan_body, tuple(h_ref[b:b + 1, :] for b in range(nb)), unroll=8)
    for b in range(nb):
        h_ref[b:b + 1, :] = hs[b]
        ya_ref[b] = (u_ref[b] * _gelu_tanh(gr_ref[b])).astype(BF16)
        buf_ref[b] = xr_ref[b, tt - (cw - 1):tt, :]
    hlast_ref[...] = h_ref[...]


def _rglru_p(xr, gr, cw, cb, wa, wx, ba, bx, lam, tt):
    nb, t, d = xr.shape
    blk = pl.BlockSpec((nb, tt, d), lambda i: (0, i, 0))
    consts = [cw, cb, wa, wx, ba, bx, lam]
    return pl.pallas_call(
        _rglru_p_kernel, grid=(t // tt,),
        in_specs=[blk, blk] + [_const_spec(c.shape) for c in consts],
        out_specs=(blk, _whole_out_spec((nb, d)), _whole_out_spec((nb, cw.shape[0] - 1, d))),
        out_shape=(jax.ShapeDtypeStruct((nb, t, d), BF16), jax.ShapeDtypeStruct((nb, d), F32),
                   jax.ShapeDtypeStruct((nb, cw.shape[0] - 1, d), F32)),
        scratch_shapes=[pltpu.VMEM((nb, tt + 8, d), F32), pltpu.VMEM((nb, tt, d), F32),
                        pltpu.VMEM((nb, tt, d), F32), pltpu.VMEM((nb, d), F32)],
        compiler_params=_params(1), name="rglru_prompt")(xr, gr, *consts)


def _rglru_s_kernel(xr_ref, gr_ref, st_ref, h0_ref, cw_ref, cb_ref, wa_ref, wx_ref, ba_ref,
                    bx_ref, lam_ref, ya_ref, hlast_ref, buf_ref):
    t_len, nb, d = xr_ref.shape
    cw = cw_ref.shape[0]
    rows = [st_ref[j] for j in range(cw - 1)] + [xr_ref[t] for t in range(t_len)]
    h = h0_ref[...]
    for t in range(t_len):
        xc = cb_ref[...] + rows[t + cw - 1] * cw_ref[cw - 1:cw, :]
        for j in range(cw - 1):
            xc = xc + rows[t + j] * cw_ref[j:j + 1, :]
        a, u = _lru_gates(xc, wa_ref, wx_ref, ba_ref[...], bx_ref[...], lam_ref[...])
        h = a * h + u
        ya_ref[t] = (h * _gelu_tanh(gr_ref[t])).astype(BF16)
    hlast_ref[...] = h
    for j in range(cw - 1):
        buf_ref[j] = rows[t_len + j]


def _rglru_s(xr, gr, st, h0, cw, cb, wa, wx, ba, bx, lam):
    t_len, nb, d = xr.shape
    args = [xr, gr, st, h0, cw, cb, wa, wx, ba, bx, lam]
    return pl.pallas_call(
        _rglru_s_kernel, grid=(1,),
        in_specs=[_const_spec(a.shape) for a in args],
        out_specs=(_whole_out_spec((t_len, nb, d)), _whole_out_spec((nb, d)),
                   _whole_out_spec((cw.shape[0] - 1, nb, d))),
        out_shape=(jax.ShapeDtypeStruct((t_len, nb, d), BF16), jax.ShapeDtypeStruct((nb, d), F32),
                   jax.ShapeDtypeStruct((cw.shape[0] - 1, nb, d), F32)),
        compiler_params=_params(1), name="rglru_sample")(*args)


def _sortable_key(score):
    bits = pltpu.bitcast(score, I32)
    return bits ^ ((bits >> 31) & 0x7FFFFFFF)


GROUP = 16


def _sort_network(n):
    pairs = []
    p = 1
    while p < n:
        k = p
        while k >= 1:
            for j in range(k % p, n - k, 2 * k):
                for i in range(min(k, n - j - k)):
                    if (i + j) // (2 * p) == (i + j + k) // (2 * p):
                        pairs.append((i + j, i + j + k))
            k //= 2
        p *= 2
    return pairs


def _count_in_sorted_group(v, cmp):
    one = lambda m, w: jnp.where(m, w, 0)
    m1 = cmp(v(7))
    m2 = cmp(jnp.where(m1, v(11), v(3)))
    m3 = cmp(jnp.where(m1, jnp.where(m2, v(13), v(9)), jnp.where(m2, v(5), v(1))))
    hi = jnp.where(m2, jnp.where(m3, v(14), v(12)), jnp.where(m3, v(10), v(8)))
    lo = jnp.where(m2, jnp.where(m3, v(6), v(4)), jnp.where(m3, v(2), v(0)))
    m4 = cmp(jnp.where(m1, hi, lo))
    m5 = cmp(v(15))
    return one(m1, 8) + one(m2, 4) + one(m3, 2) + one(m4, 1) + one(m5, 1)


def _attn_p_kernel(qT_ref, iqT_ref, iwT_ref, kb_ref, vT_ref, ikw_ref, dtab_ref, o_ref,
                   keys_ref, sorted_ref, rhs_ref, qbd_ref, m_ref, acc_ref, x_ref, *, k_top):
    qb = pl.program_id(1)
    t0 = qb * TQ
    n_chunks = (qb + 2) >> 1
    n_far = jnp.maximum(qb - 1, 0) >> 1
    half = LANES // 2

    zeros_half = jnp.zeros((half, TQ), BF16)
    for c in range(N_HEADS // 2):
        for hh in range(2):
            h = 2 * c + hh
            iq_h = iqT_ref[h * IDX_DIM:(h + 1) * IDX_DIM, :]
            rhs_ref[c, :, hh * TQ:(hh + 1) * TQ] = jnp.concatenate([iq_h, zeros_half], axis=0)
            q_h = qT_ref[h * HEAD_DIM:(h + 1) * HEAD_DIM, :]
            parts = [zeros_half, q_h] if hh else [q_h, zeros_half]
            qbd_ref[c, :, hh * TQ:(hh + 1) * TQ] = jnp.concatenate(parts, axis=0)

    w = (iwT_ref[...] * (IDX_HEADS ** -0.5)) * (IDX_DIM ** -0.5)
    s_iota = lax.broadcasted_iota(I32, (TK, TQ), 0)
    t_glob = t0 + lax.broadcasted_iota(I32, (TK, TQ), 1)

    def score_body(j, carry, band):
        off = pl.multiple_of(j * TK, TK)
        ikc = ikw_ref[pl.ds(off, TK), :]
        score = jnp.zeros((TK, TQ), F32)
        for c in range(N_HEADS // 2):
            dd = _dot(ikc, rhs_ref[c])
            for hh in range(2):
                h = 2 * c + hh
                score = score + jnp.maximum(dd[:, hh * TQ:(hh + 1) * TQ], 0.0) * w[h:h + 1, :]
        key = _sortable_key(score)
        if band:
            key = jnp.where(s_iota + off <= t_glob, key, INT_MIN)
        keys_ref[pl.ds(off, TK), :] = key
        per = TK // GROUP
        v = [key[i * per:(i + 1) * per, :] for i in range(GROUP)]
        for a, b in _sort_network(GROUP):
            v[a], v[b] = jnp.maximum(v[a], v[b]), jnp.minimum(v[a], v[b])
        for i in range(GROUP):
            sorted_ref[j, i] = v[i]
        return carry

    _fori_pairs(0, n_far, functools.partial(score_body, band=False), 0)
    _fori_pairs(n_far, n_chunks, functools.partial(score_body, band=True), 0)

    @pl.when((n_chunks & 1) == 1)
    def _():
        sorted_ref[n_chunks] = jnp.full(sorted_ref.shape[1:], INT_MIN, I32)

    def count(cmp):
        def body(i, acc):
            for u in range(2):
                acc = acc + _count_in_sorted_group(lambda r: sorted_ref[2 * i + u, r], cmp)
            return acc
        acc = lax.fori_loop(0, (n_chunks + 1) >> 1, body, jnp.zeros((TK // GROUP, TQ), I32))
        return acc.sum(axis=0, keepdims=True)

    def bit_body(p, carry):
        thr, n_ge = carry
        bit = jnp.left_shift(jnp.int32(1), 31 - p)
        cand = thr ^ bit
        cnt = count(lambda kc: kc >= cand)
        ok = cnt >= k_top
        return jnp.where(ok, cand, thr), jnp.where(ok, cnt, n_ge)

    thr, n_ge = lax.fori_loop(0, 32, bit_body, (jnp.full((1, TQ), INT_MIN, I32),
                                                jnp.zeros((1, TQ), I32)))
    any_ties = jnp.max(jnp.where((thr != INT_MIN) & (n_ge != k_top), 1, 0))

    fold = lambda a, op: op(a.reshape(TK // 8, 8, TQ), axis=0)

    def logits_body(j, carry, band, ties, need=None, ltri=None):
        taken, mx = carry
        off = pl.multiple_of(j * TK, TK)
        keyc = keys_ref[pl.ds(off, TK), :]
        if ties:
            eq = keyc == thr
            prefix = _dot(ltri, eq.astype(BF16))
            sel = (keyc > thr) | (eq & (prefix + taken <= need))
            taken = taken + prefix[TK - 1:TK, :]
        else:
            sel = keyc >= thr
        if band:
            sel = sel & (s_iota + off <= t_glob)
            tab = (t0 - off) // TQ
        kc = kb_ref[pl.ds(off, TK), :]
        new_mx = []
        for c in range(N_HEADS // 2):
            lg = _dot(kc[:, c * LANES:(c + 1) * LANES], qbd_ref[c])
            for hh in range(2):
                h = 2 * c + hh
                x = lg[:, hh * TQ:(hh + 1) * TQ]
                if band:
                    x = x + dtab_ref[tab, h]
                x = jnp.where(sel, x * LOG2E, NEG)
                x_ref[h, pl.ds(off, TK), :] = x
                new_mx.append(jnp.maximum(mx[h * 8:(h + 1) * 8, :], fold(x, jnp.max)))
        return taken, jnp.concatenate(new_mx, axis=0)

    def all_logits(ties):
        extra = {}
        if ties:
            extra["need"] = (k_top - count(lambda kc: kc > thr)).astype(F32)
            extra["ltri"] = (lax.broadcasted_iota(I32, (TK, TK), 1)
                             <= lax.broadcasted_iota(I32, (TK, TK), 0)).astype(BF16)
        body = functools.partial(logits_body, ties=ties, **extra)
        carry = (jnp.zeros((1, TQ), F32), jnp.full((N_HEADS * 8, TQ), NEG, F32))
        carry = _fori_pairs(0, n_far, functools.partial(body, band=False), carry)
        return _fori_pairs(n_far, n_chunks, functools.partial(body, band=True), carry)[1]

    mx = lax.cond(any_ties > 0, lambda: all_logits(True), lambda: all_logits(False))
    m_ref[...] = jnp.concatenate(
        [mx[h * 8:(h + 1) * 8, :].max(axis=0, keepdims=True) for h in range(N_HEADS)], axis=0)

    acc_ref[...] = jnp.zeros(acc_ref.shape, F32)

    def pv_body(j, lsum):
        off = pl.multiple_of(j * TK, TK)
        new_lsum = []
        for h in range(N_HEADS):
            p = jnp.exp2(x_ref[h, pl.ds(off, TK), :] - m_ref[h:h + 1, :])
            new_lsum.append(lsum[h * 8:(h + 1) * 8, :] + fold(p, jnp.sum))
            rows = slice(h * HEAD_DIM, (h + 1) * HEAD_DIM)
            acc_ref[rows, :] += _dot(vT_ref[j, rows, :], p.astype(BF16))
        return jnp.concatenate(new_lsum, axis=0)

    lsum = _fori_pairs(0, n_chunks, pv_body, jnp.zeros((N_HEADS * 8, TQ), F32))
    for h in range(N_HEADS):
        rows = slice(h * HEAD_DIM, (h + 1) * HEAD_DIM)
        inv = 1.0 / lsum[h * 8:(h + 1) * 8, :].sum(axis=0, keepdims=True)
        acc_ref[rows, :] = acc_ref[rows, :] * inv
    o_ref[...] = acc_ref[...].T.astype(BF16)


def _attn_p(qT, iqT, iwT, kb, vT, ikwb, dtab, nb, t):
    n = nb * t
    nq = t // TQ
    k_top = min(TOPK_MAX, t // 4)
    colT = lambda r: pl.BlockSpec((r, TQ), lambda b, q: (0, b * nq + q))
    return pl.pallas_call(
        functools.partial(_attn_p_kernel, k_top=k_top), grid=(nb, nq),
        in_specs=[colT(ATT_W), colT(ATT_W), colT(IDX_HEADS),
                  pl.BlockSpec((t, ATT_W), lambda b, q: (b, 0)),
                  pl.BlockSpec((t // TK, ATT_W, TK), lambda b, q: (b, 0, 0)),
                  pl.BlockSpec((t, LANES), lambda b, q: (b, 0)),
                  _const_spec(dtab.shape)],
        out_specs=pl.BlockSpec((TQ, ATT_W), lambda b, q: (b * nq + q, 0)),
        out_shape=jax.ShapeDtypeStruct((n, ATT_W), BF16),
        scratch_shapes=[pltpu.VMEM((t, TQ), I32),
                        pltpu.VMEM((t // TK + 1, GROUP, TK // GROUP, TQ), I32),
                        pltpu.VMEM((N_HEADS // 2, LANES, 2 * TQ), BF16),
                        pltpu.VMEM((N_HEADS // 2, LANES, 2 * TQ), BF16),
                        pltpu.VMEM((N_HEADS, TQ), F32), pltpu.VMEM((ATT_W, TQ), F32),
                        pltpu.VMEM((N_HEADS, t, TQ), F32)],
        compiler_params=_params(2), name="attn_prompt")(qT, iqT, iwT, kb, vT, ikwb, dtab)


def _select_s_kernel(pt_ref, iq_ref, wcol_ref, iknew_ref, pool_ref, mask_ref,
                     ikbuf_ref, score_ref, sem, *, k_top, layer_off, t_len):
    g = pl.program_id(0)
    gs, _, s_pad = ikbuf_ref.shape
    n_pages = pt_ref.shape[1]
    page = pool_ref.shape[2]
    past = n_pages * page

    def page_copy(i, j):
        return pltpu.make_async_copy(pool_ref.at[layer_off + pt_ref[g * gs + i, j]],
                                     ikbuf_ref.at[i, :, j * page:(j + 1) * page], sem)

    def start_body(i, c):
        for j in range(n_pages):
            page_copy(i, j).start()
        return c

    lax.fori_loop(0, gs, start_body, 0)
    ikbuf_ref[:, :, past:s_pad] = jnp.zeros((gs, IDX_DIM, s_pad - past), F32)
    ikbuf_ref[:, :, past:past + t_len] = iknew_ref[...]

    def wait_body(i, c):
        for j in range(n_pages):
            page_copy(i, j).wait()
        return c

    lax.fori_loop(0, gs, wait_body, 0)

    tile = score_ref.shape[1]
    per_tile = tile // t_len

    def score_body(it, c):
        for u in range(per_tile):
            i = it * per_tile + u
            dd = _dot(iq_ref[i], ikbuf_ref[i].astype(BF16))
            wv = (wcol_ref[i] * (IDX_HEADS ** -0.5)) * (IDX_DIM ** -0.5)
            sc = (jnp.maximum(dd, 0.0) * wv).reshape(t_len, IDX_HEADS, s_pad).sum(axis=1)
            score_ref[it, u * t_len:(u + 1) * t_len, :] = sc
        return c

    lax.fori_loop(0, gs // per_tile, score_body, 0)

    rows = gs * t_len
    s_idx = lax.broadcasted_iota(I32, (rows, s_pad), 1)
    t_idx = lax.broadcasted_iota(I32, (rows, s_pad), 0) % t_len
    valid = s_idx <= past + t_idx
    keys = jnp.where(valid, _sortable_key(score_ref[...].reshape(rows, s_pad)), INT_MIN)

    def bit_body(p, thr):
        bit = jnp.left_shift(jnp.int32(1), 31 - p)
        cand = thr ^ bit
        cnt = jnp.sum((keys >= cand).astype(I32), axis=1, keepdims=True)
        return jnp.where(cnt >= k_top, cand, thr)

    thr = lax.fori_loop(0, 32, bit_body, jnp.full((rows, 1), INT_MIN, I32))
    gt = keys > thr
    eq = keys == thr
    need = (k_top - jnp.sum(gt.astype(I32), axis=1, keepdims=True)).astype(F32)
    utri = (lax.broadcasted_iota(I32, (LANES, LANES), 0) <= lax.broadcasted_iota(I32, (LANES, LANES), 1)
            ).astype(BF16)
    eqb = eq.astype(BF16)
    carry = jnp.zeros((rows, 1), F32)
    for c in range(s_pad // LANES):
        cols = slice(c * LANES, (c + 1) * LANES)
        prefix = _dot(eqb[:, cols], utri) + carry
        take = gt[:, cols] | (eq[:, cols] & (prefix <= need))
        take = take & valid[:, cols]
        m = jnp.where(take, 0.0, NEG).reshape(gs // per_tile, tile, LANES)
        for u in range(per_tile):
            mask_ref[:, u, :, cols] = m[:, u * t_len:(u + 1) * t_len, :]
        carry = prefix[:, LANES - 1:LANES]


def _select_s(page_table, iq, wcol, iknew, pool, layer, gs, k_top):
    nb, rows_q, _ = iq.shape
    t_len = rows_q // IDX_HEADS
    n_pages = page_table.shape[1]
    page = pool.shape[2]
    s_pad = n_pages * page + LANES
    sublanes = 8
    assert sublanes % t_len == 0
    per_tile = sublanes // t_len
    assert nb % gs == 0 and gs % per_tile == 0
    kern = functools.partial(_select_s_kernel, k_top=k_top, layer_off=layer, t_len=t_len)
    grid_spec = pltpu.PrefetchScalarGridSpec(
        num_scalar_prefetch=1, grid=(nb // gs,),
        in_specs=[pl.BlockSpec((gs, rows_q, IDX_DIM), lambda g, pt: (g, 0, 0)),
                  pl.BlockSpec((gs, rows_q, 1), lambda g, pt: (g, 0, 0)),
                  pl.BlockSpec((gs, IDX_DIM, t_len), lambda g, pt: (g, 0, 0)),
                  pl.BlockSpec(memory_space=pl.ANY)],
        out_specs=pl.BlockSpec((gs // per_tile, per_tile, t_len, s_pad), lambda g, pt: (g, 0, 0, 0)),
        scratch_shapes=[pltpu.VMEM((gs, IDX_DIM, s_pad), F32),
                        pltpu.VMEM((gs // per_tile, sublanes, s_pad), F32),
                        pltpu.SemaphoreType.DMA(())])
    return pl.pallas_call(
        kern, grid_spec=grid_spec,
        out_shape=jax.ShapeDtypeStruct((nb // per_tile, per_tile, t_len, s_pad), F32),
        compiler_params=_params(1), name="select_sample")(page_table, iq, wcol, iknew, pool)


def _attn_s_kernel(pt_ref, q_ref, knew_ref, vnew_ref, mask_ref, dtab_ref, *rest, n_pages, t_len):
    k_pages = rest[:n_pages]
    v_pages = rest[n_pages:2 * n_pages]
    o_ref, kpad_ref, vpad_ref = rest[2 * n_pages:]
    rows = t_len * N_HEADS
    page = k_pages[0].shape[2]

    def flat_bf16(page_ref):
        return page_ref[...].reshape(ATT_W, page).astype(BF16)

    def per_head_rows(x):
        return jnp.concatenate(
            [jnp.broadcast_to(x[t:t + 1, :], (N_HEADS, x.shape[1])) for t in range(t_len)], axis=0)

    col_head = lax.broadcasted_iota(I32, (rows, ATT_W), 1) // HEAD_DIM
    row_head = lax.broadcasted_iota(I32, (rows, ATT_W), 0) % N_HEADS
    own = col_head == row_head
    qbd = jnp.where(own, per_head_rows(q_ref[...].astype(F32)), 0.0).astype(BF16)
    kpad_ref[...] = jnp.zeros(kpad_ref.shape, F32)
    vpad_ref[...] = jnp.zeros(vpad_ref.shape, F32)
    kpad_ref[0:t_len, :] = knew_ref[...]
    vpad_ref[0:t_len, :] = vnew_ref[...]
    knew = kpad_ref[...].astype(BF16)
    vnew = vpad_ref[...].astype(BF16)
    logits = [_dot(qbd, flat_bf16(kp)) for kp in k_pages] + [_dot_nt(qbd, knew)]
    x = jnp.concatenate(logits, axis=1)
    x = x + dtab_ref[...] + per_head_rows(mask_ref[...])
    m = x.max(axis=1, keepdims=True)
    p = jnp.exp(x - m)
    l = p.sum(axis=1, keepdims=True)
    pb = p.astype(BF16)
    out = _dot(pb[:, n_pages * page:], vnew)
    for j, vp in enumerate(v_pages):
        out = out + _dot_nt(pb[:, j * page:(j + 1) * page], flat_bf16(vp))
    out = out / l
    out = jnp.where(own, out, 0.0).reshape(t_len, N_HEADS, ATT_W).sum(axis=1)
    o_ref[...] = out.astype(BF16)


def _attn_s(page_table, q, knew, vnew, mask, dtab, k_pool, v_pool, layer_off):
    nb, t_len, _ = q.shape
    n_pages = page_table.shape[1]
    page = k_pool.shape[3]
    s_pad = n_pages * page + LANES

    def page_spec(j):
        return pl.BlockSpec((None, N_HEADS, HEAD_DIM, page),
                            lambda b, pt, j=j: (layer_off + pt[b, j], 0, 0, 0))

    seq = lambda w: pl.BlockSpec((None, t_len, w), lambda b, pt: (b, 0, 0))
    per_tile = mask.shape[1]
    mask_spec = pl.BlockSpec((None, None, t_len, s_pad),
                             lambda b, pt: (b // per_tile, b % per_tile, 0, 0))
    grid_spec = pltpu.PrefetchScalarGridSpec(
        num_scalar_prefetch=1, grid=(nb,),
        in_specs=[seq(ATT_W), seq(ATT_W), seq(ATT_W), mask_spec,
                  pl.BlockSpec(dtab.shape, lambda b, pt: (0, 0))]
                 + [page_spec(j) for j in range(n_pages)] * 2,
        out_specs=seq(ATT_W),
        scratch_shapes=[pltpu.VMEM((page, ATT_W), F32), pltpu.VMEM((page, ATT_W), F32)])
    kern = functools.partial(_attn_s_kernel, n_pages=n_pages, t_len=t_len)
    return pl.pallas_call(
        kern, grid_spec=grid_spec, out_shape=jax.ShapeDtypeStruct((nb, t_len, ATT_W), BF16),
        compiler_params=_params(1), name="attn_sample")(
            page_table, q, knew, vnew, mask, dtab, *([k_pool] * n_pages), *([v_pool] * n_pages))


def _post_kernel(x_ref, ya_ref, o_ref, ga_ref, gb_ref, p_ref, wr_ref, wa_ref, wo_ref,
                 nf_ref, wg_ref, wu_ref, wd_ref, np_ref, wpg_ref, wpp_ref, nfin_ref,
                 out_ref, *, final):
    a = _dot(ya_ref[...], wr_ref[...])
    b = _dot(o_ref[...], wa_ref[...])
    mix = _sigmoid(ga_ref[...]) * a + _sigmoid(gb_ref[...]) * b
    x = x_ref[...] + _dot(mix.astype(BF16), wo_ref[...])
    h2 = _rms(x, nf_ref[...]).astype(BF16)
    g = _dot(h2, wg_ref[...])
    u = _dot(h2, wu_ref[...])
    act = (g * _sigmoid(g)) * u
    x = x + _dot(act.astype(BF16), wd_ref[...])
    h3 = _rms(x, np_ref[...]).astype(BF16)
    x = x + _sigmoid(_dot(h3, wpg_ref[...])) * _dot(p_ref[...].astype(BF16), wpp_ref[...])
    if final:
        x = _rms(x, nfin_ref[...])
    out_ref[...] = x


def _layer_spec(w, l):
    tail = (0,) * (w.ndim - 1)
    return pl.BlockSpec((None,) + w.shape[1:], lambda *_: (l,) + tail,
                        pipeline_mode=pl.Buffered(1))


def _post(x, ya, o, ga, gb, p, weights, tm, final, l, p_row0=0):
    n, d = x.shape
    row = lambda w: pl.BlockSpec((tm, w), lambda i: (i, 0))
    p_blk0 = p_row0 // tm
    return pl.pallas_call(
        functools.partial(_post_kernel, final=final), grid=(n // tm,),
        in_specs=[row(d), row(d), row(ATT_W), row(d), row(d),
                  pl.BlockSpec((tm, p.shape[1]), lambda i: (i + p_blk0, 0))]
                 + [_layer_spec(c, l) if c.ndim == 3 else _const_spec(c.shape) for c in weights],
        out_specs=row(d), out_shape=jax.ShapeDtypeStruct((n, d), F32),
        compiler_params=_params(1), name="post")(x, ya, o, ga, gb, p, *weights)


def _rel_bucket(n, n_buckets):
    max_exact = n_buckets // 2
    nf = jnp.maximum(n, 1).astype(F32)
    large = max_exact + (jnp.log(nf / max_exact) / math.log(MAX_DISTANCE / max_exact)
                         * (n_buckets - max_exact)).astype(I32)
    large = jnp.minimum(large, n_buckets - 1)
    return jnp.where(n < max_exact, n, large)


def _bias_delta(rel_bias, n):
    nbk = rel_bias.shape[0]
    b = rel_bias.astype(F32)
    onehot = _rel_bucket(jnp.maximum(n, 0), nbk)[..., None] == jnp.arange(nbk, dtype=I32)
    picked = jnp.where(onehot[..., None], b, 0.0).sum(axis=-2)
    return picked - b[nbk - 1]


def _block_diag_groups(w, group):
    nbk, c, _ = w.shape
    per = group // c
    wg = w.reshape(nbk // per, per, c, c)
    eye = jnp.eye(per, dtype=w.dtype)
    return jnp.einsum('gpcd,pq->gpcqd', wg, eye).reshape(nbk // per, group, group)


def kernel(x_prompt, x_sample, p_prompt, p_sample, cache_k, cache_v, cache_idx_k, state_rglru_h,
           state_conv, page_table, rel_bias, norm_mix, w_in, conv_w, conv_b, w_rg_a, b_rg_a,
           w_rg_x, b_rg_x, lru_lambda, w_rnn_out, w_att_out, w_o, norm_ffn, w_ffn_gate, w_ffn_up,
           w_ffn_down, norm_ple, w_ple_gate, w_ple_proj, norm_final):
    nb, t, d = x_prompt.shape
    sb, st, _ = x_sample.shape
    depth = w_in.shape[0]
    n_pool, page = cache_k.shape[1], cache_k.shape[2]
    n_pages = page_table.shape[1]
    past = n_pages * page
    cw = conv_w.shape[1]
    s_pad = past + LANES
    k_top_s = min(TOPK_MAX, (past + st) // 4)
    row2 = lambda v: v.reshape(1, -1).astype(F32)

    idx3 = jnp.arange(3, dtype=I32)[:, None, None]
    s_rel = jnp.arange(TK, dtype=I32)[None, :, None]
    t_rel = jnp.arange(TQ, dtype=I32)[None, None, :]
    dtab_p = _bias_delta(rel_bias, idx3 * TQ + t_rel - s_rel).transpose(0, 3, 1, 2)
    s_all = jnp.arange(s_pad, dtype=I32)[None, :]
    q_pos = past + jnp.arange(st, dtype=I32)[:, None]
    dtab_s = _bias_delta(rel_bias, q_pos - s_all)
    dtab_s = dtab_s.transpose(0, 2, 1).reshape(st * N_HEADS, s_pad)

    k_pool = cache_k.transpose(0, 1, 3, 4, 2).reshape(depth * n_pool, N_HEADS, HEAD_DIM, page)
    v_pool = cache_v.transpose(0, 1, 3, 4, 2).reshape(depth * n_pool, N_HEADS, HEAD_DIM, page)
    ik_pool = cache_idx_k.transpose(0, 1, 3, 2).reshape(depth * n_pool, IDX_DIM, page)

    w_r, w_a, w_oo = w_rnn_out.astype(BF16), w_att_out.astype(BF16), w_o.astype(BF16)
    w_fg, w_fu, w_fd = w_ffn_gate.astype(BF16), w_ffn_up.astype(BF16), w_ffn_down.astype(BF16)
    w_pg, w_pp = w_ple_gate.astype(BF16), w_ple_proj.astype(BF16)
    p_rows = p_prompt.reshape(depth * nb * t, -1)

    xp = x_prompt.reshape(nb * t, d)
    xs = x_sample.transpose(1, 0, 2).reshape(st * sb, d)
    outs_p, outs_s = [], []
    caches = (jnp.zeros((depth, nb, ATT_W, t), F32), jnp.zeros((depth, nb, ATT_W, t), F32),
              jnp.zeros((depth, nb, IDX_DIM, t), F32))
    for l in range(depth):
        splits = [d, 2 * d, 2 * d + ATT_W, 2 * d + 2 * ATT_W, 2 * d + 3 * ATT_W,
                  2 * d + 3 * ATT_W + IDX_HEADS * IDX_DIM,
                  2 * d + 3 * ATT_W + IDX_HEADS * IDX_DIM + IDX_DIM,
                  2 * d + 3 * ATT_W + IDX_HEADS * IDX_DIM + IDX_DIM + IDX_HEADS,
                  3 * d + 3 * ATT_W + IDX_HEADS * IDX_DIM + IDX_DIM + IDX_HEADS]
        w_xr, w_gr, w_q, w_k, w_v, w_iq, w_ik, w_iw, w_ga, w_gb = jnp.split(w_in[l], splits, axis=1)
        w_ikw = jnp.concatenate(
            [w_ik, w_iw, jnp.zeros((d, LANES - IDX_DIM - IDX_HEADS), F32)], axis=1)
        wm_p = jnp.concatenate([w_xr, w_gr, w_ga, w_gb, w_k, w_ikw], axis=1).astype(BF16)
        wt_p = jnp.concatenate([w_q, w_iq, w_k, w_v, w_ik, w_iw], axis=1).T.astype(BF16)
        gw = 2 * LANES
        wa_bd = _block_diag_groups(w_rg_a[l], gw).astype(BF16)
        wx_bd = _block_diag_groups(w_rg_x[l], gw).astype(BF16)
        lru_consts = (conv_w[l], row2(conv_b[l]), wa_bd, wx_bd, row2(b_rg_a[l]), row2(b_rg_x[l]),
                      row2(lru_lambda[l]))
        post_w = (w_r, w_a, w_oo, row2(norm_ffn[l]), w_fg, w_fu, w_fd, row2(norm_ple[l]), w_pg, w_pp,
                  row2(norm_final))
        final = l == depth - 1

        (xr, gr, ga, gb, kb, ikwb, qT, iqT, vTb, iwT, *caches) = _inproj_p(
            xp, row2(norm_mix[l]), wm_p, wt_p, caches, tm=256, nb=nb, l=l)
        ya, h_last, new_buf = _rglru_p(xr.reshape(nb, t, d), gr.reshape(nb, t, d), *lru_consts, tt=128)
        o = _attn_p(qT, iqT, iwT, kb, vTb, ikwb, dtab_p, nb, t)
        xp = _post(xp, ya.reshape(nb * t, d), o, ga, gb, p_rows, post_w, tm=256, final=final, l=l,
                   p_row0=l * nb * t)
        outs_p.append((h_last, new_buf))

        (xr, gr, ga, gb, k, v, ikw, q, iq) = _inproj_s(xs, row2(norm_mix[l]), wm_p, wt_p, tm=st * sb)
        tmaj = lambda a2: a2.reshape(st, sb, -1)
        ya, h_last, new_buf = _rglru_s(tmaj(xr), tmaj(gr), state_conv[l].transpose(1, 0, 2),
                                       state_rglru_h[l], *lru_consts)
        bmaj = lambda a2: a2.reshape(st, sb, -1).transpose(1, 0, 2)
        k_b, v_b, ikw_b, q_b, iq_b = bmaj(k), bmaj(v), bmaj(ikw), bmaj(q), bmaj(iq)
        iq_rows = iq_b.reshape(sb, st * IDX_HEADS, IDX_DIM)
        w_col = ikw_b[:, :, IDX_DIM:IDX_DIM + IDX_HEADS].reshape(sb, st * IDX_HEADS, 1)
        mask = _select_s(page_table, iq_rows, w_col, ikw_b[:, :, :IDX_DIM].transpose(0, 2, 1),
                         ik_pool, l * n_pool, gs=min(sb, 32), k_top=k_top_s)
        o = _attn_s(page_table, q_b, k_b, v_b, mask, dtab_s, k_pool, v_pool, l * n_pool)
        o = o.transpose(1, 0, 2).reshape(st * sb, ATT_W)
        xs = _post(xs, ya.reshape(st * sb, d), o, ga, gb,
                   p_sample[l].transpose(1, 0, 2).reshape(st * sb, -1), post_w,
                   tm=min(256, st * sb), final=final, l=l)
        outs_s.append((k_b.reshape(sb, st, N_HEADS, HEAD_DIM), v_b.reshape(sb, st, N_HEADS, HEAD_DIM),
                       ikw_b[:, :, :IDX_DIM], h_last, new_buf.transpose(1, 0, 2)))

    y_prompt = xp.reshape(nb, t, d)
    y_sample = xs.reshape(st, sb, d).transpose(1, 0, 2)
    stack = lambda outs, i: jnp.stack([o_[i] for o_ in outs])
    kT_all, vT_all, ikT_all = caches
    heads_last = lambda aT: aT.reshape(depth, nb, N_HEADS, HEAD_DIM, t).transpose(0, 1, 4, 2, 3)
    return (y_prompt, y_sample,
            heads_last(kT_all), heads_last(vT_all), ikT_all.transpose(0, 1, 3, 2),
            stack(outs_p, 0), stack(outs_p, 1),
            stack(outs_s, 0), stack(outs_s, 1), stack(outs_s, 2), stack(outs_s, 3), stack(outs_s, 4))
```

```python
import functools
import math

import jax
import jax.numpy as jnp
from jax import lax
from jax.experimental import pallas as pl
from jax.experimental.pallas import tpu as pltpu

F32 = jnp.float32
BF16 = jnp.bfloat16
I32 = jnp.int32

N_HEADS = 8
HEAD_DIM = 64
ATT_W = N_HEADS * HEAD_DIM
IDX_HEADS = 8
IDX_DIM = 64
TOPK_MAX = 256
LRU_C = 8.0
MAX_DISTANCE = 128
EPS = 1e-6

LANES = 128
SUBLANES = 8
TQ = 128
TK = 256
NEG = -1e30
LOG2E = math.log2(math.e)
INT_MIN = -2147483648
VMEM_LIMIT = 56 * 1024 * 1024


def _dot(a, b):
    return jnp.dot(a, b, preferred_element_type=F32)


def _dot_nt(a, b):
    return lax.dot_general(a, b, (((1,), (1,)), ((), ())), preferred_element_type=F32)


def _sigmoid(x):
    return 0.5 * jnp.tanh(0.5 * x) + 0.5


def _gelu_tanh(x):
    c = math.sqrt(2.0 / math.pi)
    return 0.5 * x * (1.0 + jnp.tanh(c * (x + 0.044715 * (x * x * x))))


def _rms(x, g):
    y = x * lax.rsqrt(jnp.mean(x * x, axis=-1, keepdims=True) + EPS)
    return y * g


def _params(n_axes=1):
    return pltpu.CompilerParams(dimension_semantics=("arbitrary",) * n_axes,
                                vmem_limit_bytes=VMEM_LIMIT)


def _fori_pairs(lo, hi, body, carry):
    n = hi - lo

    def pair(i, c):
        j = lo + 2 * i
        return body(j + 1, body(j, c))

    carry = lax.fori_loop(0, n >> 1, pair, carry)
    return lax.cond((n & 1) == 1, lambda c: body(hi - 1, c), lambda c: c, carry)


def _const_spec(shape):
    nd = len(shape)
    return pl.BlockSpec(shape, lambda *_: (0,) * nd, pipeline_mode=pl.Buffered(1))


def _whole_out_spec(shape):
    nd = len(shape)
    return pl.BlockSpec(shape, lambda *_: (0,) * nd)


def _inproj_p_kernel(x_ref, g_ref, wm_ref, wt_ref, kT_all_ref, vT_all_ref, ikT_all_ref,
                     xr_ref, gr_ref, ga_ref, gb_ref, kb_ref, ikwb_ref,
                     qT_ref, iqT_ref, vTb_ref, iwT_ref, kT_ref, vT_ref, ikT_ref):
    del kT_all_ref, vT_all_ref, ikT_all_ref
    d = x_ref.shape[1]
    hb = _rms(x_ref[...], g_ref[...]).astype(BF16)
    z = _dot(hb, wm_ref[...])
    xr_ref[...] = z[:, 0:d]
    gr_ref[...] = z[:, d:2 * d]
    ga_ref[...] = z[:, 2 * d:3 * d]
    gb_ref[...] = z[:, 3 * d:4 * d]
    o = 4 * d
    kb_ref[...] = z[:, o:o + ATT_W].astype(BF16)
    ikwb_ref[...] = z[:, o + ATT_W:o + ATT_W + LANES].astype(BF16)
    zt = _dot_nt(wt_ref[...], hb)
    qT_ref[...] = (zt[0:ATT_W] * (HEAD_DIM ** -0.5)).astype(BF16)
    iqT_ref[...] = zt[ATT_W:2 * ATT_W].astype(BF16)
    kT_ref[...] = zt[2 * ATT_W:3 * ATT_W]
    vt = zt[3 * ATT_W:4 * ATT_W]
    vT_ref[...] = vt
    for c in range(vTb_ref.shape[0]):
        vTb_ref[c] = vt[:, c * TK:(c + 1) * TK].astype(BF16)
    ikT_ref[...] = zt[4 * ATT_W:4 * ATT_W + IDX_DIM]
    iwT_ref[...] = zt[4 * ATT_W + IDX_DIM:4 * ATT_W + IDX_DIM + IDX_HEADS]


def _inproj_p(x, g, wm, wt, caches, tm, nb, l):
    n, d = x.shape
    t = n // nb
    per_b = t // tm
    row = lambda w: pl.BlockSpec((tm, w), lambda i: (i, 0))
    out_shape = (
        jax.ShapeDtypeStruct((n, d), F32), jax.ShapeDtypeStruct((n, d), F32),
        jax.ShapeDtypeStruct((n, d), F32), jax.ShapeDtypeStruct((n, d), F32),
        jax.ShapeDtypeStruct((n, ATT_W), BF16), jax.ShapeDtypeStruct((n, LANES), BF16),
        jax.ShapeDtypeStruct((ATT_W, n), BF16), jax.ShapeDtypeStruct((ATT_W, n), BF16),
        jax.ShapeDtypeStruct((n // TK, ATT_W, TK), BF16),
        jax.ShapeDtypeStruct((IDX_HEADS, n), F32),
    ) + tuple(jax.ShapeDtypeStruct(c.shape, c.dtype) for c in caches)
    colT = lambda r: pl.BlockSpec((r, tm), lambda i: (0, i))
    layer_seq = lambda r: pl.BlockSpec((None, None, r, tm), lambda i: (l, i // per_b, 0, i % per_b))
    out_specs = (row(d), row(d), row(d), row(d), row(ATT_W), row(LANES), colT(ATT_W), colT(ATT_W),
                 pl.BlockSpec((tm // TK, ATT_W, TK), lambda i: (i, 0, 0)), colT(IDX_HEADS),
                 layer_seq(ATT_W), layer_seq(ATT_W), layer_seq(IDX_DIM))
    n_plain_in, n_plain_out = 4, 10
    return pl.pallas_call(
        _inproj_p_kernel, grid=(n // tm,),
        in_specs=[row(d), _const_spec(g.shape), _const_spec(wm.shape), _const_spec(wt.shape)]
                 + [pl.BlockSpec(memory_space=pl.ANY)] * len(caches),
        out_specs=out_specs, out_shape=out_shape,
        input_output_aliases={n_plain_in + i: n_plain_out + i for i in range(len(caches))},
        compiler_params=_params(1), name="inproj_prompt")(x, g, wm, wt, *caches)


def _inproj_s_kernel(x_ref, g_ref, wm_ref, wt_ref,
                     xr_ref, gr_ref, ga_ref, gb_ref, k_ref, v_ref, ikw_ref, q_ref, iq_ref):
    d = x_ref.shape[1]
    hb = _rms(x_ref[...], g_ref[...]).astype(BF16)
    z = _dot(hb, wm_ref[...])
    xr_ref[...] = z[:, 0:d]
    gr_ref[...] = z[:, d:2 * d]
    ga_ref[...] = z[:, 2 * d:3 * d]
    gb_ref[...] = z[:, 3 * d:4 * d]
    o = 4 * d
    k_ref[...] = z[:, o:o + ATT_W]
    ikw_ref[...] = z[:, o + ATT_W:o + ATT_W + LANES]
    qiq = _dot_nt(hb, wt_ref[0:2 * ATT_W, :])
    q_ref[...] = (qiq[:, 0:ATT_W] * (HEAD_DIM ** -0.5)).astype(BF16)
    iq_ref[...] = qiq[:, ATT_W:2 * ATT_W].astype(BF16)
    v_ref[...] = _dot_nt(hb, wt_ref[3 * ATT_W:4 * ATT_W, :])


def _inproj_s(x, g, wm, wt, tm):
    n, d = x.shape
    row = lambda w: pl.BlockSpec((tm, w), lambda i: (i, 0))
    out_shape = (
        jax.ShapeDtypeStruct((n, d), F32), jax.ShapeDtypeStruct((n, d), F32),
        jax.ShapeDtypeStruct((n, d), F32), jax.ShapeDtypeStruct((n, d), F32),
        jax.ShapeDtypeStruct((n, ATT_W), F32), jax.ShapeDtypeStruct((n, ATT_W), F32),
        jax.ShapeDtypeStruct((n, LANES), F32),
        jax.ShapeDtypeStruct((n, ATT_W), BF16), jax.ShapeDtypeStruct((n, ATT_W), BF16),
    )
    out_specs = (row(d), row(d), row(d), row(d), row(ATT_W), row(ATT_W), row(LANES),
                 row(ATT_W), row(ATT_W))
    return pl.pallas_call(
        _inproj_s_kernel, grid=(n // tm,),
        in_specs=[row(d), _const_spec(g.shape), _const_spec(wm.shape), _const_spec(wt.shape)],
        out_specs=out_specs, out_shape=out_shape, compiler_params=_params(1),
        name="inproj_sample")(x, g, wm, wt)


def _lru_gates(xc, wa_ref, wx_ref, ba, bx, lam):
    xcb = xc.astype(BF16)
    gw = wa_ref.shape[1]
    r_parts, i_parts = [], []
    for g in range(wa_ref.shape[0]):
        xs = xcb[:, g * gw:(g + 1) * gw]
        r_parts.append(_dot(xs, wa_ref[g]))
        i_parts.append(_dot(xs, wx_ref[g]))
    r = _sigmoid(jnp.concatenate(r_parts, axis=1) + ba)
    i = _sigmoid(jnp.concatenate(i_parts, axis=1) + bx)
    log_sig_lam = jnp.minimum(lam, 0.0) - jnp.log(1.0 + jnp.exp(-jnp.abs(lam)))
    log_a = LRU_C * r * log_sig_lam
    a = jnp.exp(log_a)
    u = jnp.sqrt(1.0 - a * a) * (i * xc)
    return a, u


def _rglru_p_kernel(xr_ref, gr_ref, cw_ref, cb_ref, wa_ref, wx_ref, ba_ref, bx_ref, lam_ref,
                    ya_ref, hlast_ref, buf_ref,
                    xx_ref, a_ref, u_ref, h_ref):
    step = pl.program_id(0)
    nb, tt, d = xr_ref.shape
    cw = cw_ref.shape[0]

    @pl.when(step == 0)
    def _():
        xx_ref[:, 0:8, :] = jnp.zeros((nb, 8, d), F32)
        h_ref[...] = jnp.zeros_like(h_ref)

    for b in range(nb):
        x = xr_ref[b]
        xx_ref[b, 8:8 + tt, :] = x
        xc = cb_ref[...] + x * cw_ref[cw - 1:cw, :]
        for j in range(cw - 1):
            sh = cw - 1 - j
            xc = xc + xx_ref[b, 8 - sh:8 - sh + tt, :] * cw_ref[j:j + 1, :]
        xx_ref[b, 0:8, :] = x[tt - 8:tt, :]
        a, u = _lru_gates(xc, wa_ref, wx_ref, ba_ref[...], bx_ref[...], lam_ref[...])
        a_ref[b] = a
        u_ref[b] = u

    def scan_body(t, hs):
        new = []
        for b in range(nb):
            h = a_ref[b, pl.ds(t, 1), :] * hs[b] + u_ref[b, pl.ds(t, 1), :]
            u_ref[b, pl.ds(t, 1), :] = h
            new.append(h)
        return tuple(new)

    hs = lax.fori_loop(0, tt, scan_body, tuple(h_ref[b:b + 1, :] for b in range(nb)), unroll=8)
    for b in range(nb):
        h_ref[b:b + 1, :] = hs[b]
        ya_ref[b] = (u_ref[b] * _gelu_tanh(gr_ref[b])).astype(BF16)
        buf_ref[b] = xr_ref[b, tt - (cw - 1):tt, :]
    hlast_ref[...] = h_ref[...]


def _rglru_p(xr, gr, cw, cb, wa, wx, ba, bx, lam, tt):
    nb, t, d = xr.shape
    blk = pl.BlockSpec((nb, tt, d), lambda i: (0, i, 0))
    consts = [cw, cb, wa, wx, ba, bx, lam]
    return pl.pallas_call(
        _rglru_p_kernel, grid=(t // tt,),
        in_specs=[blk, blk] + [_const_spec(c.shape) for c in consts],
        out_specs=(blk, _whole_out_spec((nb, d)), _whole_out_spec((nb, cw.shape[0] - 1, d))),
        out_shape=(jax.ShapeDtypeStruct((nb, t, d), BF16), jax.ShapeDtypeStruct((nb, d), F32),
                   jax.ShapeDtypeStruct((nb, cw.shape[0] - 1, d), F32)),
        scratch_shapes=[pltpu.VMEM((nb, tt + 8, d), F32), pltpu.VMEM((nb, tt, d), F32),
                        pltpu.VMEM((nb, tt, d), F32), pltpu.VMEM((nb, d), F32)],
        compiler_params=_params(1), name="rglru_prompt")(xr, gr, *consts)


def _rglru_s_kernel(xr_ref, gr_ref, st_ref, h0_ref, cw_ref, cb_ref, wa_ref, wx_ref, ba_ref,
                    bx_ref, lam_ref, ya_ref, hlast_ref, buf_ref):
    t_len, nb, d = xr_ref.shape
    cw = cw_ref.shape[0]
    rows = [st_ref[j] for j in range(cw - 1)] + [xr_ref[t] for t in range(t_len)]
    h = h0_ref[...]
    for t in range(t_len):
        xc = cb_ref[...] + rows[t + cw - 1] * cw_ref[cw - 1:cw, :]
        for j in range(cw - 1):
            xc = xc + rows[t + j] * cw_ref[j:j + 1, :]
        a, u = _lru_gates(xc, wa_ref, wx_ref, ba_ref[...], bx_ref[...], lam_ref[...])
        h = a * h + u
        ya_ref[t] = (h * _gelu_tanh(gr_ref[t])).astype(BF16)
    hlast_ref[...] = h
    for j in range(cw - 1):
        buf_ref[j] = rows[t_len + j]


def _rglru_s(xr, gr, st, h0, cw, cb, wa, wx, ba, bx, lam):
    t_len, nb, d = xr.shape
    args = [xr, gr, st, h0, cw, cb, wa, wx, ba, bx, lam]
    return pl.pallas_call(
        _rglru_s_kernel, grid=(1,),
        in_specs=[_const_spec(a.shape) for a in args],
        out_specs=(_whole_out_spec((t_len, nb, d)), _whole_out_spec((nb, d)),
                   _whole_out_spec((cw.shape[0] - 1, nb, d))),
        out_shape=(jax.ShapeDtypeStruct((t_len, nb, d), BF16), jax.ShapeDtypeStruct((nb, d), F32),
                   jax.ShapeDtypeStruct((cw.shape[0] - 1, nb, d), F32)),
        compiler_params=_params(1), name="rglru_sample")(*args)


def _sortable_key(score):
    bits = pltpu.bitcast(score, I32)
    return bits ^ ((bits >> 31) & 0x7FFFFFFF)


GROUP = 16
QT = 2


def _sort_network(n):
    pairs = []
    p = 1
    while p < n:
        k = p
        while k >= 1:
            for j in range(k % p, n - k, 2 * k):
                for i in range(min(k, n - j - k)):
                    if (i + j) // (2 * p) == (i + j + k) // (2 * p):
                        pairs.append((i + j, i + j + k))
            k //= 2
        p *= 2
    return pairs


def _count_in_sorted_group(v, cmp):
    one = lambda m, w: jnp.where(m, w, 0)
    pick = jnp.where
    m1 = cmp(v(7))
    m2 = cmp(pick(m1, v(11), v(3)))
    m3 = cmp(pick(m2, pick(m1, v(13), v(5)), pick(m1, v(9), v(1))))
    if_m3 = pick(m2, pick(m1, v(14), v(6)), pick(m1, v(10), v(2)))
    if_not = pick(m2, pick(m1, v(12), v(4)), pick(m1, v(8), v(0)))
    m4 = cmp(pick(m3, if_m3, if_not))
    m5 = cmp(v(15))
    return one(m1, 8) + one(m2, 4) + one(m3, 2) + one(m4, 1) + one(m5, 1)


def _attn_p_kernel(qT_ref, iqT_ref, iwT_ref, kb_ref, vT_ref, ikw_ref, dtab_ref, o_ref,
                   keys_ref, sorted_ref, rhs_ref, qbd_ref, m_ref, acc_ref, x_ref, *, k_top):
    qp = pl.program_id(1)
    n_chunks = qp + 1
    half = LANES // 2
    zeros_half = jnp.zeros((half, TQ), BF16)
    s_iota = lax.broadcasted_iota(I32, (TK, TQ), 0)
    t_iota = lax.broadcasted_iota(I32, (TK, TQ), 1)
    w_all = (iwT_ref[...] * (IDX_HEADS ** -0.5)) * (IDX_DIM ** -0.5)

    def tile_consts(qi):
        qb = QT * qp + qi
        lanes = slice(qi * TQ, (qi + 1) * TQ)
        n_far = jnp.maximum(qb - 1, 0) >> 1
        return qb * TQ, lanes, n_far

    def score_body(j, carry, band, qi, t0, lanes, w):
        off = pl.multiple_of(j * TK, TK)
        ikc = ikw_ref[pl.ds(off, TK), :]
        score = jnp.zeros((TK, TQ), F32)
        for c in range(N_HEADS // 2):
            dd = _dot(ikc, rhs_ref[qi, c])
            for hh in range(2):
                h = 2 * c + hh
                score = score + jnp.maximum(dd[:, hh * TQ:(hh + 1) * TQ], 0.0) * w[h:h + 1, :]
        key = _sortable_key(score)
        if band:
            key = jnp.where(s_iota + off <= t0 + t_iota, key, INT_MIN)
        keys_ref[pl.ds(off, TK), lanes] = key
        per = TK // GROUP
        v = [key[i * per:(i + 1) * per, :] for i in range(GROUP)]
        for a, b in _sort_network(GROUP):
            v[a], v[b] = jnp.maximum(v[a], v[b]), jnp.minimum(v[a], v[b])
        for i in range(GROUP):
            sorted_ref[j, i, :, lanes] = v[i]
        return carry

    for qi in range(QT):
        t0, lanes, n_far = tile_consts(qi)
        for c in range(N_HEADS // 2):
            for hh in range(2):
                h = 2 * c + hh
                iq_h = iqT_ref[h * IDX_DIM:(h + 1) * IDX_DIM, lanes]
                rhs_ref[qi, c, :, hh * TQ:(hh + 1) * TQ] = jnp.concatenate([iq_h, zeros_half], axis=0)
                q_h = qT_ref[h * HEAD_DIM:(h + 1) * HEAD_DIM, lanes]
                parts = [zeros_half, q_h] if hh else [q_h, zeros_half]
                qbd_ref[qi, c, :, hh * TQ:(hh + 1) * TQ] = jnp.concatenate(parts, axis=0)
        body = functools.partial(score_body, qi=qi, t0=t0, lanes=lanes, w=w_all[:, lanes])
        _fori_pairs(0, n_far, functools.partial(body, band=False), 0)
        _fori_pairs(n_far, n_chunks, functools.partial(body, band=True), 0)

    @pl.when((n_chunks & 1) == 1)
    def _():
        sorted_ref[n_chunks] = jnp.full(sorted_ref.shape[1:], INT_MIN, I32)

    def count(cmp):
        def body(i, acc):
            for u in range(2):
                acc = acc + _count_in_sorted_group(lambda r: sorted_ref[2 * i + u, r], cmp)
            return acc
        acc = lax.fori_loop(0, (n_chunks + 1) >> 1, body, jnp.zeros((TK // GROUP, QT * TQ), I32))
        return acc.sum(axis=0, keepdims=True)

    def bit_body(p, carry):
        thr, n_ge = carry
        bit = jnp.left_shift(jnp.int32(1), 31 - p)
        cand = thr ^ bit
        cnt = count(lambda kc: kc >= cand)
        ok = cnt >= k_top
        return jnp.where(ok, cand, thr), jnp.where(ok, cnt, n_ge)

    thr_all, n_ge = lax.fori_loop(0, 32, bit_body, (jnp.full((1, QT * TQ), INT_MIN, I32),
                                                    jnp.zeros((1, QT * TQ), I32)))
    any_ties = jnp.max(jnp.where((thr_all != INT_MIN) & (n_ge != k_top), 1, 0))

    fold = lambda a, op: op(a.reshape(TK // SUBLANES, SUBLANES, TQ), axis=0)

    def logits_body(j, carry, band, ties, qi, t0, lanes, thr, need=None, ltri=None):
        taken, mx = carry
        off = pl.multiple_of(j * TK, TK)
        keyc = keys_ref[pl.ds(off, TK), lanes]
        if ties:
            eq = keyc == thr
            prefix = _dot(ltri, eq.astype(BF16))
            sel = (keyc > thr) | (eq & (prefix + taken <= need))
            taken = taken + prefix[TK - 1:TK, :]
        else:
            sel = keyc >= thr
        if band:
            sel = sel & (s_iota + off <= t0 + t_iota)
            tab = (t0 - off) // TQ
        kc = kb_ref[pl.ds(off, TK), :]
        new_mx = []
        for c in range(N_HEADS // 2):
            lg = _dot(kc[:, c * LANES:(c + 1) * LANES], qbd_ref[qi, c])
            for hh in range(2):
                h = 2 * c + hh
                x = lg[:, hh * TQ:(hh + 1) * TQ]
                if band:
                    x = x + dtab_ref[tab, h]
                x = jnp.where(sel, x * LOG2E, NEG)
                x_ref[h, pl.ds(off, TK), :] = x
                new_mx.append(jnp.maximum(mx[h * SUBLANES:(h + 1) * SUBLANES, :], fold(x, jnp.max)))
        return taken, jnp.concatenate(new_mx, axis=0)

    def all_logits(ties, qi, t0, lanes, n_far):
        thr = thr_all[:, lanes]
        extra = {}
        if ties:
            n_gt = count(lambda kc: kc > thr_all)
            extra["need"] = (k_top - n_gt[:, lanes]).astype(F32)
            extra["ltri"] = (lax.broadcasted_iota(I32, (TK, TK), 1)
                             <= lax.broadcasted_iota(I32, (TK, TK), 0)).astype(BF16)
        body = functools.partial(logits_body, ties=ties, qi=qi, t0=t0, lanes=lanes, thr=thr, **extra)
        carry = (jnp.zeros((1, TQ), F32), jnp.full((N_HEADS * SUBLANES, TQ), NEG, F32))
        carry = _fori_pairs(0, n_far, functools.partial(body, band=False), carry)
        return _fori_pairs(n_far, n_chunks, functools.partial(body, band=True), carry)[1]

    def pv_body(j, lsum):
        off = pl.multiple_of(j * TK, TK)
        new_lsum = []
        for h in range(N_HEADS):
            p = jnp.exp2(x_ref[h, pl.ds(off, TK), :] - m_ref[h:h + 1, :])
            new_lsum.append(lsum[h * SUBLANES:(h + 1) * SUBLANES, :] + fold(p, jnp.sum))
            rows = slice(h * HEAD_DIM, (h + 1) * HEAD_DIM)
            acc_ref[rows, :] += _dot(vT_ref[j, rows, :], p.astype(BF16))
        return jnp.concatenate(new_lsum, axis=0)

    for qi in range(QT):
        t0, lanes, n_far = tile_consts(qi)
        mx = lax.cond(any_ties > 0,
                      functools.partial(all_logits, True, qi, t0, lanes, n_far),
                      functools.partial(all_logits, False, qi, t0, lanes, n_far))
        m_ref[...] = jnp.concatenate(
            [mx[h * SUBLANES:(h + 1) * SUBLANES, :].max(axis=0, keepdims=True)
             for h in range(N_HEADS)], axis=0)
        acc_ref[...] = jnp.zeros(acc_ref.shape, F32)
        lsum = _fori_pairs(0, n_chunks, pv_body, jnp.zeros((N_HEADS * SUBLANES, TQ), F32))
        for h in range(N_HEADS):
            rows = slice(h * HEAD_DIM, (h + 1) * HEAD_DIM)
            inv = 1.0 / lsum[h * SUBLANES:(h + 1) * SUBLANES, :].sum(axis=0, keepdims=True)
            acc_ref[rows, :] = acc_ref[rows, :] * inv
        o_ref[qi * TQ:(qi + 1) * TQ, :] = acc_ref[...].T.astype(BF16)


def _attn_p(qT, iqT, iwT, kb, vT, ikwb, dtab, nb, t):
    n = nb * t
    nq = t // (QT * TQ)
    k_top = min(TOPK_MAX, t // 4)
    colT = lambda r: pl.BlockSpec((r, QT * TQ), lambda b, q: (0, b * nq + q))
    return pl.pallas_call(
        functools.partial(_attn_p_kernel, k_top=k_top), grid=(nb, nq),
        in_specs=[colT(ATT_W), colT(ATT_W), colT(IDX_HEADS),
                  pl.BlockSpec((t, ATT_W), lambda b, q: (b, 0)),
                  pl.BlockSpec((t // TK, ATT_W, TK), lambda b, q: (b, 0, 0)),
                  pl.BlockSpec((t, LANES), lambda b, q: (b, 0)),
                  _const_spec(dtab.shape)],
        out_specs=pl.BlockSpec((QT * TQ, ATT_W), lambda b, q: (b * nq + q, 0)),
        out_shape=jax.ShapeDtypeStruct((n, ATT_W), BF16),
        scratch_shapes=[pltpu.VMEM((t, QT * TQ), I32),
                        pltpu.VMEM((t // TK + 1, GROUP, TK // GROUP, QT * TQ), I32),
                        pltpu.VMEM((QT, N_HEADS // 2, LANES, 2 * TQ), BF16),
                        pltpu.VMEM((QT, N_HEADS // 2, LANES, 2 * TQ), BF16),
                        pltpu.VMEM((N_HEADS, TQ), F32), pltpu.VMEM((ATT_W, TQ), F32),
                        pltpu.VMEM((N_HEADS, t, TQ), F32)],
        compiler_params=_params(2), name="attn_prompt")(qT, iqT, iwT, kb, vT, ikwb, dtab)


def _select_s_kernel(pt_ref, iq_ref, wcol_ref, iknew_ref, pool_ref, mask_ref,
                     ikbuf_ref, score_ref, sem, *, k_top, layer_off, t_len):
    g = pl.program_id(0)
    gs, _, s_pad = ikbuf_ref.shape
    n_pages = pt_ref.shape[1]
    page = pool_ref.shape[2]
    past = n_pages * page

    def page_copy(i, j):
        return pltpu.make_async_copy(pool_ref.at[layer_off + pt_ref[g * gs + i, j]],
                                     ikbuf_ref.at[i, :, j * page:(j + 1) * page], sem)

    def start_body(i, c):
        for j in range(n_pages):
            page_copy(i, j).start()
        return c

    lax.fori_loop(0, gs, start_body, 0)
    ikbuf_ref[:, :, past:s_pad] = jnp.zeros((gs, IDX_DIM, s_pad - past), F32)
    ikbuf_ref[:, :, past:past + t_len] = iknew_ref[...]

    def wait_body(i, c):
        for j in range(n_pages):
            page_copy(i, j).wait()
        return c

    lax.fori_loop(0, gs, wait_body, 0)

    tile = score_ref.shape[1]
    per_tile = tile // t_len

    def score_body(it, c):
        for u in range(per_tile):
            i = it * per_tile + u
            dd = _dot(iq_ref[i], ikbuf_ref[i].astype(BF16))
            wv = (wcol_ref[i] * (IDX_HEADS ** -0.5)) * (IDX_DIM ** -0.5)
            sc = (jnp.maximum(dd, 0.0) * wv).reshape(t_len, IDX_HEADS, s_pad).sum(axis=1)
            score_ref[it, u * t_len:(u + 1) * t_len, :] = sc
        return c

    lax.fori_loop(0, gs // per_tile, score_body, 0)

    rows = gs * t_len
    s_idx = lax.broadcasted_iota(I32, (rows, s_pad), 1)
    t_idx = lax.broadcasted_iota(I32, (rows, s_pad), 0) % t_len
    valid = s_idx <= past + t_idx
    keys = jnp.where(valid, _sortable_key(score_ref[...].reshape(rows, s_pad)), INT_MIN)

    def bit_body(p, thr):
        bit = jnp.left_shift(jnp.int32(1), 31 - p)
        cand = thr ^ bit
        cnt = jnp.sum((keys >= cand).astype(I32), axis=1, keepdims=True)
        return jnp.where(cnt >= k_top, cand, thr)

    thr = lax.fori_loop(0, 32, bit_body, jnp.full((rows, 1), INT_MIN, I32))
    gt = keys > thr
    eq = keys == thr
    need = (k_top - jnp.sum(gt.astype(I32), axis=1, keepdims=True)).astype(F32)
    utri = (lax.broadcasted_iota(I32, (LANES, LANES), 0) <= lax.broadcasted_iota(I32, (LANES, LANES), 1)
            ).astype(BF16)
    eqb = eq.astype(BF16)
    carry = jnp.zeros((rows, 1), F32)
    for c in range(s_pad // LANES):
        cols = slice(c * LANES, (c + 1) * LANES)
        prefix = _dot(eqb[:, cols], utri) + carry
        take = gt[:, cols] | (eq[:, cols] & (prefix <= need))
        take = take & valid[:, cols]
        m = jnp.where(take, 0.0, NEG).reshape(gs // per_tile, tile, LANES)
        for u in range(per_tile):
            mask_ref[:, u, :, cols] = m[:, u * t_len:(u + 1) * t_len, :]
        carry = prefix[:, LANES - 1:LANES]


def _select_s(page_table, iq, wcol, iknew, pool, layer, gs, k_top):
    nb, rows_q, _ = iq.shape
    t_len = rows_q // IDX_HEADS
    n_pages = page_table.shape[1]
    page = pool.shape[2]
    s_pad = n_pages * page + LANES
    assert SUBLANES % t_len == 0
    per_tile = SUBLANES // t_len
    assert nb % gs == 0 and gs % per_tile == 0
    kern = functools.partial(_select_s_kernel, k_top=k_top, layer_off=layer, t_len=t_len)
    grid_spec = pltpu.PrefetchScalarGridSpec(
        num_scalar_prefetch=1, grid=(nb // gs,),
        in_specs=[pl.BlockSpec((gs, rows_q, IDX_DIM), lambda g, pt: (g, 0, 0)),
                  pl.BlockSpec((gs, rows_q, 1), lambda g, pt: (g, 0, 0)),
                  pl.BlockSpec((gs, IDX_DIM, t_len), lambda g, pt: (g, 0, 0)),
                  pl.BlockSpec(memory_space=pl.ANY)],
        out_specs=pl.BlockSpec((gs // per_tile, per_tile, t_len, s_pad), lambda g, pt: (g, 0, 0, 0)),
        scratch_shapes=[pltpu.VMEM((gs, IDX_DIM, s_pad), F32),
                        pltpu.VMEM((gs // per_tile, SUBLANES, s_pad), F32),
                        pltpu.SemaphoreType.DMA(())])
    return pl.pallas_call(
        kern, grid_spec=grid_spec,
        out_shape=jax.ShapeDtypeStruct((nb // per_tile, per_tile, t_len, s_pad), F32),
        compiler_params=_params(1), name="select_sample")(page_table, iq, wcol, iknew, pool)


def _attn_s_kernel(pt_ref, q_ref, knew_ref, vnew_ref, mask_ref, dtab_ref, *rest, n_pages, t_len):
    k_pages = rest[:n_pages]
    v_pages = rest[n_pages:2 * n_pages]
    o_ref, kpad_ref, vpad_ref = rest[2 * n_pages:]
    rows = t_len * N_HEADS
    page = k_pages[0].shape[2]

    def flat_bf16(page_ref):
        return page_ref[...].reshape(ATT_W, page).astype(BF16)

    def per_head_rows(x):
        return jnp.concatenate(
            [jnp.broadcast_to(x[t:t + 1, :], (N_HEADS, x.shape[1])) for t in range(t_len)], axis=0)

    col_head = lax.broadcasted_iota(I32, (rows, ATT_W), 1) // HEAD_DIM
    row_head = lax.broadcasted_iota(I32, (rows, ATT_W), 0) % N_HEADS
    own = col_head == row_head
    qbd = jnp.where(own, per_head_rows(q_ref[...].astype(F32)), 0.0).astype(BF16)
    kpad_ref[...] = jnp.zeros(kpad_ref.shape, F32)
    vpad_ref[...] = jnp.zeros(vpad_ref.shape, F32)
    kpad_ref[0:t_len, :] = knew_ref[...]
    vpad_ref[0:t_len, :] = vnew_ref[...]
    knew = kpad_ref[...].astype(BF16)
    vnew = vpad_ref[...].astype(BF16)
    logits = [_dot(qbd, flat_bf16(kp)) for kp in k_pages] + [_dot_nt(qbd, knew)]
    x = jnp.concatenate(logits, axis=1)
    x = x + dtab_ref[...] + per_head_rows(mask_ref[...])
    m = x.max(axis=1, keepdims=True)
    p = jnp.exp(x - m)
    l = p.sum(axis=1, keepdims=True)
    pb = p.astype(BF16)
    out = _dot(pb[:, n_pages * page:], vnew)
    for j, vp in enumerate(v_pages):
        out = out + _dot_nt(pb[:, j * page:(j + 1) * page], flat_bf16(vp))
    out = out / l
    out = jnp.where(own, out, 0.0).reshape(t_len, N_HEADS, ATT_W).sum(axis=1)
    o_ref[...] = out.astype(BF16)


def _attn_s(page_table, q, knew, vnew, mask, dtab, k_pool, v_pool, layer_off):
    nb, t_len, _ = q.shape
    n_pages = page_table.shape[1]
    page = k_pool.shape[3]
    s_pad = n_pages * page + LANES

    def page_spec(j):
        return pl.BlockSpec((None, N_HEADS, HEAD_DIM, page),
                            lambda b, pt, j=j: (layer_off + pt[b, j], 0, 0, 0))

    seq = lambda w: pl.BlockSpec((None, t_len, w), lambda b, pt: (b, 0, 0))
    per_tile = mask.shape[1]
    mask_spec = pl.BlockSpec((None, None, t_len, s_pad),
                             lambda b, pt: (b // per_tile, b % per_tile, 0, 0))
    grid_spec = pltpu.PrefetchScalarGridSpec(
        num_scalar_prefetch=1, grid=(nb,),
        in_specs=[seq(ATT_W), seq(ATT_W), seq(ATT_W), mask_spec,
                  pl.BlockSpec(dtab.shape, lambda b, pt: (0, 0))]
                 + [page_spec(j) for j in range(n_pages)] * 2,
        out_specs=seq(ATT_W),
        scratch_shapes=[pltpu.VMEM((page, ATT_W), F32), pltpu.VMEM((page, ATT_W), F32)])
    kern = functools.partial(_attn_s_kernel, n_pages=n_pages, t_len=t_len)
    return pl.pallas_call(
        kern, grid_spec=grid_spec, out_shape=jax.ShapeDtypeStruct((nb, t_len, ATT_W), BF16),
        compiler_params=_params(1), name="attn_sample")(
            page_table, q, knew, vnew, mask, dtab, *([k_pool] * n_pages), *([v_pool] * n_pages))


def _post_kernel(x_ref, ya_ref, o_ref, ga_ref, gb_ref, p_ref, wr_ref, wa_ref, wo_ref,
                 nf_ref, wg_ref, wu_ref, wd_ref, np_ref, wpg_ref, wpp_ref, nfin_ref,
                 out_ref, *, final):
    a = _dot(ya_ref[...], wr_ref[...])
    b = _dot(o_ref[...], wa_ref[...])
    mix = _sigmoid(ga_ref[...]) * a + _sigmoid(gb_ref[...]) * b
    x = x_ref[...] + _dot(mix.astype(BF16), wo_ref[...])
    h2 = _rms(x, nf_ref[...]).astype(BF16)
    g = _dot(h2, wg_ref[...])
    u = _dot(h2, wu_ref[...])
    act = (g * _sigmoid(g)) * u
    x = x + _dot(act.astype(BF16), wd_ref[...])
    h3 = _rms(x, np_ref[...]).astype(BF16)
    x = x + _sigmoid(_dot(h3, wpg_ref[...])) * _dot(p_ref[...].astype(BF16), wpp_ref[...])
    if final:
        x = _rms(x, nfin_ref[...])
    out_ref[...] = x


def _layer_spec(w, l):
    tail = (0,) * (w.ndim - 1)
    return pl.BlockSpec((None,) + w.shape[1:], lambda *_: (l,) + tail,
                        pipeline_mode=pl.Buffered(1))


def _post(x, ya, o, ga, gb, p, weights, tm, final, l, p_row0=0):
    n, d = x.shape
    row = lambda w: pl.BlockSpec((tm, w), lambda i: (i, 0))
    p_blk0 = p_row0 // tm
    return pl.pallas_call(
        functools.partial(_post_kernel, final=final), grid=(n // tm,),
        in_specs=[row(d), row(d), row(ATT_W), row(d), row(d),
                  pl.BlockSpec((tm, p.shape[1]), lambda i: (i + p_blk0, 0))]
                 + [_layer_spec(c, l) if c.ndim == 3 else _const_spec(c.shape) for c in weights],
        out_specs=row(d), out_shape=jax.ShapeDtypeStruct((n, d), F32),
        compiler_params=_params(1), name="post")(x, ya, o, ga, gb, p, *weights)


def _rel_bucket(n, n_buckets):
    max_exact = n_buckets // 2
    nf = jnp.maximum(n, 1).astype(F32)
    large = max_exact + (jnp.log(nf / max_exact) / math.log(MAX_DISTANCE / max_exact)
                         * (n_buckets - max_exact)).astype(I32)
    large = jnp.minimum(large, n_buckets - 1)
    return jnp.where(n < max_exact, n, large)


def _bias_delta(rel_bias, n):
    nbk = rel_bias.shape[0]
    b = rel_bias.astype(F32)
    onehot = _rel_bucket(jnp.maximum(n, 0), nbk)[..., None] == jnp.arange(nbk, dtype=I32)
    picked = jnp.where(onehot[..., None], b, 0.0).sum(axis=-2)
    return picked - b[nbk - 1]


def _block_diag_groups(w, group):
    nbk, c, _ = w.shape
    per = group // c
    wg = w.reshape(nbk // per, per, c, c)
    eye = jnp.eye(per, dtype=w.dtype)
    return jnp.einsum('gpcd,pq->gpcqd', wg, eye).reshape(nbk // per, group, group)


def kernel(x_prompt, x_sample, p_prompt, p_sample, cache_k, cache_v, cache_idx_k, state_rglru_h,
           state_conv, page_table, rel_bias, norm_mix, w_in, conv_w, conv_b, w_rg_a, b_rg_a,
           w_rg_x, b_rg_x, lru_lambda, w_rnn_out, w_att_out, w_o, norm_ffn, w_ffn_gate, w_ffn_up,
           w_ffn_down, norm_ple, w_ple_gate, w_ple_proj, norm_final):
    nb, t, d = x_prompt.shape
    sb, st, _ = x_sample.shape
    depth = w_in.shape[0]
    n_pool, page = cache_k.shape[1], cache_k.shape[2]
    n_pages = page_table.shape[1]
    past = n_pages * page
    cw = conv_w.shape[1]
    s_pad = past + LANES
    k_top_s = min(TOPK_MAX, (past + st) // 4)
    row2 = lambda v: v.reshape(1, -1).astype(F32)

    idx3 = jnp.arange(3, dtype=I32)[:, None, None]
    s_rel = jnp.arange(TK, dtype=I32)[None, :, None]
    t_rel = jnp.arange(TQ, dtype=I32)[None, None, :]
    dtab_p = _bias_delta(rel_bias, idx3 * TQ + t_rel - s_rel).transpose(0, 3, 1, 2)
    s_all = jnp.arange(s_pad, dtype=I32)[None, :]
    q_pos = past + jnp.arange(st, dtype=I32)[:, None]
    dtab_s = _bias_delta(rel_bias, q_pos - s_all)
    dtab_s = dtab_s.transpose(0, 2, 1).reshape(st * N_HEADS, s_pad)

    k_pool = cache_k.transpose(0, 1, 3, 4, 2).reshape(depth * n_pool, N_HEADS, HEAD_DIM, page)
    v_pool = cache_v.transpose(0, 1, 3, 4, 2).reshape(depth * n_pool, N_HEADS, HEAD_DIM, page)
    ik_pool = cache_idx_k.transpose(0, 1, 3, 2).reshape(depth * n_pool, IDX_DIM, page)

    w_r, w_a, w_oo = w_rnn_out.astype(BF16), w_att_out.astype(BF16), w_o.astype(BF16)
    w_fg, w_fu, w_fd = w_ffn_gate.astype(BF16), w_ffn_up.astype(BF16), w_ffn_down.astype(BF16)
    w_pg, w_pp = w_ple_gate.astype(BF16), w_ple_proj.astype(BF16)
    p_rows = p_prompt.reshape(depth * nb * t, -1)

    xp = x_prompt.reshape(nb * t, d)
    xs = x_sample.transpose(1, 0, 2).reshape(st * sb, d)
    outs_p, outs_s = [], []
    caches = (jnp.zeros((depth, nb, ATT_W, t), F32), jnp.zeros((depth, nb, ATT_W, t), F32),
              jnp.zeros((depth, nb, IDX_DIM, t), F32))
    for l in range(depth):
        splits = [d, 2 * d, 2 * d + ATT_W, 2 * d + 2 * ATT_W, 2 * d + 3 * ATT_W,
                  2 * d + 3 * ATT_W + IDX_HEADS * IDX_DIM,
                  2 * d + 3 * ATT_W + IDX_HEADS * IDX_DIM + IDX_DIM,
                  2 * d + 3 * ATT_W + IDX_HEADS * IDX_DIM + IDX_DIM + IDX_HEADS,
                  3 * d + 3 * ATT_W + IDX_HEADS * IDX_DIM + IDX_DIM + IDX_HEADS]
        w_xr, w_gr, w_q, w_k, w_v, w_iq, w_ik, w_iw, w_ga, w_gb = jnp.split(w_in[l], splits, axis=1)
        w_ikw = jnp.concatenate(
            [w_ik, w_iw, jnp.zeros((d, LANES - IDX_DIM - IDX_HEADS), F32)], axis=1)
        wm_p = jnp.concatenate([w_xr, w_gr, w_ga, w_gb, w_k, w_ikw], axis=1).astype(BF16)
        wt_p = jnp.concatenate([w_q, w_iq, w_k, w_v, w_ik, w_iw], axis=1).T.astype(BF16)
        gw = 2 * LANES
        wa_bd = _block_diag_groups(w_rg_a[l], gw).astype(BF16)
        wx_bd = _block_diag_groups(w_rg_x[l], gw).astype(BF16)
        lru_consts = (conv_w[l], row2(conv_b[l]), wa_bd, wx_bd, row2(b_rg_a[l]), row2(b_rg_x[l]),
                      row2(lru_lambda[l]))
        post_w = (w_r, w_a, w_oo, row2(norm_ffn[l]), w_fg, w_fu, w_fd, row2(norm_ple[l]), w_pg, w_pp,
                  row2(norm_final))
        final = l == depth - 1

        (xr, gr, ga, gb, kb, ikwb, qT, iqT, vTb, iwT, *caches) = _inproj_p(
            xp, row2(norm_mix[l]), wm_p, wt_p, caches, tm=256, nb=nb, l=l)
        ya, h_last, new_buf = _rglru_p(xr.reshape(nb, t, d), gr.reshape(nb, t, d), *lru_consts, tt=128)
        o = _attn_p(qT, iqT, iwT, kb, vTb, ikwb, dtab_p, nb, t)
        xp = _post(xp, ya.reshape(nb * t, d), o, ga, gb, p_rows, post_w, tm=256, final=final, l=l,
                   p_row0=l * nb * t)
        outs_p.append((h_last, new_buf))

        (xr, gr, ga, gb, k, v, ikw, q, iq) = _inproj_s(xs, row2(norm_mix[l]), wm_p, wt_p, tm=st * sb)
        tmaj = lambda a2: a2.reshape(st, sb, -1)
        ya, h_last, new_buf = _rglru_s(tmaj(xr), tmaj(gr), state_conv[l].transpose(1, 0, 2),
                                       state_rglru_h[l], *lru_consts)
        bmaj = lambda a2: a2.reshape(st, sb, -1).transpose(1, 0, 2)
        k_b, v_b, ikw_b, q_b, iq_b = bmaj(k), bmaj(v), bmaj(ikw), bmaj(q), bmaj(iq)
        iq_rows = iq_b.reshape(sb, st * IDX_HEADS, IDX_DIM)
        w_col = ikw_b[:, :, IDX_DIM:IDX_DIM + IDX_HEADS].reshape(sb, st * IDX_HEADS, 1)
        mask = _select_s(page_table, iq_rows, w_col, ikw_b[:, :, :IDX_DIM].transpose(0, 2, 1),
                         ik_pool, l * n_pool, gs=min(sb, 32), k_top=k_top_s)
        o = _attn_s(page_table, q_b, k_b, v_b, mask, dtab_s, k_pool, v_pool, l * n_pool)
        o = o.transpose(1, 0, 2).reshape(st * sb, ATT_W)
        xs = _post(xs, ya.reshape(st * sb, d), o, ga, gb,
                   p_sample[l].transpose(1, 0, 2).reshape(st * sb, -1), post_w,
                   tm=min(256, st * sb), final=final, l=l)
        outs_s.append((k_b.reshape(sb, st, N_HEADS, HEAD_DIM), v_b.reshape(sb, st, N_HEADS, HEAD_DIM),
                       ikw_b[:, :, :IDX_DIM], h_last, new_buf.transpose(1, 0, 2)))

    y_prompt = xp.reshape(nb, t, d)
    y_sample = xs.reshape(st, sb, d).transpose(1, 0, 2)
    stack = lambda outs, i: jnp.stack([o_[i] for o_ in outs])
    kT_all, vT_all, ikT_all = caches
    heads_last = lambda aT: aT.reshape(depth, nb, N_HEADS, HEAD_DIM, t).transpose(0, 1, 4, 2, 3)
    return (y_prompt, y_sample,
            heads_last(kT_all), heads_last(vT_all), ikT_all.transpose(0, 1, 3, 2),
            stack(outs_p, 0), stack(outs_p, 1),
            stack(outs_s, 0), stack(outs_s, 1), stack(outs_s, 2), stack(outs_s, 3), stack(outs_s, 4))
```

```python
import functools
import math

import jax
import jax.numpy as jnp
from jax import lax
from jax.experimental import pallas as pl
from jax.experimental.pallas import tpu as pltpu

F32 = jnp.float32
BF16 = jnp.bfloat16
I32 = jnp.int32

N_HEADS = 8
HEAD_DIM = 64
ATT_W = N_HEADS * HEAD_DIM
IDX_HEADS = 8
IDX_DIM = 64
TOPK_MAX = 256
LRU_C = 8.0
MAX_DISTANCE = 128
EPS = 1e-6

LANES = 128
SUBLANES = 8
TQ = 128
TK = 256
NEG = -1e30
LOG2E = math.log2(math.e)
INT_MIN = -2147483648
VMEM_LIMIT = 56 * 1024 * 1024


def _dot(a, b):
    return jnp.dot(a, b, preferred_element_type=F32)


def _dot_nt(a, b):
    return lax.dot_general(a, b, (((1,), (1,)), ((), ())), preferred_element_type=F32)


def _sigmoid(x):
    return 0.5 * jnp.tanh(0.5 * x) + 0.5


def _gelu_tanh(x):
    c = math.sqrt(2.0 / math.pi)
    return 0.5 * x * (1.0 + jnp.tanh(c * (x + 0.044715 * (x * x * x))))


def _rms(x, g):
    y = x * lax.rsqrt(jnp.mean(x * x, axis=-1, keepdims=True) + EPS)
    return y * g


def _params(n_axes=1):
    return pltpu.CompilerParams(dimension_semantics=("arbitrary",) * n_axes,
                                vmem_limit_bytes=VMEM_LIMIT)


def _fori_pairs(lo, hi, body, carry):
    n = hi - lo

    def pair(i, c):
        j = lo + 2 * i
        return body(j + 1, body(j, c))

    carry = lax.fori_loop(0, n >> 1, pair, carry)
    return lax.cond((n & 1) == 1, lambda c: body(hi - 1, c), lambda c: c, carry)


def _const_spec(shape):
    nd = len(shape)
    return pl.BlockSpec(shape, lambda *_: (0,) * nd, pipeline_mode=pl.Buffered(1))


def _whole_out_spec(shape):
    nd = len(shape)
    return pl.BlockSpec(shape, lambda *_: (0,) * nd)


def _inproj_p_kernel(x_ref, g_ref, wm_ref, wt_ref, kT_all_ref, vT_all_ref, ikT_all_ref,
                     xr_ref, gr_ref, ga_ref, gb_ref, kb_ref, ikwb_ref,
                     qT_ref, iqT_ref, vTb_ref, iwT_ref, kT_ref, vT_ref, ikT_ref):
    del kT_all_ref, vT_all_ref, ikT_all_ref
    d = x_ref.shape[1]
    hb = _rms(x_ref[...], g_ref[...]).astype(BF16)
    z = _dot(hb, wm_ref[...])
    xr_ref[...] = z[:, 0:d]
    gr_ref[...] = z[:, d:2 * d]
    ga_ref[...] = z[:, 2 * d:3 * d]
    gb_ref[...] = z[:, 3 * d:4 * d]
    o = 4 * d
    kb_ref[...] = z[:, o:o + ATT_W].astype(BF16)
    ikwb_ref[...] = z[:, o + ATT_W:o + ATT_W + LANES].astype(BF16)
    zt = _dot_nt(wt_ref[...], hb)
    qT_ref[...] = (zt[0:ATT_W] * (HEAD_DIM ** -0.5)).astype(BF16)
    iqT_ref[...] = zt[ATT_W:2 * ATT_W].astype(BF16)
    kT_ref[...] = zt[2 * ATT_W:3 * ATT_W]
    vt = zt[3 * ATT_W:4 * ATT_W]
    vT_ref[...] = vt
    for c in range(vTb_ref.shape[0]):
        vTb_ref[c] = vt[:, c * TK:(c + 1) * TK].astype(BF16)
    ikT_ref[...] = zt[4 * ATT_W:4 * ATT_W + IDX_DIM]
    iwT_ref[...] = zt[4 * ATT_W + IDX_DIM:4 * ATT_W + IDX_DIM + IDX_HEADS]


def _inproj_p(x, g, wm, wt, caches, tm, nb, l):
    n, d = x.shape
    t = n // nb
    per_b = t // tm
    row = lambda w: pl.BlockSpec((tm, w), lambda i: (i, 0))
    out_shape = (
        jax.ShapeDtypeStruct((n, d), F32), jax.ShapeDtypeStruct((n, d), F32),
        jax.ShapeDtypeStruct((n, d), F32), jax.ShapeDtypeStruct((n, d), F32),
        jax.ShapeDtypeStruct((n, ATT_W), BF16), jax.ShapeDtypeStruct((n, LANES), BF16),
        jax.ShapeDtypeStruct((ATT_W, n), BF16), jax.ShapeDtypeStruct((ATT_W, n), BF16),
        jax.ShapeDtypeStruct((n // TK, ATT_W, TK), BF16),
        jax.ShapeDtypeStruct((IDX_HEADS, n), F32),
    ) + tuple(jax.ShapeDtypeStruct(c.shape, c.dtype) for c in caches)
    colT = lambda r: pl.BlockSpec((r, tm), lambda i: (0, i))
    layer_seq = lambda r: pl.BlockSpec((None, None, r, tm), lambda i: (l, i // per_b, 0, i % per_b))
    out_specs = (row(d), row(d), row(d), row(d), row(ATT_W), row(LANES), colT(ATT_W), colT(ATT_W),
                 pl.BlockSpec((tm // TK, ATT_W, TK), lambda i: (i, 0, 0)), colT(IDX_HEADS),
                 layer_seq(ATT_W), layer_seq(ATT_W), layer_seq(IDX_DIM))
    n_plain_in, n_plain_out = 4, 10
    return pl.pallas_call(
        _inproj_p_kernel, grid=(n // tm,),
        in_specs=[row(d), _const_spec(g.shape), _const_spec(wm.shape), _const_spec(wt.shape)]
                 + [pl.BlockSpec(memory_space=pl.ANY)] * len(caches),
        out_specs=out_specs, out_shape=out_shape,
        input_output_aliases={n_plain_in + i: n_plain_out + i for i in range(len(caches))},
        compiler_params=_params(1), name="inproj_prompt")(x, g, wm, wt, *caches)


def _inproj_s_kernel(x_ref, g_ref, wm_ref, wt_ref,
                     xr_ref, gr_ref, ga_ref, gb_ref, k_ref, v_ref, ikw_ref, q_ref, iq_ref):
    d = x_ref.shape[1]
    hb = _rms(x_ref[...], g_ref[...]).astype(BF16)
    z = _dot(hb, wm_ref[...])
    xr_ref[...] = z[:, 0:d]
    gr_ref[...] = z[:, d:2 * d]
    ga_ref[...] = z[:, 2 * d:3 * d]
    gb_ref[...] = z[:, 3 * d:4 * d]
    o = 4 * d
    k_ref[...] = z[:, o:o + ATT_W]
    ikw_ref[...] = z[:, o + ATT_W:o + ATT_W + LANES]
    qiq = _dot_nt(hb, wt_ref[0:2 * ATT_W, :])
    q_ref[...] = (qiq[:, 0:ATT_W] * (HEAD_DIM ** -0.5)).astype(BF16)
    iq_ref[...] = qiq[:, ATT_W:2 * ATT_W].astype(BF16)
    v_ref[...] = _dot_nt(hb, wt_ref[3 * ATT_W:4 * ATT_W, :])


def _inproj_s(x, g, wm, wt, tm):
    n, d = x.shape
    row = lambda w: pl.BlockSpec((tm, w), lambda i: (i, 0))
    out_shape = (
        jax.ShapeDtypeStruct((n, d), F32), jax.ShapeDtypeStruct((n, d), F32),
        jax.ShapeDtypeStruct((n, d), F32), jax.ShapeDtypeStruct((n, d), F32),
        jax.ShapeDtypeStruct((n, ATT_W), F32), jax.ShapeDtypeStruct((n, ATT_W), F32),
        jax.ShapeDtypeStruct((n, LANES), F32),
        jax.ShapeDtypeStruct((n, ATT_W), BF16), jax.ShapeDtypeStruct((n, ATT_W), BF16),
    )
    out_specs = (row(d), row(d), row(d), row(d), row(ATT_W), row(ATT_W), row(LANES),
                 row(ATT_W), row(ATT_W))
    return pl.pallas_call(
        _inproj_s_kernel, grid=(n // tm,),
        in_specs=[row(d), _const_spec(g.shape), _const_spec(wm.shape), _const_spec(wt.shape)],
        out_specs=out_specs, out_shape=out_shape, compiler_params=_params(1),
        name="inproj_sample")(x, g, wm, wt)


def _lru_gates(xc, wa_ref, wx_ref, ba, bx, lam):
    xcb = xc.astype(BF16)
    gw = wa_ref.shape[1]
    r_parts, i_parts = [], []
    for g in range(wa_ref.shape[0]):
        xs = xcb[:, g * gw:(g + 1) * gw]
        r_parts.append(_dot(xs, wa_ref[g]))
        i_parts.append(_dot(xs, wx_ref[g]))
    r = _sigmoid(jnp.concatenate(r_parts, axis=1) + ba)
    i = _sigmoid(jnp.concatenate(i_parts, axis=1) + bx)
    log_sig_lam = jnp.minimum(lam, 0.0) - jnp.log(1.0 + jnp.exp(-jnp.abs(lam)))
    log_a = LRU_C * r * log_sig_lam
    a = jnp.exp(log_a)
    u = jnp.sqrt(1.0 - a * a) * (i * xc)
    return a, u


def _rglru_p_kernel(xr_ref, gr_ref, cw_ref, cb_ref, wa_ref, wx_ref, ba_ref, bx_ref, lam_ref,
                    ya_ref, hlast_ref, buf_ref,
                    xx_ref, a_ref, u_ref, h_ref):
    step = pl.program_id(0)
    nb, tt, d = xr_ref.shape
    cw = cw_ref.shape[0]

    @pl.when(step == 0)
    def _():
        xx_ref[:, 0:8, :] = jnp.zeros((nb, 8, d), F32)
        h_ref[...] = jnp.zeros_like(h_ref)

    for b in range(nb):
        x = xr_ref[b]
        xx_ref[b, 8:8 + tt, :] = x
        xc = cb_ref[...] + x * cw_ref[cw - 1:cw, :]
        for j in range(cw - 1):
            sh = cw - 1 - j
            xc = xc + xx_ref[b, 8 - sh:8 - sh + tt, :] * cw_ref[j:j + 1, :]
        xx_ref[b, 0:8, :] = x[tt - 8:tt, :]
        a, u = _lru_gates(xc, wa_ref, wx_ref, ba_ref[...], bx_ref[...], lam_ref[...])
        a_ref[b] = a
        u_ref[b] = u

    def scan_body(t, hs):
        new = []
        for b in range(nb):
            h = a_ref[b, pl.ds(t, 1), :] * hs[b] + u_ref[b, pl.ds(t, 1), :]
            u_ref[b, pl.ds(t, 1), :] = h
            new.append(h)
        return tuple(new)

    hs = lax.fori_loop(0, tt, scan_body, tuple(h_ref[b:b + 1, :] for b in range(nb)), unroll=8)
    for b in range(nb):
        h_ref[b:b + 1, :] = hs[b]
        ya_ref[b] = (u_ref[b] * _gelu_tanh(gr_ref[b])).astype(BF16)
        buf_ref[b] = xr_ref[b, tt - (cw - 1):tt, :]
    hlast_ref[...] = h_ref[...]


def _rglru_p(xr, gr, cw, cb, wa, wx, ba, bx, lam, tt):
    nb, t, d = xr.shape
    blk = pl.BlockSpec((nb, tt, d), lambda i: (0, i, 0))
    consts = [cw, cb, wa, wx, ba, bx, lam]
    return pl.pallas_call(
        _rglru_p_kernel, grid=(t // tt,),
        in_specs=[blk, blk] + [_const_spec(c.shape) for c in consts],
        out_specs=(blk, _whole_out_spec((nb, d)), _whole_out_spec((nb, cw.shape[0] - 1, d))),
        out_shape=(jax.ShapeDtypeStruct((nb, t, d), BF16), jax.ShapeDtypeStruct((nb, d), F32),
                   jax.ShapeDtypeStruct((nb, cw.shape[0] - 1, d), F32)),
        scratch_shapes=[pltpu.VMEM((nb, tt + 8, d), F32), pltpu.VMEM((nb, tt, d), F32),
                        pltpu.VMEM((nb, tt, d), F32), pltpu.VMEM((nb, d), F32)],
        compiler_params=_params(1), name="rglru_prompt")(xr, gr, *consts)


def _rglru_s_kernel(xr_ref, gr_ref, st_ref, h0_ref, cw_ref, cb_ref, wa_ref, wx_ref, ba_ref,
                    bx_ref, lam_ref, ya_ref, hlast_ref, buf_ref):
    t_len, nb, d = xr_ref.shape
    cw = cw_ref.shape[0]
    rows = [st_ref[j] for j in range(cw - 1)] + [xr_ref[t] for t in range(t_len)]
    h = h0_ref[...]
    for t in range(t_len):
        xc = cb_ref[...] + rows[t + cw - 1] * cw_ref[cw - 1:cw, :]
        for j in range(cw - 1):
            xc = xc + rows[t + j] * cw_ref[j:j + 1, :]
        a, u = _lru_gates(xc, wa_ref, wx_ref, ba_ref[...], bx_ref[...], lam_ref[...])
        h = a * h + u
        ya_ref[t] = (h * _gelu_tanh(gr_ref[t])).astype(BF16)
    hlast_ref[...] = h
    for j in range(cw - 1):
        buf_ref[j] = rows[t_len + j]


def _rglru_s(xr, gr, st, h0, cw, cb, wa, wx, ba, bx, lam):
    t_len, nb, d = xr.shape
    args = [xr, gr, st, h0, cw, cb, wa, wx, ba, bx, lam]
    return pl.pallas_call(
        _rglru_s_kernel, grid=(1,),
        in_specs=[_const_spec(a.shape) for a in args],
        out_specs=(_whole_out_spec((t_len, nb, d)), _whole_out_spec((nb, d)),
                   _whole_out_spec((cw.shape[0] - 1, nb, d))),
        out_shape=(jax.ShapeDtypeStruct((t_len, nb, d), BF16), jax.ShapeDtypeStruct((nb, d), F32),
                   jax.ShapeDtypeStruct((cw.shape[0] - 1, nb, d), F32)),
        compiler_params=_params(1), name="rglru_sample")(*args)


def _sortable_key(score):
    bits = pltpu.bitcast(score, I32)
    return bits ^ ((bits >> 31) & 0x7FFFFFFF)


GROUP = 16
QT = 2


def _sort_network(n):
    pairs = []
    p = 1
    while p < n:
        k = p
        while k >= 1:
            for j in range(k % p, n - k, 2 * k):
                for i in range(min(k, n - j - k)):
                    if (i + j) // (2 * p) == (i + j + k) // (2 * p):
                        pairs.append((i + j, i + j + k))
            k //= 2
        p *= 2
    return pairs


def _count_in_sorted_group(v, cmp):
    one = lambda m, w: jnp.where(m, w, 0)
    pick = jnp.where
    m1 = cmp(v(7))
    m2 = cmp(pick(m1, v(11), v(3)))
    m3 = cmp(pick(m2, pick(m1, v(13), v(5)), pick(m1, v(9), v(1))))
    if_m3 = pick(m2, pick(m1, v(14), v(6)), pick(m1, v(10), v(2)))
    if_not = pick(m2, pick(m1, v(12), v(4)), pick(m1, v(8), v(0)))
    m4 = cmp(pick(m3, if_m3, if_not))
    m5 = cmp(v(15))
    return one(m1, 8) + one(m2, 4) + one(m3, 2) + one(m4, 1) + one(m5, 1)


def _attn_p_kernel(qT_ref, iqT_ref, iwT_ref, kb_ref, vT_ref, ikw_ref, dtab_ref, o_ref,
                   keys_ref, sorted_ref, rhs_ref, qbd_ref, m_ref, acc_ref, x_ref, *, k_top):
    qp = pl.program_id(1)
    n_chunks = qp + 1
    half = LANES // 2
    zeros_half = jnp.zeros((half, TQ), BF16)
    s_iota = lax.broadcasted_iota(I32, (TK, TQ), 0)
    t_iota = lax.broadcasted_iota(I32, (TK, TQ), 1)
    w_all = (iwT_ref[...] * (IDX_HEADS ** -0.5)) * (IDX_DIM ** -0.5)

    def tile_consts(qi):
        qb = QT * qp + qi
        lanes = slice(qi * TQ, (qi + 1) * TQ)
        n_far = jnp.maximum(qb - 1, 0) >> 1
        return qb * TQ, lanes, n_far

    def score_body(j, carry, band, qi, t0, lanes, w):
        off = pl.multiple_of(j * TK, TK)
        ikc = ikw_ref[pl.ds(off, TK), :]
        score = jnp.zeros((TK, TQ), F32)
        for c in range(N_HEADS // 2):
            dd = _dot(ikc, rhs_ref[qi, c])
            for hh in range(2):
                h = 2 * c + hh
                score = score + jnp.maximum(dd[:, hh * TQ:(hh + 1) * TQ], 0.0) * w[h:h + 1, :]
        key = _sortable_key(score)
        if band:
            key = jnp.where(s_iota + off <= t0 + t_iota, key, INT_MIN)
        keys_ref[pl.ds(off, TK), lanes] = key
        per = TK // GROUP
        v = [key[i * per:(i + 1) * per, :] for i in range(GROUP)]
        for a, b in _sort_network(GROUP):
            v[a], v[b] = jnp.maximum(v[a], v[b]), jnp.minimum(v[a], v[b])
        for i in range(GROUP):
            sorted_ref[j, i, :, lanes] = v[i]
        return carry

    for qi in range(QT):
        t0, lanes, n_far = tile_consts(qi)
        for c in range(N_HEADS // 2):
            for hh in range(2):
                h = 2 * c + hh
                iq_h = iqT_ref[h * IDX_DIM:(h + 1) * IDX_DIM, lanes]
                rhs_ref[qi, c, :, hh * TQ:(hh + 1) * TQ] = jnp.concatenate([iq_h, zeros_half], axis=0)
                q_h = qT_ref[h * HEAD_DIM:(h + 1) * HEAD_DIM, lanes]
                parts = [zeros_half, q_h] if hh else [q_h, zeros_half]
                qbd_ref[qi, c, :, hh * TQ:(hh + 1) * TQ] = jnp.concatenate(parts, axis=0)
        body = functools.partial(score_body, qi=qi, t0=t0, lanes=lanes, w=w_all[:, lanes])
        _fori_pairs(0, n_far, functools.partial(body, band=False), 0)
        _fori_pairs(n_far, n_chunks, functools.partial(body, band=True), 0)

    @pl.when((n_chunks & 1) == 1)
    def _():
        sorted_ref[n_chunks] = jnp.full(sorted_ref.shape[1:], INT_MIN, I32)

    def count(cmp):
        def body(i, acc):
            for u in range(2):
                acc = acc + _count_in_sorted_group(lambda r: sorted_ref[2 * i + u, r], cmp)
            return acc
        acc = lax.fori_loop(0, (n_chunks + 1) >> 1, body, jnp.zeros((TK // GROUP, QT * TQ), I32))
        return acc.sum(axis=0, keepdims=True)

    def bit_body(p, carry):
        thr, n_ge = carry
        bit = jnp.left_shift(jnp.int32(1), 31 - p)
        cand = thr ^ bit
        cnt = count(lambda kc: kc >= cand)
        ok = cnt >= k_top
        return jnp.where(ok, cand, thr), jnp.where(ok, cnt, n_ge)

    thr_all, n_ge = lax.fori_loop(0, 32, bit_body, (jnp.full((1, QT * TQ), INT_MIN, I32),
                                                    jnp.zeros((1, QT * TQ), I32)))
    any_ties = jnp.max(jnp.where((thr_all != INT_MIN) & (n_ge != k_top), 1, 0))

    fold = lambda a, op: op(a.reshape(TK // SUBLANES, SUBLANES, TQ), axis=0)

    def logits_body(j, carry, band, ties, qi, t0, lanes, thr, need=None, ltri=None):
        taken, mx = carry
        off = pl.multiple_of(j * TK, TK)
        keyc = keys_ref[pl.ds(off, TK), lanes]
        if ties:
            eq = keyc == thr
            prefix = _dot(ltri, eq.astype(BF16))
            sel = (keyc > thr) | (eq & (prefix + taken <= need))
            taken = taken + prefix[TK - 1:TK, :]
        else:
            sel = keyc >= thr
        if band:
            sel = sel & (s_iota + off <= t0 + t_iota)
            tab = (t0 - off) // TQ
        kc = kb_ref[pl.ds(off, TK), :]
        new_mx = []
        for c in range(N_HEADS // 2):
            lg = _dot(kc[:, c * LANES:(c + 1) * LANES], qbd_ref[qi, c])
            for hh in range(2):
                h = 2 * c + hh
                x = lg[:, hh * TQ:(hh + 1) * TQ]
                if band:
                    x = x + dtab_ref[tab, h]
                x = jnp.where(sel, x * LOG2E, NEG)
                x_ref[h, pl.ds(off, TK), :] = x
                new_mx.append(jnp.maximum(mx[h * SUBLANES:(h + 1) * SUBLANES, :], fold(x, jnp.max)))
        return taken, jnp.concatenate(new_mx, axis=0)

    def all_logits(ties, qi, t0, lanes, n_far):
        thr = thr_all[:, lanes]
        extra = {}
        if ties:
            n_gt = count(lambda kc: kc > thr_all)
            extra["need"] = (k_top - n_gt[:, lanes]).astype(F32)
            extra["ltri"] = (lax.broadcasted_iota(I32, (TK, TK), 1)
                             <= lax.broadcasted_iota(I32, (TK, TK), 0)).astype(BF16)
        body = functools.partial(logits_body, ties=ties, qi=qi, t0=t0, lanes=lanes, thr=thr, **extra)
        carry = (jnp.zeros((1, TQ), F32), jnp.full((N_HEADS * SUBLANES, TQ), NEG, F32))
        carry = _fori_pairs(0, n_far, functools.partial(body, band=False), carry)
        return _fori_pairs(n_far, n_chunks, functools.partial(body, band=True), carry)[1]

    def pv_body(j, lsum):
        off = pl.multiple_of(j * TK, TK)
        new_lsum = []
        for h in range(N_HEADS):
            p = jnp.exp2(x_ref[h, pl.ds(off, TK), :] - m_ref[h:h + 1, :])
            new_lsum.append(lsum[h * SUBLANES:(h + 1) * SUBLANES, :] + fold(p, jnp.sum))
            rows = slice(h * HEAD_DIM, (h + 1) * HEAD_DIM)
            acc_ref[rows, :] += _dot(vT_ref[j, rows, :], p.astype(BF16))
        return jnp.concatenate(new_lsum, axis=0)

    zero_sums = jnp.zeros((N_HEADS * SUBLANES, TQ), F32)

    def start_pv(mx):
        m_ref[...] = jnp.concatenate(
            [mx[h * SUBLANES:(h + 1) * SUBLANES, :].max(axis=0, keepdims=True)
             for h in range(N_HEADS)], axis=0)
        acc_ref[...] = jnp.zeros(acc_ref.shape, F32)

    def finish_tile(qi, lsum):
        for h in range(N_HEADS):
            rows = slice(h * HEAD_DIM, (h + 1) * HEAD_DIM)
            inv = 1.0 / lsum[h * SUBLANES:(h + 1) * SUBLANES, :].sum(axis=0, keepdims=True)
            acc_ref[rows, :] = acc_ref[rows, :] * inv
        o_ref[qi * TQ:(qi + 1) * TQ, :] = acc_ref[...].T.astype(BF16)

    def tiles_in_turn():
        for qi in range(QT):
            t0, lanes, n_far = tile_consts(qi)
            start_pv(all_logits(True, qi, t0, lanes, n_far))
            finish_tile(qi, _fori_pairs(0, n_chunks, pv_body, zero_sums))
        return 0

    def tiles_overlapped():
        assert QT == 2
        t0_a, lanes_a, n_far_a = tile_consts(0)
        t0_b, lanes_b, n_far_b = tile_consts(1)
        start_pv(all_logits(False, 0, t0_a, lanes_a, n_far_a))
        logits_b = functools.partial(logits_body, ties=False, qi=1, t0=t0_b, lanes=lanes_b,
                                     thr=thr_all[:, lanes_b])

        def both(j, carry, band):
            lsum_a, state_b = carry
            lsum_a = pv_body(j, lsum_a)
            return lsum_a, logits_b(j, state_b, band=band)

        carry = (zero_sums, (jnp.zeros((1, TQ), F32), jnp.full((N_HEADS * SUBLANES, TQ), NEG, F32)))
        carry = _fori_pairs(0, n_far_b, functools.partial(both, band=False), carry)
        lsum_a, (_, mx_b) = _fori_pairs(n_far_b, n_chunks, functools.partial(both, band=True), carry)
        finish_tile(0, lsum_a)
        start_pv(mx_b)
        finish_tile(1, _fori_pairs(0, n_chunks, pv_body, zero_sums))
        return 0

    lax.cond(any_ties > 0, tiles_in_turn, tiles_overlapped)


def _attn_p(qT, iqT, iwT, kb, vT, ikwb, dtab, nb, t):
    n = nb * t
    nq = t // (QT * TQ)
    k_top = min(TOPK_MAX, t // 4)
    colT = lambda r: pl.BlockSpec((r, QT * TQ), lambda b, q: (0, b * nq + q))
    return pl.pallas_call(
        functools.partial(_attn_p_kernel, k_top=k_top), grid=(nb, nq),
        in_specs=[colT(ATT_W), colT(ATT_W), colT(IDX_HEADS),
                  pl.BlockSpec((t, ATT_W), lambda b, q: (b, 0)),
                  pl.BlockSpec((t // TK, ATT_W, TK), lambda b, q: (b, 0, 0)),
                  pl.BlockSpec((t, LANES), lambda b, q: (b, 0)),
                  _const_spec(dtab.shape)],
        out_specs=pl.BlockSpec((QT * TQ, ATT_W), lambda b, q: (b * nq + q, 0)),
        out_shape=jax.ShapeDtypeStruct((n, ATT_W), BF16),
        scratch_shapes=[pltpu.VMEM((t, QT * TQ), I32),
                        pltpu.VMEM((t // TK + 1, GROUP, TK // GROUP, QT * TQ), I32),
                        pltpu.VMEM((QT, N_HEADS // 2, LANES, 2 * TQ), BF16),
                        pltpu.VMEM((QT, N_HEADS // 2, LANES, 2 * TQ), BF16),
                        pltpu.VMEM((N_HEADS, TQ), F32), pltpu.VMEM((ATT_W, TQ), F32),
                        pltpu.VMEM((N_HEADS, t, TQ), F32)],
        compiler_params=_params(2), name="attn_prompt")(qT, iqT, iwT, kb, vT, ikwb, dtab)


def _select_s_kernel(pt_ref, iq_ref, wcol_ref, iknew_ref, pool_ref, mask_ref,
                     ikbuf_ref, score_ref, sem, *, k_top, layer_off, t_len):
    g = pl.program_id(0)
    gs, _, s_pad = ikbuf_ref.shape
    n_pages = pt_ref.shape[1]
    page = pool_ref.shape[2]
    past = n_pages * page

    def page_copy(i, j):
        return pltpu.make_async_copy(pool_ref.at[layer_off + pt_ref[g * gs + i, j]],
                                     ikbuf_ref.at[i, :, j * page:(j + 1) * page], sem)

    def start_body(i, c):
        for j in range(n_pages):
            page_copy(i, j).start()
        return c

    lax.fori_loop(0, gs, start_body, 0)
    ikbuf_ref[:, :, past:s_pad] = jnp.zeros((gs, IDX_DIM, s_pad - past), F32)
    ikbuf_ref[:, :, past:past + t_len] = iknew_ref[...]

    def wait_body(i, c):
        for j in range(n_pages):
            page_copy(i, j).wait()
        return c

    lax.fori_loop(0, gs, wait_body, 0)

    tile = score_ref.shape[1]
    per_tile = tile // t_len

    def score_body(it, c):
        for u in range(per_tile):
            i = it * per_tile + u
            dd = _dot(iq_ref[i], ikbuf_ref[i].astype(BF16))
            wv = (wcol_ref[i] * (IDX_HEADS ** -0.5)) * (IDX_DIM ** -0.5)
            sc = (jnp.maximum(dd, 0.0) * wv).reshape(t_len, IDX_HEADS, s_pad).sum(axis=1)
            score_ref[it, u * t_len:(u + 1) * t_len, :] = sc
        return c

    lax.fori_loop(0, gs // per_tile, score_body, 0)

    rows = gs * t_len
    s_idx = lax.broadcasted_iota(I32, (rows, s_pad), 1)
    t_idx = lax.broadcasted_iota(I32, (rows, s_pad), 0) % t_len
    valid = s_idx <= past + t_idx
    keys = jnp.where(valid, _sortable_key(score_ref[...].reshape(rows, s_pad)), INT_MIN)

    def bit_body(p, thr):
        bit = jnp.left_shift(jnp.int32(1), 31 - p)
        cand = thr ^ bit
        cnt = jnp.sum((keys >= cand).astype(I32), axis=1, keepdims=True)
        return jnp.where(cnt >= k_top, cand, thr)

    thr = lax.fori_loop(0, 32, bit_body, jnp.full((rows, 1), INT_MIN, I32))
    gt = keys > thr
    eq = keys == thr
    need = (k_top - jnp.sum(gt.astype(I32), axis=1, keepdims=True)).astype(F32)
    utri = (lax.broadcasted_iota(I32, (LANES, LANES), 0) <= lax.broadcasted_iota(I32, (LANES, LANES), 1)
            ).astype(BF16)
    eqb = eq.astype(BF16)
    carry = jnp.zeros((rows, 1), F32)
    for c in range(s_pad // LANES):
        cols = slice(c * LANES, (c + 1) * LANES)
        prefix = _dot(eqb[:, cols], utri) + carry
        take = gt[:, cols] | (eq[:, cols] & (prefix <= need))
        take = take & valid[:, cols]
        m = jnp.where(take, 0.0, NEG).reshape(gs // per_tile, tile, LANES)
        for u in range(per_tile):
            mask_ref[:, u, :, cols] = m[:, u * t_len:(u + 1) * t_len, :]
        carry = prefix[:, LANES - 1:LANES]


def _select_s(page_table, iq, wcol, iknew, pool, layer, gs, k_top):
    nb, rows_q, _ = iq.shape
    t_len = rows_q // IDX_HEADS
    n_pages = page_table.shape[1]
    page = pool.shape[2]
    s_pad = n_pages * page + LANES
    assert SUBLANES % t_len == 0
    per_tile = SUBLANES // t_len
    assert nb % gs == 0 and gs % per_tile == 0
    kern = functools.partial(_select_s_kernel, k_top=k_top, layer_off=layer, t_len=t_len)
    grid_spec = pltpu.PrefetchScalarGridSpec(
        num_scalar_prefetch=1, grid=(nb // gs,),
        in_specs=[pl.BlockSpec((gs, rows_q, IDX_DIM), lambda g, pt: (g, 0, 0)),
                  pl.BlockSpec((gs, rows_q, 1), lambda g, pt: (g, 0, 0)),
                  pl.BlockSpec((gs, IDX_DIM, t_len), lambda g, pt: (g, 0, 0)),
                  pl.BlockSpec(memory_space=pl.ANY)],
        out_specs=pl.BlockSpec((gs // per_tile, per_tile, t_len, s_pad), lambda g, pt: (g, 0, 0, 0)),
        scratch_shapes=[pltpu.VMEM((gs, IDX_DIM, s_pad), F32),
                        pltpu.VMEM((gs // per_tile, SUBLANES, s_pad), F32),
                        pltpu.SemaphoreType.DMA(())])
    return pl.pallas_call(
        kern, grid_spec=grid_spec,
        out_shape=jax.ShapeDtypeStruct((nb // per_tile, per_tile, t_len, s_pad), F32),
        compiler_params=_params(1), name="select_sample")(page_table, iq, wcol, iknew, pool)


def _attn_s_kernel(pt_ref, q_ref, knew_ref, vnew_ref, mask_ref, dtab_ref, *rest, n_pages, t_len):
    k_pages = rest[:n_pages]
    v_pages = rest[n_pages:2 * n_pages]
    o_ref, kpad_ref, vpad_ref = rest[2 * n_pages:]
    rows = t_len * N_HEADS
    page = k_pages[0].shape[2]

    def flat_bf16(page_ref):
        return page_ref[...].reshape(ATT_W, page).astype(BF16)

    def per_head_rows(x):
        return jnp.concatenate(
            [jnp.broadcast_to(x[t:t + 1, :], (N_HEADS, x.shape[1])) for t in range(t_len)], axis=0)

    col_head = lax.broadcasted_iota(I32, (rows, ATT_W), 1) // HEAD_DIM
    row_head = lax.broadcasted_iota(I32, (rows, ATT_W), 0) % N_HEADS
    own = col_head == row_head
    qbd = jnp.where(own, per_head_rows(q_ref[...].astype(F32)), 0.0).astype(BF16)
    kpad_ref[...] = jnp.zeros(kpad_ref.shape, F32)
    vpad_ref[...] = jnp.zeros(vpad_ref.shape, F32)
    kpad_ref[0:t_len, :] = knew_ref[...]
    vpad_ref[0:t_len, :] = vnew_ref[...]
    knew = kpad_ref[...].astype(BF16)
    vnew = vpad_ref[...].astype(BF16)
    logits = [_dot(qbd, flat_bf16(kp)) for kp in k_pages] + [_dot_nt(qbd, knew)]
    x = jnp.concatenate(logits, axis=1)
    x = x + dtab_ref[...] + per_head_rows(mask_ref[...])
    m = x.max(axis=1, keepdims=True)
    p = jnp.exp(x - m)
    l = p.sum(axis=1, keepdims=True)
    pb = p.astype(BF16)
    out = _dot(pb[:, n_pages * page:], vnew)
    for j, vp in enumerate(v_pages):
        out = out + _dot_nt(pb[:, j * page:(j + 1) * page], flat_bf16(vp))
    out = out / l
    out = jnp.where(own, out, 0.0).reshape(t_len, N_HEADS, ATT_W).sum(axis=1)
    o_ref[...] = out.astype(BF16)


def _attn_s(page_table, q, knew, vnew, mask, dtab, k_pool, v_pool, layer_off):
    nb, t_len, _ = q.shape
    n_pages = page_table.shape[1]
    page = k_pool.shape[3]
    s_pad = n_pages * page + LANES

    def page_spec(j):
        return pl.BlockSpec((None, N_HEADS, HEAD_DIM, page),
                            lambda b, pt, j=j: (layer_off + pt[b, j], 0, 0, 0))

    seq = lambda w: pl.BlockSpec((None, t_len, w), lambda b, pt: (b, 0, 0))
    per_tile = mask.shape[1]
    mask_spec = pl.BlockSpec((None, None, t_len, s_pad),
                             lambda b, pt: (b // per_tile, b % per_tile, 0, 0))
    grid_spec = pltpu.PrefetchScalarGridSpec(
        num_scalar_prefetch=1, grid=(nb,),
        in_specs=[seq(ATT_W), seq(ATT_W), seq(ATT_W), mask_spec,
                  pl.BlockSpec(dtab.shape, lambda b, pt: (0, 0))]
                 + [page_spec(j) for j in range(n_pages)] * 2,
        out_specs=seq(ATT_W),
        scratch_shapes=[pltpu.VMEM((page, ATT_W), F32), pltpu.VMEM((page, ATT_W), F32)])
    kern = functools.partial(_attn_s_kernel, n_pages=n_pages, t_len=t_len)
    return pl.pallas_call(
        kern, grid_spec=grid_spec, out_shape=jax.ShapeDtypeStruct((nb, t_len, ATT_W), BF16),
        compiler_params=_params(1), name="attn_sample")(
            page_table, q, knew, vnew, mask, dtab, *([k_pool] * n_pages), *([v_pool] * n_pages))


def _post_kernel(x_ref, ya_ref, o_ref, ga_ref, gb_ref, p_ref, wr_ref, wa_ref, wo_ref,
                 nf_ref, wg_ref, wu_ref, wd_ref, np_ref, wpg_ref, wpp_ref, nfin_ref,
                 out_ref, *, final):
    a = _dot(ya_ref[...], wr_ref[...])
    b = _dot(o_ref[...], wa_ref[...])
    mix = _sigmoid(ga_ref[...]) * a + _sigmoid(gb_ref[...]) * b
    x = x_ref[...] + _dot(mix.astype(BF16), wo_ref[...])
    h2 = _rms(x, nf_ref[...]).astype(BF16)
    g = _dot(h2, wg_ref[...])
    u = _dot(h2, wu_ref[...])
    act = (g * _sigmoid(g)) * u
    x = x + _dot(act.astype(BF16), wd_ref[...])
    h3 = _rms(x, np_ref[...]).astype(BF16)
    x = x + _sigmoid(_dot(h3, wpg_ref[...])) * _dot(p_ref[...].astype(BF16), wpp_ref[...])
    if final:
        x = _rms(x, nfin_ref[...])
    out_ref[...] = x


def _layer_spec(w, l):
    tail = (0,) * (w.ndim - 1)
    return pl.BlockSpec((None,) + w.shape[1:], lambda *_: (l,) + tail,
                        pipeline_mode=pl.Buffered(1))


def _post(x, ya, o, ga, gb, p, weights, tm, final, l, p_row0=0):
    n, d = x.shape
    row = lambda w: pl.BlockSpec((tm, w), lambda i: (i, 0))
    p_blk0 = p_row0 // tm
    return pl.pallas_call(
        functools.partial(_post_kernel, final=final), grid=(n // tm,),
        in_specs=[row(d), row(d), row(ATT_W), row(d), row(d),
                  pl.BlockSpec((tm, p.shape[1]), lambda i: (i + p_blk0, 0))]
                 + [_layer_spec(c, l) if c.ndim == 3 else _const_spec(c.shape) for c in weights],
        out_specs=row(d), out_shape=jax.ShapeDtypeStruct((n, d), F32),
        compiler_params=_params(1), name="post")(x, ya, o, ga, gb, p, *weights)


def _rel_bucket(n, n_buckets):
    max_exact = n_buckets // 2
    nf = jnp.maximum(n, 1).astype(F32)
    large = max_exact + (jnp.log(nf / max_exact) / math.log(MAX_DISTANCE / max_exact)
                         * (n_buckets - max_exact)).astype(I32)
    large = jnp.minimum(large, n_buckets - 1)
    return jnp.where(n < max_exact, n, large)


def _bias_delta(rel_bias, n):
    nbk = rel_bias.shape[0]
    b = rel_bias.astype(F32)
    onehot = _rel_bucket(jnp.maximum(n, 0), nbk)[..., None] == jnp.arange(nbk, dtype=I32)
    picked = jnp.where(onehot[..., None], b, 0.0).sum(axis=-2)
    return picked - b[nbk - 1]


def _block_diag_groups(w, group):
    nbk, c, _ = w.shape
    per = group // c
    wg = w.reshape(nbk // per, per, c, c)
    eye = jnp.eye(per, dtype=w.dtype)
    return jnp.einsum('gpcd,pq->gpcqd', wg, eye).reshape(nbk // per, group, group)


def kernel(x_prompt, x_sample, p_prompt, p_sample, cache_k, cache_v, cache_idx_k, state_rglru_h,
           state_conv, page_table, rel_bias, norm_mix, w_in, conv_w, conv_b, w_rg_a, b_rg_a,
           w_rg_x, b_rg_x, lru_lambda, w_rnn_out, w_att_out, w_o, norm_ffn, w_ffn_gate, w_ffn_up,
           w_ffn_down, norm_ple, w_ple_gate, w_ple_proj, norm_final):
    nb, t, d = x_prompt.shape
    sb, st, _ = x_sample.shape
    depth = w_in.shape[0]
    n_pool, page = cache_k.shape[1], cache_k.shape[2]
    n_pages = page_table.shape[1]
    past = n_pages * page
    cw = conv_w.shape[1]
    s_pad = past + LANES
    k_top_s = min(TOPK_MAX, (past + st) // 4)
    row2 = lambda v: v.reshape(1, -1).astype(F32)

    idx3 = jnp.arange(3, dtype=I32)[:, None, None]
    s_rel = jnp.arange(TK, dtype=I32)[None, :, None]
    t_rel = jnp.arange(TQ, dtype=I32)[None, None, :]
    dtab_p = _bias_delta(rel_bias, idx3 * TQ + t_rel - s_rel).transpose(0, 3, 1, 2)
    s_all = jnp.arange(s_pad, dtype=I32)[None, :]
    q_pos = past + jnp.arange(st, dtype=I32)[:, None]
    dtab_s = _bias_delta(rel_bias, q_pos - s_all)
    dtab_s = dtab_s.transpose(0, 2, 1).reshape(st * N_HEADS, s_pad)

    k_pool = cache_k.transpose(0, 1, 3, 4, 2).reshape(depth * n_pool, N_HEADS, HEAD_DIM, page)
    v_pool = cache_v.transpose(0, 1, 3, 4, 2).reshape(depth * n_pool, N_HEADS, HEAD_DIM, page)
    ik_pool = cache_idx_k.transpose(0, 1, 3, 2).reshape(depth * n_pool, IDX_DIM, page)

    w_r, w_a, w_oo = w_rnn_out.astype(BF16), w_att_out.astype(BF16), w_o.astype(BF16)
    w_fg, w_fu, w_fd = w_ffn_gate.astype(BF16), w_ffn_up.astype(BF16), w_ffn_down.astype(BF16)
    w_pg, w_pp = w_ple_gate.astype(BF16), w_ple_proj.astype(BF16)
    p_rows = p_prompt.reshape(depth * nb * t, -1)

    xp = x_prompt.reshape(nb * t, d)
    xs = x_sample.transpose(1, 0, 2).reshape(st * sb, d)
    outs_p, outs_s = [], []
    caches = (jnp.zeros((depth, nb, ATT_W, t), F32), jnp.zeros((depth, nb, ATT_W, t), F32),
              jnp.zeros((depth, nb, IDX_DIM, t), F32))
    for l in range(depth):
        splits = [d, 2 * d, 2 * d + ATT_W, 2 * d + 2 * ATT_W, 2 * d + 3 * ATT_W,
                  2 * d + 3 * ATT_W + IDX_HEADS * IDX_DIM,
                  2 * d + 3 * ATT_W + IDX_HEADS * IDX_DIM + IDX_DIM,
                  2 * d + 3 * ATT_W + IDX_HEADS * IDX_DIM + IDX_DIM + IDX_HEADS,
                  3 * d + 3 * ATT_W + IDX_HEADS * IDX_DIM + IDX_DIM + IDX_HEADS]
        w_xr, w_gr, w_q, w_k, w_v, w_iq, w_ik, w_iw, w_ga, w_gb = jnp.split(w_in[l], splits, axis=1)
        w_ikw = jnp.concatenate(
            [w_ik, w_iw, jnp.zeros((d, LANES - IDX_DIM - IDX_HEADS), F32)], axis=1)
        wm_p = jnp.concatenate([w_xr, w_gr, w_ga, w_gb, w_k, w_ikw], axis=1).astype(BF16)
        wt_p = jnp.concatenate([w_q, w_iq, w_k, w_v, w_ik, w_iw], axis=1).T.astype(BF16)
        gw = 2 * LANES
        wa_bd = _block_diag_groups(w_rg_a[l], gw).astype(BF16)
        wx_bd = _block_diag_groups(w_rg_x[l], gw).astype(BF16)
        lru_consts = (conv_w[l], row2(conv_b[l]), wa_bd, wx_bd, row2(b_rg_a[l]), row2(b_rg_x[l]),
                      row2(lru_lambda[l]))
        post_w = (w_r, w_a, w_oo, row2(norm_ffn[l]), w_fg, w_fu, w_fd, row2(norm_ple[l]), w_pg, w_pp,
                  row2(norm_final))
        final = l == depth - 1

        (xr, gr, ga, gb, kb, ikwb, qT, iqT, vTb, iwT, *caches) = _inproj_p(
            xp, row2(norm_mix[l]), wm_p, wt_p, caches, tm=256, nb=nb, l=l)
        ya, h_last, new_buf = _rglru_p(xr.reshape(nb, t, d), gr.reshape(nb, t, d), *lru_consts, tt=128)
        o = _attn_p(qT, iqT, iwT, kb, vTb, ikwb, dtab_p, nb, t)
        xp = _post(xp, ya.reshape(nb * t, d), o, ga, gb, p_rows, post_w, tm=256, final=final, l=l,
                   p_row0=l * nb * t)
        outs_p.append((h_last, new_buf))

        (xr, gr, ga, gb, k, v, ikw, q, iq) = _inproj_s(xs, row2(norm_mix[l]), wm_p, wt_p, tm=st * sb)
        tmaj = lambda a2: a2.reshape(st, sb, -1)
        ya, h_last, new_buf = _rglru_s(tmaj(xr), tmaj(gr), state_conv[l].transpose(1, 0, 2),
                                       state_rglru_h[l], *lru_consts)
        bmaj = lambda a2: a2.reshape(st, sb, -1).transpose(1, 0, 2)
        k_b, v_b, ikw_b, q_b, iq_b = bmaj(k), bmaj(v), bmaj(ikw), bmaj(q), bmaj(iq)
        iq_rows = iq_b.reshape(sb, st * IDX_HEADS, IDX_DIM)
        w_col = ikw_b[:, :, IDX_DIM:IDX_DIM + IDX_HEADS].reshape(sb, st * IDX_HEADS, 1)
        mask = _select_s(page_table, iq_rows, w_col, ikw_b[:, :, :IDX_DIM].transpose(0, 2, 1),
                         ik_pool, l * n_pool, gs=min(sb, 32), k_top=k_top_s)
        o = _attn_s(page_table, q_b, k_b, v_b, mask, dtab_s, k_pool, v_pool, l * n_pool)
        o = o.transpose(1, 0, 2).reshape(st * sb, ATT_W)
        xs = _post(xs, ya.reshape(st * sb, d), o, ga, gb,
                   p_sample[l].transpose(1, 0, 2).reshape(st * sb, -1), post_w,
                   tm=min(256, st * sb), final=final, l=l)
        outs_s.append((k_b.reshape(sb, st, N_HEADS, HEAD_DIM), v_b.reshape(sb, st, N_HEADS, HEAD_DIM),
                       ikw_b[:, :, :IDX_DIM], h_last, new_buf.transpose(1, 0, 2)))

    y_prompt = xp.reshape(nb, t, d)
    y_sample = xs.reshape(st, sb, d).transpose(1, 0, 2)
    stack = lambda outs, i: jnp.stack([o_[i] for o_ in outs])
    kT_all, vT_all, ikT_all = caches
    heads_last = lambda aT: aT.reshape(depth, nb, N_HEADS, HEAD_DIM, t).transpose(0, 1, 4, 2, 3)
    return (y_prompt, y_sample,
            heads_last(kT_all), heads_last(vT_all), ikT_all.transpose(0, 1, 3, 2),
            stack(outs_p, 0), stack(outs_p, 1),
            stack(outs_s, 0), stack(outs_s, 1), stack(outs_s, 2), stack(outs_s, 3), stack(outs_s, 4))
```

```python
import functools
import math

import jax
import jax.numpy as jnp
from jax import lax
from jax.experimental import pallas as pl
from jax.experimental.pallas import tpu as pltpu

F32 = jnp.float32
BF16 = jnp.bfloat16
I32 = jnp.int32

N_HEADS = 8
HEAD_DIM = 64
ATT_W = N_HEADS * HEAD_DIM
IDX_HEADS = 8
IDX_DIM = 64
TOPK_MAX = 256
LRU_C = 8.0
MAX_DISTANCE = 128
EPS = 1e-6

LANES = 128
SUBLANES = 8
TQ = 128
TK = 256
NEG = -1e30
LOG2E = math.log2(math.e)
INT_MIN = -2147483648
VMEM_LIMIT = 56 * 1024 * 1024


def _dot(a, b):
    return jnp.dot(a, b, preferred_element_type=F32)


def _dot_nt(a, b):
    return lax.dot_general(a, b, (((1,), (1,)), ((), ())), preferred_element_type=F32)


def _sigmoid(x):
    return 0.5 * jnp.tanh(0.5 * x) + 0.5


def _gelu_tanh(x):
    c = math.sqrt(2.0 / math.pi)
    return 0.5 * x * (1.0 + jnp.tanh(c * (x + 0.044715 * (x * x * x))))


def _rms(x, g):
    y = x * lax.rsqrt(jnp.mean(x * x, axis=-1, keepdims=True) + EPS)
    return y * g


def _params(n_axes=1):
    return pltpu.CompilerParams(dimension_semantics=("arbitrary",) * n_axes,
                                vmem_limit_bytes=VMEM_LIMIT)


def _fori_pairs(lo, hi, body, carry):
    n = hi - lo

    def pair(i, c):
        j = lo + 2 * i
        return body(j + 1, body(j, c))

    carry = lax.fori_loop(0, n >> 1, pair, carry)
    return lax.cond((n & 1) == 1, lambda c: body(hi - 1, c), lambda c: c, carry)


def _const_spec(shape):
    nd = len(shape)
    return pl.BlockSpec(shape, lambda *_: (0,) * nd, pipeline_mode=pl.Buffered(1))


def _whole_out_spec(shape):
    nd = len(shape)
    return pl.BlockSpec(shape, lambda *_: (0,) * nd)


def _inproj_p_kernel(x_ref, g_ref, wm_ref, wt_ref, kT_all_ref, vT_all_ref, ikT_all_ref,
                     xr_ref, gr_ref, ga_ref, gb_ref, kb_ref, ikwb_ref,
                     qT_ref, iqT_ref, vTb_ref, iwT_ref, kT_ref, vT_ref, ikT_ref):
    del kT_all_ref, vT_all_ref, ikT_all_ref
    d = x_ref.shape[1]
    hb = _rms(x_ref[...], g_ref[...]).astype(BF16)
    z = _dot(hb, wm_ref[...])
    xr_ref[...] = z[:, 0:d]
    gr_ref[...] = z[:, d:2 * d]
    ga_ref[...] = z[:, 2 * d:3 * d]
    gb_ref[...] = z[:, 3 * d:4 * d]
    o = 4 * d
    kb_ref[...] = z[:, o:o + ATT_W].astype(BF16)
    ikwb_ref[...] = z[:, o + ATT_W:o + ATT_W + LANES].astype(BF16)
    zt = _dot_nt(wt_ref[...], hb)
    qT_ref[...] = (zt[0:ATT_W] * (HEAD_DIM ** -0.5)).astype(BF16)
    iqT_ref[...] = zt[ATT_W:2 * ATT_W].astype(BF16)
    kT_ref[...] = zt[2 * ATT_W:3 * ATT_W]
    vt = zt[3 * ATT_W:4 * ATT_W]
    vT_ref[...] = vt
    for c in range(vTb_ref.shape[0]):
        vTb_ref[c] = vt[:, c * TK:(c + 1) * TK].astype(BF16)
    ikT_ref[...] = zt[4 * ATT_W:4 * ATT_W + IDX_DIM]
    iwT_ref[...] = zt[4 * ATT_W + IDX_DIM:4 * ATT_W + IDX_DIM + IDX_HEADS]


def _inproj_p(x, g, wm, wt, caches, tm, nb, l):
    n, d = x.shape
    t = n // nb
    per_b = t // tm
    row = lambda w: pl.BlockSpec((tm, w), lambda i: (i, 0))
    out_shape = (
        jax.ShapeDtypeStruct((n, d), F32), jax.ShapeDtypeStruct((n, d), F32),
        jax.ShapeDtypeStruct((n, d), F32), jax.ShapeDtypeStruct((n, d), F32),
        jax.ShapeDtypeStruct((n, ATT_W), BF16), jax.ShapeDtypeStruct((n, LANES), BF16),
        jax.ShapeDtypeStruct((ATT_W, n), BF16), jax.ShapeDtypeStruct((ATT_W, n), BF16),
        jax.ShapeDtypeStruct((n // TK, ATT_W, TK), BF16),
        jax.ShapeDtypeStruct((IDX_HEADS, n), F32),
    ) + tuple(jax.ShapeDtypeStruct(c.shape, c.dtype) for c in caches)
    colT = lambda r: pl.BlockSpec((r, tm), lambda i: (0, i))
    layer_seq = lambda r: pl.BlockSpec((None, None, r, tm), lambda i: (l, i // per_b, 0, i % per_b))
    out_specs = (row(d), row(d), row(d), row(d), row(ATT_W), row(LANES), colT(ATT_W), colT(ATT_W),
                 pl.BlockSpec((tm // TK, ATT_W, TK), lambda i: (i, 0, 0)), colT(IDX_HEADS),
                 layer_seq(ATT_W), layer_seq(ATT_W), layer_seq(IDX_DIM))
    n_plain_in, n_plain_out = 4, 10
    return pl.pallas_call(
        _inproj_p_kernel, grid=(n // tm,),
        in_specs=[row(d), _const_spec(g.shape), _const_spec(wm.shape), _const_spec(wt.shape)]
                 + [pl.BlockSpec(memory_space=pl.ANY)] * len(caches),
        out_specs=out_specs, out_shape=out_shape,
        input_output_aliases={n_plain_in + i: n_plain_out + i for i in range(len(caches))},
        compiler_params=_params(1), name="inproj_prompt")(x, g, wm, wt, *caches)


def _inproj_s_kernel(x_ref, g_ref, wm_ref, wt_ref,
                     xr_ref, gr_ref, ga_ref, gb_ref, k_ref, v_ref, ikw_ref, q_ref, iq_ref):
    d = x_ref.shape[1]
    hb = _rms(x_ref[...], g_ref[...]).astype(BF16)
    z = _dot(hb, wm_ref[...])
    xr_ref[...] = z[:, 0:d]
    gr_ref[...] = z[:, d:2 * d]
    ga_ref[...] = z[:, 2 * d:3 * d]
    gb_ref[...] = z[:, 3 * d:4 * d]
    o = 4 * d
    k_ref[...] = z[:, o:o + ATT_W]
    ikw_ref[...] = z[:, o + ATT_W:o + ATT_W + LANES]
    qiq = _dot_nt(hb, wt_ref[0:2 * ATT_W, :])
    q_ref[...] = (qiq[:, 0:ATT_W] * (HEAD_DIM ** -0.5)).astype(BF16)
    iq_ref[...] = qiq[:, ATT_W:2 * ATT_W].astype(BF16)
    v_ref[...] = _dot_nt(hb, wt_ref[3 * ATT_W:4 * ATT_W, :])


def _inproj_s(x, g, wm, wt, tm):
    n, d = x.shape
    row = lambda w: pl.BlockSpec((tm, w), lambda i: (i, 0))
    out_shape = (
        jax.ShapeDtypeStruct((n, d), F32), jax.ShapeDtypeStruct((n, d), F32),
        jax.ShapeDtypeStruct((n, d), F32), jax.ShapeDtypeStruct((n, d), F32),
        jax.ShapeDtypeStruct((n, ATT_W), F32), jax.ShapeDtypeStruct((n, ATT_W), F32),
        jax.ShapeDtypeStruct((n, LANES), F32),
        jax.ShapeDtypeStruct((n, ATT_W), BF16), jax.ShapeDtypeStruct((n, ATT_W), BF16),
    )
    out_specs = (row(d), row(d), row(d), row(d), row(ATT_W), row(ATT_W), row(LANES),
                 row(ATT_W), row(ATT_W))
    return pl.pallas_call(
        _inproj_s_kernel, grid=(n // tm,),
        in_specs=[row(d), _const_spec(g.shape), _const_spec(wm.shape), _const_spec(wt.shape)],
        out_specs=out_specs, out_shape=out_shape, compiler_params=_params(1),
        name="inproj_sample")(x, g, wm, wt)


def _lru_gates(xc, wa_ref, wx_ref, ba, bx, lam):
    xcb = xc.astype(BF16)
    gw = wa_ref.shape[1]
    r_parts, i_parts = [], []
    for g in range(wa_ref.shape[0]):
        xs = xcb[:, g * gw:(g + 1) * gw]
        r_parts.append(_dot(xs, wa_ref[g]))
        i_parts.append(_dot(xs, wx_ref[g]))
    r = _sigmoid(jnp.concatenate(r_parts, axis=1) + ba)
    i = _sigmoid(jnp.concatenate(i_parts, axis=1) + bx)
    log_sig_lam = jnp.minimum(lam, 0.0) - jnp.log(1.0 + jnp.exp(-jnp.abs(lam)))
    log_a = LRU_C * r * log_sig_lam
    a = jnp.exp(log_a)
    u = jnp.sqrt(1.0 - a * a) * (i * xc)
    return a, u


def _rglru_p_kernel(xr_ref, gr_ref, cw_ref, cb_ref, wa_ref, wx_ref, ba_ref, bx_ref, lam_ref,
                    ya_ref, hlast_ref, buf_ref,
                    xx_ref, a_ref, u_ref, h_ref):
    step = pl.program_id(0)
    nb, tt, d = xr_ref.shape
    cw = cw_ref.shape[0]

    @pl.when(step == 0)
    def _():
        xx_ref[:, 0:8, :] = jnp.zeros((nb, 8, d), F32)
        h_ref[...] = jnp.zeros_like(h_ref)

    for b in range(nb):
        x = xr_ref[b]
        xx_ref[b, 8:8 + tt, :] = x
        xc = cb_ref[...] + x * cw_ref[cw - 1:cw, :]
        for j in range(cw - 1):
            sh = cw - 1 - j
            xc = xc + xx_ref[b, 8 - sh:8 - sh + tt, :] * cw_ref[j:j + 1, :]
        xx_ref[b, 0:8, :] = x[tt - 8:tt, :]
        a, u = _lru_gates(xc, wa_ref, wx_ref, ba_ref[...], bx_ref[...], lam_ref[...])
        a_ref[b] = a
        u_ref[b] = u

    def scan_body(t, hs):
        new = []
        for b in range(nb):
            h = a_ref[b, pl.ds(t, 1), :] * hs[b] + u_ref[b, pl.ds(t, 1), :]
            u_ref[b, pl.ds(t, 1), :] = h
            new.append(h)
        return tuple(new)

    hs = lax.fori_loop(0, tt, scan_body, tuple(h_ref[b:b + 1, :] for b in range(nb)), unroll=8)
    for b in range(nb):
        h_ref[b:b + 1, :] = hs[b]
        ya_ref[b] = (u_ref[b] * _gelu_tanh(gr_ref[b])).astype(BF16)
        buf_ref[b] = xr_ref[b, tt - (cw - 1):tt, :]
    hlast_ref[...] = h_ref[...]


def _rglru_p(xr, gr, cw, cb, wa, wx, ba, bx, lam, tt):
    nb, t, d = xr.shape
    blk = pl.BlockSpec((nb, tt, d), lambda i: (0, i, 0))
    consts = [cw, cb, wa, wx, ba, bx, lam]
    return pl.pallas_call(
        _rglru_p_kernel, grid=(t // tt,),
        in_specs=[blk, blk] + [_const_spec(c.shape) for c in consts],
        out_specs=(blk, _whole_out_spec((nb, d)), _whole_out_spec((nb, cw.shape[0] - 1, d))),
        out_shape=(jax.ShapeDtypeStruct((nb, t, d), BF16), jax.ShapeDtypeStruct((nb, d), F32),
                   jax.ShapeDtypeStruct((nb, cw.shape[0] - 1, d), F32)),
        scratch_shapes=[pltpu.VMEM((nb, tt + 8, d), F32), pltpu.VMEM((nb, tt, d), F32),
                        pltpu.VMEM((nb, tt, d), F32), pltpu.VMEM((nb, d), F32)],
        compiler_params=_params(1), name="rglru_prompt")(xr, gr, *consts)


def _rglru_s_kernel(xr_ref, gr_ref, st_ref, h0_ref, cw_ref, cb_ref, wa_ref, wx_ref, ba_ref,
                    bx_ref, lam_ref, ya_ref, hlast_ref, buf_ref):
    t_len, nb, d = xr_ref.shape
    cw = cw_ref.shape[0]
    rows = [st_ref[j] for j in range(cw - 1)] + [xr_ref[t] for t in range(t_len)]
    h = h0_ref[...]
    for t in range(t_len):
        xc = cb_ref[...] + rows[t + cw - 1] * cw_ref[cw - 1:cw, :]
        for j in range(cw - 1):
            xc = xc + rows[t + j] * cw_ref[j:j + 1, :]
        a, u = _lru_gates(xc, wa_ref, wx_ref, ba_ref[...], bx_ref[...], lam_ref[...])
        h = a * h + u
        ya_ref[t] = (h * _gelu_tanh(gr_ref[t])).astype(BF16)
    hlast_ref[...] = h
    for j in range(cw - 1):
        buf_ref[j] = rows[t_len + j]


def _rglru_s(xr, gr, st, h0, cw, cb, wa, wx, ba, bx, lam):
    t_len, nb, d = xr.shape
    args = [xr, gr, st, h0, cw, cb, wa, wx, ba, bx, lam]
    return pl.pallas_call(
        _rglru_s_kernel, grid=(1,),
        in_specs=[_const_spec(a.shape) for a in args],
        out_specs=(_whole_out_spec((t_len, nb, d)), _whole_out_spec((nb, d)),
                   _whole_out_spec((cw.shape[0] - 1, nb, d))),
        out_shape=(jax.ShapeDtypeStruct((t_len, nb, d), BF16), jax.ShapeDtypeStruct((nb, d), F32),
                   jax.ShapeDtypeStruct((cw.shape[0] - 1, nb, d), F32)),
        compiler_params=_params(1), name="rglru_sample")(*args)


def _sortable_key(score):
    bits = pltpu.bitcast(score, I32)
    return bits ^ ((bits >> 31) & 0x7FFFFFFF)


GROUP = 16
QT = 2


def _sort_network(n):
    pairs = []
    p = 1
    while p < n:
        k = p
        while k >= 1:
            for j in range(k % p, n - k, 2 * k):
                for i in range(min(k, n - j - k)):
                    if (i + j) // (2 * p) == (i + j + k) // (2 * p):
                        pairs.append((i + j, i + j + k))
            k //= 2
        p *= 2
    return pairs


def _count_in_sorted_group(v, cmp):
    one = lambda m, w: jnp.where(m, w, 0)
    pick = jnp.where
    m1 = cmp(v(7))
    m2 = cmp(pick(m1, v(11), v(3)))
    m3 = cmp(pick(m2, pick(m1, v(13), v(5)), pick(m1, v(9), v(1))))
    if_m3 = pick(m2, pick(m1, v(14), v(6)), pick(m1, v(10), v(2)))
    if_not = pick(m2, pick(m1, v(12), v(4)), pick(m1, v(8), v(0)))
    m4 = cmp(pick(m3, if_m3, if_not))
    m5 = cmp(v(15))
    return one(m1, 8) + one(m2, 4) + one(m3, 2) + one(m4, 1) + one(m5, 1)


def _attn_p_kernel(qT_ref, iqT_ref, iwT_ref, kb_ref, vT_ref, ikw_ref, dtab_ref, o_ref,
                   keys_ref, sorted_ref, rhs_ref, qbd_ref, m_ref, acc_ref, x_ref, *, k_top):
    qp = pl.program_id(1)
    n_chunks = qp + 1
    half = LANES // 2
    zeros_half = jnp.zeros((half, TQ), BF16)
    s_iota = lax.broadcasted_iota(I32, (TK, TQ), 0)
    t_iota = lax.broadcasted_iota(I32, (TK, TQ), 1)
    w_all = (iwT_ref[...] * (IDX_HEADS ** -0.5)) * (IDX_DIM ** -0.5)

    def tile_consts(qi):
        qb = QT * qp + qi
        lanes = slice(qi * TQ, (qi + 1) * TQ)
        n_far = jnp.maximum(qb - 1, 0) >> 1
        return qb * TQ, lanes, n_far

    def score_body(j, carry, band, qi, t0, lanes, w):
        off = pl.multiple_of(j * TK, TK)
        ikc = ikw_ref[pl.ds(off, TK), :]
        score = jnp.zeros((TK, TQ), F32)
        for c in range(N_HEADS // 2):
            dd = _dot(ikc, rhs_ref[qi, c])
            for hh in range(2):
                h = 2 * c + hh
                score = score + jnp.maximum(dd[:, hh * TQ:(hh + 1) * TQ], 0.0) * w[h:h + 1, :]
        key = _sortable_key(score)
        if band:
            key = jnp.where(s_iota + off <= t0 + t_iota, key, INT_MIN)
        keys_ref[pl.ds(off, TK), lanes] = key
        per = TK // GROUP
        v = [key[i * per:(i + 1) * per, :] for i in range(GROUP)]
        for a, b in _sort_network(GROUP):
            v[a], v[b] = jnp.maximum(v[a], v[b]), jnp.minimum(v[a], v[b])
        for i in range(GROUP):
            sorted_ref[j, i, :, lanes] = v[i]
        return carry

    for qi in range(QT):
        t0, lanes, n_far = tile_consts(qi)
        for c in range(N_HEADS // 2):
            for hh in range(2):
                h = 2 * c + hh
                iq_h = iqT_ref[h * IDX_DIM:(h + 1) * IDX_DIM, lanes]
                rhs_ref[qi, c, :, hh * TQ:(hh + 1) * TQ] = jnp.concatenate([iq_h, zeros_half], axis=0)
                q_h = qT_ref[h * HEAD_DIM:(h + 1) * HEAD_DIM, lanes]
                parts = [zeros_half, q_h] if hh else [q_h, zeros_half]
                qbd_ref[qi, c, :, hh * TQ:(hh + 1) * TQ] = jnp.concatenate(parts, axis=0)

    def score_tiles(j, carry, band):
        for qi in range(QT):
            t0, lanes, _ = tile_consts(qi)
            score_body(j, carry, band, qi, t0, lanes, w_all[:, lanes])
        return carry

    n_far_all = tile_consts(0)[2]
    _fori_pairs(0, n_far_all, functools.partial(score_tiles, band=False), 0)
    _fori_pairs(n_far_all, n_chunks, functools.partial(score_tiles, band=True), 0)

    @pl.when((n_chunks & 1) == 1)
    def _():
        sorted_ref[n_chunks] = jnp.full(sorted_ref.shape[1:], INT_MIN, I32)

    def count(cmp):
        def body(i, acc):
            for u in range(2):
                acc = acc + _count_in_sorted_group(lambda r: sorted_ref[2 * i + u, r], cmp)
            return acc
        acc = lax.fori_loop(0, (n_chunks + 1) >> 1, body, jnp.zeros((TK // GROUP, QT * TQ), I32))
        return acc.sum(axis=0, keepdims=True)

    def bit_body(p, carry):
        thr, n_ge = carry
        bit = jnp.left_shift(jnp.int32(1), 31 - p)
        cand = thr ^ bit
        cnt = count(lambda kc: kc >= cand)
        ok = cnt >= k_top
        return jnp.where(ok, cand, thr), jnp.where(ok, cnt, n_ge)

    thr_all, n_ge = lax.fori_loop(0, 32, bit_body, (jnp.full((1, QT * TQ), INT_MIN, I32),
                                                    jnp.zeros((1, QT * TQ), I32)))
    any_ties = jnp.max(jnp.where((thr_all != INT_MIN) & (n_ge != k_top), 1, 0))

    fold = lambda a, op: op(a.reshape(TK // SUBLANES, SUBLANES, TQ), axis=0)

    def logits_body(j, carry, band, ties, qi, t0, lanes, thr, need=None, ltri=None):
        taken, mx = carry
        off = pl.multiple_of(j * TK, TK)
        keyc = keys_ref[pl.ds(off, TK), lanes]
        if ties:
            eq = keyc == thr
            prefix = _dot(ltri, eq.astype(BF16))
            sel = (keyc > thr) | (eq & (prefix + taken <= need))
            taken = taken + prefix[TK - 1:TK, :]
        else:
            sel = keyc >= thr
        if band:
            sel = sel & (s_iota + off <= t0 + t_iota)
            tab = (t0 - off) // TQ
        kc = kb_ref[pl.ds(off, TK), :]
        new_mx = []
        for c in range(N_HEADS // 2):
            lg = _dot(kc[:, c * LANES:(c + 1) * LANES], qbd_ref[qi, c])
            for hh in range(2):
                h = 2 * c + hh
                x = lg[:, hh * TQ:(hh + 1) * TQ]
                if band:
                    x = x + dtab_ref[tab, h]
                x = jnp.where(sel, x * LOG2E, NEG)
                x_ref[h, pl.ds(off, TK), :] = x
                new_mx.append(jnp.maximum(mx[h * SUBLANES:(h + 1) * SUBLANES, :], fold(x, jnp.max)))
        return taken, jnp.concatenate(new_mx, axis=0)

    def all_logits(ties, qi, t0, lanes, n_far):
        thr = thr_all[:, lanes]
        extra = {}
        if ties:
            n_gt = count(lambda kc: kc > thr_all)
            extra["need"] = (k_top - n_gt[:, lanes]).astype(F32)
            extra["ltri"] = (lax.broadcasted_iota(I32, (TK, TK), 1)
                             <= lax.broadcasted_iota(I32, (TK, TK), 0)).astype(BF16)
        body = functools.partial(logits_body, ties=ties, qi=qi, t0=t0, lanes=lanes, thr=thr, **extra)
        carry = (jnp.zeros((1, TQ), F32), jnp.full((N_HEADS * SUBLANES, TQ), NEG, F32))
        carry = _fori_pairs(0, n_far, functools.partial(body, band=False), carry)
        return _fori_pairs(n_far, n_chunks, functools.partial(body, band=True), carry)[1]

    def pv_body(j, lsum):
        off = pl.multiple_of(j * TK, TK)
        new_lsum = []
        for h in range(N_HEADS):
            p = jnp.exp2(x_ref[h, pl.ds(off, TK), :] - m_ref[h:h + 1, :])
            new_lsum.append(lsum[h * SUBLANES:(h + 1) * SUBLANES, :] + fold(p, jnp.sum))
            rows = slice(h * HEAD_DIM, (h + 1) * HEAD_DIM)
            acc_ref[rows, :] += _dot(vT_ref[j, rows, :], p.astype(BF16))
        return jnp.concatenate(new_lsum, axis=0)

    zero_sums = jnp.zeros((N_HEADS * SUBLANES, TQ), F32)

    def start_pv(mx):
        m_ref[...] = jnp.concatenate(
            [mx[h * SUBLANES:(h + 1) * SUBLANES, :].max(axis=0, keepdims=True)
             for h in range(N_HEADS)], axis=0)
        acc_ref[...] = jnp.zeros(acc_ref.shape, F32)

    def finish_tile(qi, lsum):
        for h in range(N_HEADS):
            rows = slice(h * HEAD_DIM, (h + 1) * HEAD_DIM)
            inv = 1.0 / lsum[h * SUBLANES:(h + 1) * SUBLANES, :].sum(axis=0, keepdims=True)
            acc_ref[rows, :] = acc_ref[rows, :] * inv
        o_ref[qi * TQ:(qi + 1) * TQ, :] = acc_ref[...].T.astype(BF16)

    def tiles_in_turn():
        for qi in range(QT):
            t0, lanes, n_far = tile_consts(qi)
            start_pv(all_logits(True, qi, t0, lanes, n_far))
            finish_tile(qi, _fori_pairs(0, n_chunks, pv_body, zero_sums))
        return 0

    def tiles_overlapped():
        assert QT == 2
        t0_a, lanes_a, n_far_a = tile_consts(0)
        t0_b, lanes_b, n_far_b = tile_consts(1)
        start_pv(all_logits(False, 0, t0_a, lanes_a, n_far_a))
        logits_b = functools.partial(logits_body, ties=False, qi=1, t0=t0_b, lanes=lanes_b,
                                     thr=thr_all[:, lanes_b])

        def both(j, carry, band):
            lsum_a, state_b = carry
            lsum_a = pv_body(j, lsum_a)
            return lsum_a, logits_b(j, state_b, band=band)

        carry = (zero_sums, (jnp.zeros((1, TQ), F32), jnp.full((N_HEADS * SUBLANES, TQ), NEG, F32)))
        carry = _fori_pairs(0, n_far_b, functools.partial(both, band=False), carry)
        lsum_a, (_, mx_b) = _fori_pairs(n_far_b, n_chunks, functools.partial(both, band=True), carry)
        finish_tile(0, lsum_a)
        start_pv(mx_b)
        finish_tile(1, _fori_pairs(0, n_chunks, pv_body, zero_sums))
        return 0

    lax.cond(any_ties > 0, tiles_in_turn, tiles_overlapped)


def _attn_p(qT, iqT, iwT, kb, vT, ikwb, dtab, nb, t):
    n = nb * t
    nq = t // (QT * TQ)
    k_top = min(TOPK_MAX, t // 4)
    colT = lambda r: pl.BlockSpec((r, QT * TQ), lambda b, q: (0, b * nq + q))
    return pl.pallas_call(
        functools.partial(_attn_p_kernel, k_top=k_top), grid=(nb, nq),
        in_specs=[colT(ATT_W), colT(ATT_W), colT(IDX_HEADS),
                  pl.BlockSpec((t, ATT_W), lambda b, q: (b, 0)),
                  pl.BlockSpec((t // TK, ATT_W, TK), lambda b, q: (b, 0, 0)),
                  pl.BlockSpec((t, LANES), lambda b, q: (b, 0)),
                  _const_spec(dtab.shape)],
        out_specs=pl.BlockSpec((QT * TQ, ATT_W), lambda b, q: (b * nq + q, 0)),
        out_shape=jax.ShapeDtypeStruct((n, ATT_W), BF16),
        scratch_shapes=[pltpu.VMEM((t, QT * TQ), I32),
                        pltpu.VMEM((t // TK + 1, GROUP, TK // GROUP, QT * TQ), I32),
                        pltpu.VMEM((QT, N_HEADS // 2, LANES, 2 * TQ), BF16),
                        pltpu.VMEM((QT, N_HEADS // 2, LANES, 2 * TQ), BF16),
                        pltpu.VMEM((N_HEADS, TQ), F32), pltpu.VMEM((ATT_W, TQ), F32),
                        pltpu.VMEM((N_HEADS, t, TQ), F32)],
        compiler_params=_params(2), name="attn_prompt")(qT, iqT, iwT, kb, vT, ikwb, dtab)


def _select_s_kernel(pt_ref, iq_ref, wcol_ref, iknew_ref, pool_ref, mask_ref,
                     ikbuf_ref, score_ref, sem, *, k_top, layer_off, t_len):
    g = pl.program_id(0)
    gs, _, s_pad = ikbuf_ref.shape
    n_pages = pt_ref.shape[1]
    page = pool_ref.shape[2]
    past = n_pages * page

    def page_copy(i, j):
        return pltpu.make_async_copy(pool_ref.at[layer_off + pt_ref[g * gs + i, j]],
                                     ikbuf_ref.at[i, :, j * page:(j + 1) * page], sem)

    def start_body(i, c):
        for j in range(n_pages):
            page_copy(i, j).start()
        return c

    lax.fori_loop(0, gs, start_body, 0)
    ikbuf_ref[:, :, past:s_pad] = jnp.zeros((gs, IDX_DIM, s_pad - past), F32)
    ikbuf_ref[:, :, past:past + t_len] = iknew_ref[...]

    def wait_body(i, c):
        for j in range(n_pages):
            page_copy(i, j).wait()
        return c

    lax.fori_loop(0, gs, wait_body, 0)

    tile = score_ref.shape[1]
    per_tile = tile // t_len

    def score_body(it, c):
        for u in range(per_tile):
            i = it * per_tile + u
            dd = _dot(iq_ref[i], ikbuf_ref[i].astype(BF16))
            wv = (wcol_ref[i] * (IDX_HEADS ** -0.5)) * (IDX_DIM ** -0.5)
            sc = (jnp.maximum(dd, 0.0) * wv).reshape(t_len, IDX_HEADS, s_pad).sum(axis=1)
            score_ref[it, u * t_len:(u + 1) * t_len, :] = sc
        return c

    lax.fori_loop(0, gs // per_tile, score_body, 0)

    rows = gs * t_len
    s_idx = lax.broadcasted_iota(I32, (rows, s_pad), 1)
    t_idx = lax.broadcasted_iota(I32, (rows, s_pad), 0) % t_len
    valid = s_idx <= past + t_idx
    keys = jnp.where(valid, _sortable_key(score_ref[...].reshape(rows, s_pad)), INT_MIN)

    def bit_body(p, thr):
        bit = jnp.left_shift(jnp.int32(1), 31 - p)
        cand = thr ^ bit
        cnt = jnp.sum((keys >= cand).astype(I32), axis=1, keepdims=True)
        return jnp.where(cnt >= k_top, cand, thr)

    thr = lax.fori_loop(0, 32, bit_body, jnp.full((rows, 1), INT_MIN, I32))
    gt = keys > thr
    eq = keys == thr
    need = (k_top - jnp.sum(gt.astype(I32), axis=1, keepdims=True)).astype(F32)
    utri = (lax.broadcasted_iota(I32, (LANES, LANES), 0) <= lax.broadcasted_iota(I32, (LANES, LANES), 1)
            ).astype(BF16)
    eqb = eq.astype(BF16)
    carry = jnp.zeros((rows, 1), F32)
    for c in range(s_pad // LANES):
        cols = slice(c * LANES, (c + 1) * LANES)
        prefix = _dot(eqb[:, cols], utri) + carry
        take = gt[:, cols] | (eq[:, cols] & (prefix <= need))
        take = take & valid[:, cols]
        m = jnp.where(take, 0.0, NEG).reshape(gs // per_tile, tile, LANES)
        for u in range(per_tile):
            mask_ref[:, u, :, cols] = m[:, u * t_len:(u + 1) * t_len, :]
        carry = prefix[:, LANES - 1:LANES]


def _select_s(page_table, iq, wcol, iknew, pool, layer, gs, k_top):
    nb, rows_q, _ = iq.shape
    t_len = rows_q // IDX_HEADS
    n_pages = page_table.shape[1]
    page = pool.shape[2]
    s_pad = n_pages * page + LANES
    assert SUBLANES % t_len == 0
    per_tile = SUBLANES // t_len
    assert nb % gs == 0 and gs % per_tile == 0
    kern = functools.partial(_select_s_kernel, k_top=k_top, layer_off=layer, t_len=t_len)
    grid_spec = pltpu.PrefetchScalarGridSpec(
        num_scalar_prefetch=1, grid=(nb // gs,),
        in_specs=[pl.BlockSpec((gs, rows_q, IDX_DIM), lambda g, pt: (g, 0, 0)),
                  pl.BlockSpec((gs, rows_q, 1), lambda g, pt: (g, 0, 0)),
                  pl.BlockSpec((gs, IDX_DIM, t_len), lambda g, pt: (g, 0, 0)),
                  pl.BlockSpec(memory_space=pl.ANY)],
        out_specs=pl.BlockSpec((gs // per_tile, per_tile, t_len, s_pad), lambda g, pt: (g, 0, 0, 0)),
        scratch_shapes=[pltpu.VMEM((gs, IDX_DIM, s_pad), F32),
                        pltpu.VMEM((gs // per_tile, SUBLANES, s_pad), F32),
                        pltpu.SemaphoreType.DMA(())])
    return pl.pallas_call(
        kern, grid_spec=grid_spec,
        out_shape=jax.ShapeDtypeStruct((nb // per_tile, per_tile, t_len, s_pad), F32),
        compiler_params=_params(1), name="select_sample")(page_table, iq, wcol, iknew, pool)


def _attn_s_kernel(pt_ref, q_ref, knew_ref, vnew_ref, mask_ref, dtab_ref, *rest, n_pages, t_len):
    k_pages = rest[:n_pages]
    v_pages = rest[n_pages:2 * n_pages]
    o_ref, kpad_ref, vpad_ref = rest[2 * n_pages:]
    rows = t_len * N_HEADS
    page = k_pages[0].shape[2]

    def flat_bf16(page_ref):
        return page_ref[...].reshape(ATT_W, page).astype(BF16)

    def per_head_rows(x):
        return jnp.concatenate(
            [jnp.broadcast_to(x[t:t + 1, :], (N_HEADS, x.shape[1])) for t in range(t_len)], axis=0)

    col_head = lax.broadcasted_iota(I32, (rows, ATT_W), 1) // HEAD_DIM
    row_head = lax.broadcasted_iota(I32, (rows, ATT_W), 0) % N_HEADS
    own = col_head == row_head
    qbd = jnp.where(own, per_head_rows(q_ref[...].astype(F32)), 0.0).astype(BF16)
    kpad_ref[...] = jnp.zeros(kpad_ref.shape, F32)
    vpad_ref[...] = jnp.zeros(vpad_ref.shape, F32)
    kpad_ref[0:t_len, :] = knew_ref[...]
    vpad_ref[0:t_len, :] = vnew_ref[...]
    knew = kpad_ref[...].astype(BF16)
    vnew = vpad_ref[...].astype(BF16)
    logits = [_dot(qbd, flat_bf16(kp)) for kp in k_pages] + [_dot_nt(qbd, knew)]
    x = jnp.concatenate(logits, axis=1)
    x = x + dtab_ref[...] + per_head_rows(mask_ref[...])
    m = x.max(axis=1, keepdims=True)
    p = jnp.exp(x - m)
    l = p.sum(axis=1, keepdims=True)
    pb = p.astype(BF16)
    out = _dot(pb[:, n_pages * page:], vnew)
    for j, vp in enumerate(v_pages):
        out = out + _dot_nt(pb[:, j * page:(j + 1) * page], flat_bf16(vp))
    out = out / l
    out = jnp.where(own, out, 0.0).reshape(t_len, N_HEADS, ATT_W).sum(axis=1)
    o_ref[...] = out.astype(BF16)


def _attn_s(page_table, q, knew, vnew, mask, dtab, k_pool, v_pool, layer_off):
    nb, t_len, _ = q.shape
    n_pages = page_table.shape[1]
    page = k_pool.shape[3]
    s_pad = n_pages * page + LANES

    def page_spec(j):
        return pl.BlockSpec((None, N_HEADS, HEAD_DIM, page),
                            lambda b, pt, j=j: (layer_off + pt[b, j], 0, 0, 0))

    seq = lambda w: pl.BlockSpec((None, t_len, w), lambda b, pt: (b, 0, 0))
    per_tile = mask.shape[1]
    mask_spec = pl.BlockSpec((None, None, t_len, s_pad),
                             lambda b, pt: (b // per_tile, b % per_tile, 0, 0))
    grid_spec = pltpu.PrefetchScalarGridSpec(
        num_scalar_prefetch=1, grid=(nb,),
        in_specs=[seq(ATT_W), seq(ATT_W), seq(ATT_W), mask_spec,
                  pl.BlockSpec(dtab.shape, lambda b, pt: (0, 0))]
                 + [page_spec(j) for j in range(n_pages)] * 2,
        out_specs=seq(ATT_W),
        scratch_shapes=[pltpu.VMEM((page, ATT_W), F32), pltpu.VMEM((page, ATT_W), F32)])
    kern = functools.partial(_attn_s_kernel, n_pages=n_pages, t_len=t_len)
    return pl.pallas_call(
        kern, grid_spec=grid_spec, out_shape=jax.ShapeDtypeStruct((nb, t_len, ATT_W), BF16),
        compiler_params=_params(1), name="attn_sample")(
            page_table, q, knew, vnew, mask, dtab, *([k_pool] * n_pages), *([v_pool] * n_pages))


def _post_kernel(x_ref, ya_ref, o_ref, ga_ref, gb_ref, p_ref, wr_ref, wa_ref, wo_ref,
                 nf_ref, wg_ref, wu_ref, wd_ref, np_ref, wpg_ref, wpp_ref, nfin_ref,
                 out_ref, *, final):
    a = _dot(ya_ref[...], wr_ref[...])
    b = _dot(o_ref[...], wa_ref[...])
    mix = _sigmoid(ga_ref[...]) * a + _sigmoid(gb_ref[...]) * b
    x = x_ref[...] + _dot(mix.astype(BF16), wo_ref[...])
    h2 = _rms(x, nf_ref[...]).astype(BF16)
    g = _dot(h2, wg_ref[...])
    u = _dot(h2, wu_ref[...])
    act = (g * _sigmoid(g)) * u
    x = x + _dot(act.astype(BF16), wd_ref[...])
    h3 = _rms(x, np_ref[...]).astype(BF16)
    x = x + _sigmoid(_dot(h3, wpg_ref[...])) * _dot(p_ref[...].astype(BF16), wpp_ref[...])
    if final:
        x = _rms(x, nfin_ref[...])
    out_ref[...] = x


def _layer_spec(w, l):
    tail = (0,) * (w.ndim - 1)
    return pl.BlockSpec((None,) + w.shape[1:], lambda *_: (l,) + tail,
                        pipeline_mode=pl.Buffered(1))


def _post(x, ya, o, ga, gb, p, weights, tm, final, l, p_row0=0):
    n, d = x.shape
    row = lambda w: pl.BlockSpec((tm, w), lambda i: (i, 0))
    p_blk0 = p_row0 // tm
    return pl.pallas_call(
        functools.partial(_post_kernel, final=final), grid=(n // tm,),
        in_specs=[row(d), row(d), row(ATT_W), row(d), row(d),
                  pl.BlockSpec((tm, p.shape[1]), lambda i: (i + p_blk0, 0))]
                 + [_layer_spec(c, l) if c.ndim == 3 else _const_spec(c.shape) for c in weights],
        out_specs=row(d), out_shape=jax.ShapeDtypeStruct((n, d), F32),
        compiler_params=_params(1), name="post")(x, ya, o, ga, gb, p, *weights)


def _rel_bucket(n, n_buckets):
    max_exact = n_buckets // 2
    nf = jnp.maximum(n, 1).astype(F32)
    large = max_exact + (jnp.log(nf / max_exact) / math.log(MAX_DISTANCE / max_exact)
                         * (n_buckets - max_exact)).astype(I32)
    large = jnp.minimum(large, n_buckets - 1)
    return jnp.where(n < max_exact, n, large)


def _bias_delta(rel_bias, n):
    nbk = rel_bias.shape[0]
    b = rel_bias.astype(F32)
    onehot = _rel_bucket(jnp.maximum(n, 0), nbk)[..., None] == jnp.arange(nbk, dtype=I32)
    picked = jnp.where(onehot[..., None], b, 0.0).sum(axis=-2)
    return picked - b[nbk - 1]


def _block_diag_groups(w, group):
    nbk, c, _ = w.shape
    per = group // c
    wg = w.reshape(nbk // per, per, c, c)
    eye = jnp.eye(per, dtype=w.dtype)
    return jnp.einsum('gpcd,pq->gpcqd', wg, eye).reshape(nbk // per, group, group)


def kernel(x_prompt, x_sample, p_prompt, p_sample, cache_k, cache_v, cache_idx_k, state_rglru_h,
           state_conv, page_table, rel_bias, norm_mix, w_in, conv_w, conv_b, w_rg_a, b_rg_a,
           w_rg_x, b_rg_x, lru_lambda, w_rnn_out, w_att_out, w_o, norm_ffn, w_ffn_gate, w_ffn_up,
           w_ffn_down, norm_ple, w_ple_gate, w_ple_proj, norm_final):
    nb, t, d = x_prompt.shape
    sb, st, _ = x_sample.shape
    depth = w_in.shape[0]
    n_pool, page = cache_k.shape[1], cache_k.shape[2]
    n_pages = page_table.shape[1]
    past = n_pages * page
    cw = conv_w.shape[1]
    s_pad = past + LANES
    k_top_s = min(TOPK_MAX, (past + st) // 4)
    row2 = lambda v: v.reshape(1, -1).astype(F32)

    idx3 = jnp.arange(3, dtype=I32)[:, None, None]
    s_rel = jnp.arange(TK, dtype=I32)[None, :, None]
    t_rel = jnp.arange(TQ, dtype=I32)[None, None, :]
    dtab_p = _bias_delta(rel_bias, idx3 * TQ + t_rel - s_rel).transpose(0, 3, 1, 2)
    s_all = jnp.arange(s_pad, dtype=I32)[None, :]
    q_pos = past + jnp.arange(st, dtype=I32)[:, None]
    dtab_s = _bias_delta(rel_bias, q_pos - s_all)
    dtab_s = dtab_s.transpose(0, 2, 1).reshape(st * N_HEADS, s_pad)

    k_pool = cache_k.transpose(0, 1, 3, 4, 2).reshape(depth * n_pool, N_HEADS, HEAD_DIM, page)
    v_pool = cache_v.transpose(0, 1, 3, 4, 2).reshape(depth * n_pool, N_HEADS, HEAD_DIM, page)
    ik_pool = cache_idx_k.transpose(0, 1, 3, 2).reshape(depth * n_pool, IDX_DIM, page)

    w_r, w_a, w_oo = w_rnn_out.astype(BF16), w_att_out.astype(BF16), w_o.astype(BF16)
    w_fg, w_fu, w_fd = w_ffn_gate.astype(BF16), w_ffn_up.astype(BF16), w_ffn_down.astype(BF16)
    w_pg, w_pp = w_ple_gate.astype(BF16), w_ple_proj.astype(BF16)
    p_rows = p_prompt.reshape(depth * nb * t, -1)

    xp = x_prompt.reshape(nb * t, d)
    xs = x_sample.transpose(1, 0, 2).reshape(st * sb, d)
    outs_p, outs_s = [], []
    caches = (jnp.zeros((depth, nb, ATT_W, t), F32), jnp.zeros((depth, nb, ATT_W, t), F32),
              jnp.zeros((depth, nb, IDX_DIM, t), F32))
    for l in range(depth):
        splits = [d, 2 * d, 2 * d + ATT_W, 2 * d + 2 * ATT_W, 2 * d + 3 * ATT_W,
                  2 * d + 3 * ATT_W + IDX_HEADS * IDX_DIM,
                  2 * d + 3 * ATT_W + IDX_HEADS * IDX_DIM + IDX_DIM,
                  2 * d + 3 * ATT_W + IDX_HEADS * IDX_DIM + IDX_DIM + IDX_HEADS,
                  3 * d + 3 * ATT_W + IDX_HEADS * IDX_DIM + IDX_DIM + IDX_HEADS]
        w_xr, w_gr, w_q, w_k, w_v, w_iq, w_ik, w_iw, w_ga, w_gb = jnp.split(w_in[l], splits, axis=1)
        w_ikw = jnp.concatenate(
            [w_ik, w_iw, jnp.zeros((d, LANES - IDX_DIM - IDX_HEADS), F32)], axis=1)
        wm_p = jnp.concatenate([w_xr, w_gr, w_ga, w_gb, w_k, w_ikw], axis=1).astype(BF16)
        wt_p = jnp.concatenate([w_q, w_iq, w_k, w_v, w_ik, w_iw], axis=1).T.astype(BF16)
        gw = 2 * LANES
        wa_bd = _block_diag_groups(w_rg_a[l], gw).astype(BF16)
        wx_bd = _block_diag_groups(w_rg_x[l], gw).astype(BF16)
        lru_consts = (conv_w[l], row2(conv_b[l]), wa_bd, wx_bd, row2(b_rg_a[l]), row2(b_rg_x[l]),
                      row2(lru_lambda[l]))
        post_w = (w_r, w_a, w_oo, row2(norm_ffn[l]), w_fg, w_fu, w_fd, row2(norm_ple[l]), w_pg, w_pp,
                  row2(norm_final))
        final = l == depth - 1

        (xr, gr, ga, gb, kb, ikwb, qT, iqT, vTb, iwT, *caches) = _inproj_p(
            xp, row2(norm_mix[l]), wm_p, wt_p, caches, tm=256, nb=nb, l=l)
        ya, h_last, new_buf = _rglru_p(xr.reshape(nb, t, d), gr.reshape(nb, t, d), *lru_consts, tt=128)
        o = _attn_p(qT, iqT, iwT, kb, vTb, ikwb, dtab_p, nb, t)
        xp = _post(xp, ya.reshape(nb * t, d), o, ga, gb, p_rows, post_w, tm=256, final=final, l=l,
                   p_row0=l * nb * t)
        outs_p.append((h_last, new_buf))

        (xr, gr, ga, gb, k, v, ikw, q, iq) = _inproj_s(xs, row2(norm_mix[l]), wm_p, wt_p, tm=st * sb)
        tmaj = lambda a2: a2.reshape(st, sb, -1)
        ya, h_last, new_buf = _rglru_s(tmaj(xr), tmaj(gr), state_conv[l].transpose(1, 0, 2),
                                       state_rglru_h[l], *lru_consts)
        bmaj = lambda a2: a2.reshape(st, sb, -1).transpose(1, 0, 2)
        k_b, v_b, ikw_b, q_b, iq_b = bmaj(k), bmaj(v), bmaj(ikw), bmaj(q), bmaj(iq)
        iq_rows = iq_b.reshape(sb, st * IDX_HEADS, IDX_DIM)
        w_col = ikw_b[:, :, IDX_DIM:IDX_DIM + IDX_HEADS].reshape(sb, st * IDX_HEADS, 1)
        mask = _select_s(page_table, iq_rows, w_col, ikw_b[:, :, :IDX_DIM].transpose(0, 2, 1),
                         ik_pool, l * n_pool, gs=min(sb, 32), k_top=k_top_s)
        o = _attn_s(page_table, q_b, k_b, v_b, mask, dtab_s, k_pool, v_pool, l * n_pool)
        o = o.transpose(1, 0, 2).reshape(st * sb, ATT_W)
        xs = _post(xs, ya.reshape(st * sb, d), o, ga, gb,
                   p_sample[l].transpose(1, 0, 2).reshape(st * sb, -1), post_w,
                   tm=min(256, st * sb), final=final, l=l)
        outs_s.append((k_b.reshape(sb, st, N_HEADS, HEAD_DIM), v_b.reshape(sb, st, N_HEADS, HEAD_DIM),
                       ikw_b[:, :, :IDX_DIM], h_last, new_buf.transpose(1, 0, 2)))

    y_prompt = xp.reshape(nb, t, d)
    y_sample = xs.reshape(st, sb, d).transpose(1, 0, 2)
    stack = lambda outs, i: jnp.stack([o_[i] for o_ in outs])
    kT_all, vT_all, ikT_all = caches
    heads_last = lambda aT: aT.reshape(depth, nb, N_HEADS, HEAD_DIM, t).transpose(0, 1, 4, 2, 3)
    return (y_prompt, y_sample,
            heads_last(kT_all), heads_last(vT_all), ikT_all.transpose(0, 1, 3, 2),
            stack(outs_p, 0), stack(outs_p, 1),
            stack(outs_s, 0), stack(outs_s, 1), stack(outs_s, 2), stack(outs_s, 3), stack(outs_s, 4))
```

```python
import functools
import math

import jax
import jax.numpy as jnp
from jax import lax
from jax.experimental import pallas as pl
from jax.experimental.pallas import tpu as pltpu

F32 = jnp.float32
BF16 = jnp.bfloat16
I32 = jnp.int32

N_HEADS = 8
HEAD_DIM = 64
ATT_W = N_HEADS * HEAD_DIM
IDX_HEADS = 8
IDX_DIM = 64
TOPK_MAX = 256
LRU_C = 8.0
MAX_DISTANCE = 128
EPS = 1e-6

LANES = 128
SUBLANES = 8
TQ = 128
TK = 256
NEG = -1e30
LOG2E = math.log2(math.e)
INT_MIN = -2147483648
VMEM_LIMIT = 56 * 1024 * 1024


def _dot(a, b):
    return jnp.dot(a, b, preferred_element_type=F32)


def _dot_nt(a, b):
    return lax.dot_general(a, b, (((1,), (1,)), ((), ())), preferred_element_type=F32)


def _sigmoid(x):
    return 0.5 * jnp.tanh(0.5 * x) + 0.5


def _gelu_tanh(x):
    c = math.sqrt(2.0 / math.pi)
    return 0.5 * x * (1.0 + jnp.tanh(c * (x + 0.044715 * (x * x * x))))


def _rms(x, g):
    y = x * lax.rsqrt(jnp.mean(x * x, axis=-1, keepdims=True) + EPS)
    return y * g


def _params(n_axes=1):
    return pltpu.CompilerParams(dimension_semantics=("arbitrary",) * n_axes,
                                vmem_limit_bytes=VMEM_LIMIT)


def _fori_pairs(lo, hi, body, carry):
    n = hi - lo

    def pair(i, c):
        j = lo + 2 * i
        return body(j + 1, body(j, c))

    carry = lax.fori_loop(0, n >> 1, pair, carry)
    return lax.cond((n & 1) == 1, lambda c: body(hi - 1, c), lambda c: c, carry)


def _const_spec(shape):
    nd = len(shape)
    return pl.BlockSpec(shape, lambda *_: (0,) * nd, pipeline_mode=pl.Buffered(1))


def _whole_out_spec(shape):
    nd = len(shape)
    return pl.BlockSpec(shape, lambda *_: (0,) * nd)


def _inproj_p_kernel(x_ref, g_ref, wm_ref, wt_ref, kT_all_ref, vT_all_ref, ikT_all_ref,
                     xr_ref, gr_ref, ga_ref, gb_ref, kb_ref, ikwb_ref,
                     qT_ref, iqT_ref, vTb_ref, iwT_ref, kT_ref, vT_ref, ikT_ref):
    del kT_all_ref, vT_all_ref, ikT_all_ref
    d = x_ref.shape[1]
    hb = _rms(x_ref[...], g_ref[...]).astype(BF16)
    z = _dot(hb, wm_ref[...])
    xr_ref[...] = z[:, 0:d]
    gr_ref[...] = z[:, d:2 * d]
    ga_ref[...] = z[:, 2 * d:3 * d]
    gb_ref[...] = z[:, 3 * d:4 * d]
    o = 4 * d
    kb_ref[...] = z[:, o:o + ATT_W].astype(BF16)
    ikwb_ref[...] = z[:, o + ATT_W:o + ATT_W + LANES].astype(BF16)
    zt = _dot_nt(wt_ref[...], hb)
    qT_ref[...] = (zt[0:ATT_W] * (HEAD_DIM ** -0.5)).astype(BF16)
    iqT_ref[...] = zt[ATT_W:2 * ATT_W].astype(BF16)
    kT_ref[...] = zt[2 * ATT_W:3 * ATT_W]
    vt = zt[3 * ATT_W:4 * ATT_W]
    vT_ref[...] = vt
    for c in range(vTb_ref.shape[0]):
        vTb_ref[c] = vt[:, c * TK:(c + 1) * TK].astype(BF16)
    ikT_ref[...] = zt[4 * ATT_W:4 * ATT_W + IDX_DIM]
    iwT_ref[...] = zt[4 * ATT_W + IDX_DIM:4 * ATT_W + IDX_DIM + IDX_HEADS]


def _inproj_p(x, g, wm, wt, caches, tm, nb, l):
    n, d = x.shape
    t = n // nb
    per_b = t // tm
    row = lambda w: pl.BlockSpec((tm, w), lambda i: (i, 0))
    out_shape = (
        jax.ShapeDtypeStruct((n, d), F32), jax.ShapeDtypeStruct((n, d), F32),
        jax.ShapeDtypeStruct((n, d), F32), jax.ShapeDtypeStruct((n, d), F32),
        jax.ShapeDtypeStruct((n, ATT_W), BF16), jax.ShapeDtypeStruct((n, LANES), BF16),
        jax.ShapeDtypeStruct((ATT_W, n), BF16), jax.ShapeDtypeStruct((ATT_W, n), BF16),
        jax.ShapeDtypeStruct((n // TK, ATT_W, TK), BF16),
        jax.ShapeDtypeStruct((IDX_HEADS, n), F32),
    ) + tuple(jax.ShapeDtypeStruct(c.shape, c.dtype) for c in caches)
    colT = lambda r: pl.BlockSpec((r, tm), lambda i: (0, i))
    layer_seq = lambda r: pl.BlockSpec((None, None, r, tm), lambda i: (l, i // per_b, 0, i % per_b))
    out_specs = (row(d), row(d), row(d), row(d), row(ATT_W), row(LANES), colT(ATT_W), colT(ATT_W),
                 pl.BlockSpec((tm // TK, ATT_W, TK), lambda i: (i, 0, 0)), colT(IDX_HEADS),
                 layer_seq(ATT_W), layer_seq(ATT_W), layer_seq(IDX_DIM))
    n_plain_in, n_plain_out = 4, 10
    return pl.pallas_call(
        _inproj_p_kernel, grid=(n // tm,),
        in_specs=[row(d), _const_spec(g.shape), _const_spec(wm.shape), _const_spec(wt.shape)]
                 + [pl.BlockSpec(memory_space=pl.ANY)] * len(caches),
        out_specs=out_specs, out_shape=out_shape,
        input_output_aliases={n_plain_in + i: n_plain_out + i for i in range(len(caches))},
        compiler_params=_params(1), name="inproj_prompt")(x, g, wm, wt, *caches)


def _inproj_s_kernel(x_ref, g_ref, wm_ref, wt_ref,
                     xr_ref, gr_ref, ga_ref, gb_ref, k_ref, v_ref, ikw_ref, q_ref, iq_ref):
    d = x_ref.shape[1]
    hb = _rms(x_ref[...], g_ref[...]).astype(BF16)
    z = _dot(hb, wm_ref[...])
    xr_ref[...] = z[:, 0:d]
    gr_ref[...] = z[:, d:2 * d]
    ga_ref[...] = z[:, 2 * d:3 * d]
    gb_ref[...] = z[:, 3 * d:4 * d]
    o = 4 * d
    k_ref[...] = z[:, o:o + ATT_W]
    ikw_ref[...] = z[:, o + ATT_W:o + ATT_W + LANES]
    qiq = _dot_nt(hb, wt_ref[0:2 * ATT_W, :])
    q_ref[...] = (qiq[:, 0:ATT_W] * (HEAD_DIM ** -0.5)).astype(BF16)
    iq_ref[...] = qiq[:, ATT_W:2 * ATT_W].astype(BF16)
    v_ref[...] = _dot_nt(hb, wt_ref[3 * ATT_W:4 * ATT_W, :])


def _inproj_s(x, g, wm, wt, tm):
    n, d = x.shape
    row = lambda w: pl.BlockSpec((tm, w), lambda i: (i, 0))
    out_shape = (
        jax.ShapeDtypeStruct((n, d), F32), jax.ShapeDtypeStruct((n, d), F32),
        jax.ShapeDtypeStruct((n, d), F32), jax.ShapeDtypeStruct((n, d), F32),
        jax.ShapeDtypeStruct((n, ATT_W), F32), jax.ShapeDtypeStruct((n, ATT_W), F32),
        jax.ShapeDtypeStruct((n, LANES), F32),
        jax.ShapeDtypeStruct((n, ATT_W), BF16), jax.ShapeDtypeStruct((n, ATT_W), BF16),
    )
    out_specs = (row(d), row(d), row(d), row(d), row(ATT_W), row(ATT_W), row(LANES),
                 row(ATT_W), row(ATT_W))
    return pl.pallas_call(
        _inproj_s_kernel, grid=(n // tm,),
        in_specs=[row(d), _const_spec(g.shape), _const_spec(wm.shape), _const_spec(wt.shape)],
        out_specs=out_specs, out_shape=out_shape, compiler_params=_params(1),
        name="inproj_sample")(x, g, wm, wt)


def _lru_gates(xc, wa_ref, wx_ref, ba, bx, lam):
    xcb = xc.astype(BF16)
    gw = wa_ref.shape[1]
    r_parts, i_parts = [], []
    for g in range(wa_ref.shape[0]):
        xs = xcb[:, g * gw:(g + 1) * gw]
        r_parts.append(_dot(xs, wa_ref[g]))
        i_parts.append(_dot(xs, wx_ref[g]))
    r = _sigmoid(jnp.concatenate(r_parts, axis=1) + ba)
    i = _sigmoid(jnp.concatenate(i_parts, axis=1) + bx)
    log_sig_lam = jnp.minimum(lam, 0.0) - jnp.log(1.0 + jnp.exp(-jnp.abs(lam)))
    log_a = LRU_C * r * log_sig_lam
    a = jnp.exp(log_a)
    u = jnp.sqrt(1.0 - a * a) * (i * xc)
    return a, u


def _rglru_p_kernel(xr_ref, gr_ref, cw_ref, cb_ref, wa_ref, wx_ref, ba_ref, bx_ref, lam_ref,
                    ya_ref, hlast_ref, buf_ref,
                    xx_ref, a_ref, u_ref, h_ref):
    step = pl.program_id(0)
    nb, tt, d = xr_ref.shape
    cw = cw_ref.shape[0]

    @pl.when(step == 0)
    def _():
        xx_ref[:, 0:8, :] = jnp.zeros((nb, 8, d), F32)
        h_ref[...] = jnp.zeros_like(h_ref)

    for b in range(nb):
        x = xr_ref[b]
        xx_ref[b, 8:8 + tt, :] = x
        xc = cb_ref[...] + x * cw_ref[cw - 1:cw, :]
        for j in range(cw - 1):
            sh = cw - 1 - j
            xc = xc + xx_ref[b, 8 - sh:8 - sh + tt, :] * cw_ref[j:j + 1, :]
        xx_ref[b, 0:8, :] = x[tt - 8:tt, :]
        a, u = _lru_gates(xc, wa_ref, wx_ref, ba_ref[...], bx_ref[...], lam_ref[...])
        a_ref[b] = a
        u_ref[b] = u

    def scan_body(t, hs):
        new = []
        for b in range(nb):
            h = a_ref[b, pl.ds(t, 1), :] * hs[b] + u_ref[b, pl.ds(t, 1), :]
            u_ref[b, pl.ds(t, 1), :] = h
            new.append(h)
        return tuple(new)

    hs = lax.fori_loop(0, tt, scan_body, tuple(h_ref[b:b + 1, :] for b in range(nb)), unroll=8)
    for b in range(nb):
        h_ref[b:b + 1, :] = hs[b]
        ya_ref[b] = (u_ref[b] * _gelu_tanh(gr_ref[b])).astype(BF16)
        buf_ref[b] = xr_ref[b, tt - (cw - 1):tt, :]
    hlast_ref[...] = h_ref[...]


def _rglru_p(xr, gr, cw, cb, wa, wx, ba, bx, lam, tt):
    nb, t, d = xr.shape
    blk = pl.BlockSpec((nb, tt, d), lambda i: (0, i, 0))
    consts = [cw, cb, wa, wx, ba, bx, lam]
    return pl.pallas_call(
        _rglru_p_kernel, grid=(t // tt,),
        in_specs=[blk, blk] + [_const_spec(c.shape) for c in consts],
        out_specs=(blk, _whole_out_spec((nb, d)), _whole_out_spec((nb, cw.shape[0] - 1, d))),
        out_shape=(jax.ShapeDtypeStruct((nb, t, d), BF16), jax.ShapeDtypeStruct((nb, d), F32),
                   jax.ShapeDtypeStruct((nb, cw.shape[0] - 1, d), F32)),
        scratch_shapes=[pltpu.VMEM((nb, tt + 8, d), F32), pltpu.VMEM((nb, tt, d), F32),
                        pltpu.VMEM((nb, tt, d), F32), pltpu.VMEM((nb, d), F32)],
        compiler_params=_params(1), name="rglru_prompt")(xr, gr, *consts)


def _rglru_s_kernel(xr_ref, gr_ref, st_ref, h0_ref, cw_ref, cb_ref, wa_ref, wx_ref, ba_ref,
                    bx_ref, lam_ref, ya_ref, hlast_ref, buf_ref):
    t_len, nb, d = xr_ref.shape
    cw = cw_ref.shape[0]
    rows = [st_ref[j] for j in range(cw - 1)] + [xr_ref[t] for t in range(t_len)]
    h = h0_ref[...]
    for t in range(t_len):
        xc = cb_ref[...] + rows[t + cw - 1] * cw_ref[cw - 1:cw, :]
        for j in range(cw - 1):
            xc = xc + rows[t + j] * cw_ref[j:j + 1, :]
        a, u = _lru_gates(xc, wa_ref, wx_ref, ba_ref[...], bx_ref[...], lam_ref[...])
        h = a * h + u
        ya_ref[t] = (h * _gelu_tanh(gr_ref[t])).astype(BF16)
    hlast_ref[...] = h
    for j in range(cw - 1):
        buf_ref[j] = rows[t_len + j]


def _rglru_s(xr, gr, st, h0, cw, cb, wa, wx, ba, bx, lam):
    t_len, nb, d = xr.shape
    args = [xr, gr, st, h0, cw, cb, wa, wx, ba, bx, lam]
    return pl.pallas_call(
        _rglru_s_kernel, grid=(1,),
        in_specs=[_const_spec(a.shape) for a in args],
        out_specs=(_whole_out_spec((t_len, nb, d)), _whole_out_spec((nb, d)),
                   _whole_out_spec((cw.shape[0] - 1, nb, d))),
        out_shape=(jax.ShapeDtypeStruct((t_len, nb, d), BF16), jax.ShapeDtypeStruct((nb, d), F32),
                   jax.ShapeDtypeStruct((cw.shape[0] - 1, nb, d), F32)),
        compiler_params=_params(1), name="rglru_sample")(*args)


def _sortable_key(score):
    bits = pltpu.bitcast(score, I32)
    return bits ^ ((bits >> 31) & 0x7FFFFFFF)


GROUP = 16
QT = 2


def _sort_network(n):
    pairs = []
    p = 1
    while p < n:
        k = p
        while k >= 1:
            for j in range(k % p, n - k, 2 * k):
                for i in range(min(k, n - j - k)):
                    if (i + j) // (2 * p) == (i + j + k) // (2 * p):
                        pairs.append((i + j, i + j + k))
            k //= 2
        p *= 2
    return pairs


def _count_in_sorted_group(v, cmp):
    one = lambda m, w: jnp.where(m, w, 0)
    pick = jnp.where
    m1 = cmp(v(7))
    m2 = cmp(pick(m1, v(11), v(3)))
    m3 = cmp(pick(m2, pick(m1, v(13), v(5)), pick(m1, v(9), v(1))))
    if_m3 = pick(m2, pick(m1, v(14), v(6)), pick(m1, v(10), v(2)))
    if_not = pick(m2, pick(m1, v(12), v(4)), pick(m1, v(8), v(0)))
    m4 = cmp(pick(m3, if_m3, if_not))
    m5 = cmp(v(15))
    return one(m1, 8) + one(m2, 4) + one(m3, 2) + one(m4, 1) + one(m5, 1)


def _attn_p_kernel(qT_ref, iqT_ref, iwT_ref, kb_ref, vT_ref, ikw_ref, dtab_ref, o_ref,
                   keys_ref, sorted_ref, rhs_ref, qbd_ref, m_ref, acc_ref, x_ref, *, k_top):
    qp = pl.program_id(1)
    n_chunks = qp + 1
    half = LANES // 2
    zeros_half = jnp.zeros((half, TQ), BF16)
    s_iota = lax.broadcasted_iota(I32, (TK, TQ), 0)
    t_iota = lax.broadcasted_iota(I32, (TK, TQ), 1)
    w_all = (iwT_ref[...] * (IDX_HEADS ** -0.5)) * (IDX_DIM ** -0.5)

    def tile_consts(qi):
        qb = QT * qp + qi
        lanes = slice(qi * TQ, (qi + 1) * TQ)
        n_far = jnp.maximum(qb - 1, 0) >> 1
        return qb * TQ, lanes, n_far

    def score_body(j, carry, band, qi, t0, lanes, w):
        off = pl.multiple_of(j * TK, TK)
        ikc = ikw_ref[pl.ds(off, TK), :]
        score = jnp.zeros((TK, TQ), F32)
        for c in range(N_HEADS // 2):
            dd = _dot(ikc, rhs_ref[qi, c])
            for hh in range(2):
                h = 2 * c + hh
                score = score + jnp.maximum(dd[:, hh * TQ:(hh + 1) * TQ], 0.0) * w[h:h + 1, :]
        key = _sortable_key(score)
        if band:
            key = jnp.where(s_iota + off <= t0 + t_iota, key, INT_MIN)
        keys_ref[pl.ds(off, TK), lanes] = key
        per = TK // GROUP
        v = [key[i * per:(i + 1) * per, :] for i in range(GROUP)]
        for a, b in _sort_network(GROUP):
            v[a], v[b] = jnp.maximum(v[a], v[b]), jnp.minimum(v[a], v[b])
        for i in range(GROUP):
            sorted_ref[j, i, :, lanes] = v[i]
        return carry

    for qi in range(QT):
        t0, lanes, n_far = tile_consts(qi)
        for c in range(N_HEADS // 2):
            for hh in range(2):
                h = 2 * c + hh
                iq_h = iqT_ref[h * IDX_DIM:(h + 1) * IDX_DIM, lanes]
                rhs_ref[qi, c, :, hh * TQ:(hh + 1) * TQ] = jnp.concatenate([iq_h, zeros_half], axis=0)
                q_h = qT_ref[h * HEAD_DIM:(h + 1) * HEAD_DIM, lanes]
                parts = [zeros_half, q_h] if hh else [q_h, zeros_half]
                qbd_ref[qi, c, :, hh * TQ:(hh + 1) * TQ] = jnp.concatenate(parts, axis=0)

    def score_tiles(j, carry, band):
        for qi in range(QT):
            t0, lanes, _ = tile_consts(qi)
            score_body(j, carry, band, qi, t0, lanes, w_all[:, lanes])
        return carry

    n_far_all = tile_consts(0)[2]
    _fori_pairs(0, n_far_all, functools.partial(score_tiles, band=False), 0)
    _fori_pairs(n_far_all, n_chunks, functools.partial(score_tiles, band=True), 0)

    @pl.when((n_chunks & 1) == 1)
    def _():
        sorted_ref[n_chunks] = jnp.full(sorted_ref.shape[1:], INT_MIN, I32)

    def count(cmp):
        def body(i, acc):
            for u in range(2):
                acc = acc + _count_in_sorted_group(lambda r: sorted_ref[2 * i + u, r], cmp)
            return acc
        acc = lax.fori_loop(0, (n_chunks + 1) >> 1, body, jnp.zeros((TK // GROUP, QT * TQ), I32))
        return acc.sum(axis=0, keepdims=True)

    def bit_body(p, carry):
        thr, n_ge = carry
        bit = jnp.left_shift(jnp.int32(1), 31 - p)
        cand = thr ^ bit
        cnt = count(lambda kc: kc >= cand)
        ok = cnt >= k_top
        return jnp.where(ok, cand, thr), jnp.where(ok, cnt, n_ge)

    thr_all, n_ge = lax.fori_loop(0, 32, bit_body, (jnp.full((1, QT * TQ), INT_MIN, I32),
                                                    jnp.zeros((1, QT * TQ), I32)))
    any_ties = jnp.max(jnp.where((thr_all != INT_MIN) & (n_ge != k_top), 1, 0))

    fold = lambda a, op: op(a.reshape(TK // SUBLANES, SUBLANES, TQ), axis=0)

    def logits_body(j, carry, band, ties, qi, t0, lanes, thr, need=None, ltri=None):
        taken, mx = carry
        off = pl.multiple_of(j * TK, TK)
        keyc = keys_ref[pl.ds(off, TK), lanes]
        if ties:
            eq = keyc == thr
            prefix = _dot(ltri, eq.astype(BF16))
            sel = (keyc > thr) | (eq & (prefix + taken <= need))
            taken = taken + prefix[TK - 1:TK, :]
        else:
            sel = keyc >= thr
        if band:
            sel = sel & (s_iota + off <= t0 + t_iota)
            tab = (t0 - off) // TQ
        kc = kb_ref[pl.ds(off, TK), :]
        new_mx = []
        for c in range(N_HEADS // 2):
            lg = _dot(kc[:, c * LANES:(c + 1) * LANES], qbd_ref[qi, c])
            for hh in range(2):
                h = 2 * c + hh
                x = lg[:, hh * TQ:(hh + 1) * TQ]
                if band:
                    x = x + dtab_ref[tab, h]
                x = jnp.where(sel, x * LOG2E, NEG)
                x_ref[h, pl.ds(off, TK), :] = x
                new_mx.append(jnp.maximum(mx[h * SUBLANES:(h + 1) * SUBLANES, :], fold(x, jnp.max)))
        return taken, jnp.concatenate(new_mx, axis=0)

    def all_logits(ties, qi, t0, lanes, n_far):
        thr = thr_all[:, lanes]
        extra = {}
        if ties:
            n_gt = count(lambda kc: kc > thr_all)
            extra["need"] = (k_top - n_gt[:, lanes]).astype(F32)
            extra["ltri"] = (lax.broadcasted_iota(I32, (TK, TK), 1)
                             <= lax.broadcasted_iota(I32, (TK, TK), 0)).astype(BF16)
        body = functools.partial(logits_body, ties=ties, qi=qi, t0=t0, lanes=lanes, thr=thr, **extra)
        carry = (jnp.zeros((1, TQ), F32), jnp.full((N_HEADS * SUBLANES, TQ), NEG, F32))
        carry = _fori_pairs(0, n_far, functools.partial(body, band=False), carry)
        return _fori_pairs(n_far, n_chunks, functools.partial(body, band=True), carry)[1]

    def pv_body(j, lsum):
        off = pl.multiple_of(j * TK, TK)
        new_lsum = []
        for h in range(N_HEADS):
            p = jnp.exp2(x_ref[h, pl.ds(off, TK), :] - m_ref[h:h + 1, :])
            new_lsum.append(lsum[h * SUBLANES:(h + 1) * SUBLANES, :] + fold(p, jnp.sum))
            rows = slice(h * HEAD_DIM, (h + 1) * HEAD_DIM)
            acc_ref[rows, :] += _dot(vT_ref[j, rows, :], p.astype(BF16))
        return jnp.concatenate(new_lsum, axis=0)

    zero_sums = jnp.zeros((N_HEADS * SUBLANES, TQ), F32)

    def start_pv(mx):
        m_ref[...] = jnp.concatenate(
            [mx[h * SUBLANES:(h + 1) * SUBLANES, :].max(axis=0, keepdims=True)
             for h in range(N_HEADS)], axis=0)
        acc_ref[...] = jnp.zeros(acc_ref.shape, F32)

    def finish_tile(qi, lsum):
        for h in range(N_HEADS):
            rows = slice(h * HEAD_DIM, (h + 1) * HEAD_DIM)
            inv = 1.0 / lsum[h * SUBLANES:(h + 1) * SUBLANES, :].sum(axis=0, keepdims=True)
            acc_ref[rows, :] = acc_ref[rows, :] * inv
        o_ref[qi * TQ:(qi + 1) * TQ, :] = acc_ref[...].T.astype(BF16)

    def tiles_in_turn():
        for qi in range(QT):
            t0, lanes, n_far = tile_consts(qi)
            start_pv(all_logits(True, qi, t0, lanes, n_far))
            finish_tile(qi, _fori_pairs(0, n_chunks, pv_body, zero_sums))
        return 0

    def tiles_overlapped():
        assert QT == 2
        t0_a, lanes_a, n_far_a = tile_consts(0)
        t0_b, lanes_b, n_far_b = tile_consts(1)
        start_pv(all_logits(False, 0, t0_a, lanes_a, n_far_a))
        logits_b = functools.partial(logits_body, ties=False, qi=1, t0=t0_b, lanes=lanes_b,
                                     thr=thr_all[:, lanes_b])

        def both(j, carry, band):
            lsum_a, state_b = carry
            lsum_a = pv_body(j, lsum_a)
            return lsum_a, logits_b(j, state_b, band=band)

        carry = (zero_sums, (jnp.zeros((1, TQ), F32), jnp.full((N_HEADS * SUBLANES, TQ), NEG, F32)))
        carry = _fori_pairs(0, n_far_b, functools.partial(both, band=False), carry)
        lsum_a, (_, mx_b) = _fori_pairs(n_far_b, n_chunks, functools.partial(both, band=True), carry)
        finish_tile(0, lsum_a)
        start_pv(mx_b)
        finish_tile(1, _fori_pairs(0, n_chunks, pv_body, zero_sums))
        return 0

    lax.cond(any_ties > 0, tiles_in_turn, tiles_overlapped)


def _attn_p(qT, iqT, iwT, kb, vT, ikwb, dtab, nb, t):
    n = nb * t
    nq = t // (QT * TQ)
    k_top = min(TOPK_MAX, t // 4)
    colT = lambda r: pl.BlockSpec((r, QT * TQ), lambda b, q: (0, b * nq + q))
    return pl.pallas_call(
        functools.partial(_attn_p_kernel, k_top=k_top), grid=(nb, nq),
        in_specs=[colT(ATT_W), colT(ATT_W), colT(IDX_HEADS),
                  pl.BlockSpec((t, ATT_W), lambda b, q: (b, 0)),
                  pl.BlockSpec((t // TK, ATT_W, TK), lambda b, q: (b, 0, 0)),
                  pl.BlockSpec((t, LANES), lambda b, q: (b, 0)),
                  _const_spec(dtab.shape)],
        out_specs=pl.BlockSpec((QT * TQ, ATT_W), lambda b, q: (b * nq + q, 0)),
        out_shape=jax.ShapeDtypeStruct((n, ATT_W), BF16),
        scratch_shapes=[pltpu.VMEM((t, QT * TQ), I32),
                        pltpu.VMEM((t // TK + 1, GROUP, TK // GROUP, QT * TQ), I32),
                        pltpu.VMEM((QT, N_HEADS // 2, LANES, 2 * TQ), BF16),
                        pltpu.VMEM((QT, N_HEADS // 2, LANES, 2 * TQ), BF16),
                        pltpu.VMEM((N_HEADS, TQ), F32), pltpu.VMEM((ATT_W, TQ), F32),
                        pltpu.VMEM((N_HEADS, t, TQ), F32)],
        compiler_params=_params(2), name="attn_prompt")(qT, iqT, iwT, kb, vT, ikwb, dtab)


def _select_s_kernel(pt_ref, iq_ref, wcol_ref, iknew_ref, pool_ref, mask_ref,
                     ikbuf_ref, score_ref, sem, *, k_top, layer_off, t_len):
    g = pl.program_id(0)
    gs, _, s_pad = ikbuf_ref.shape
    n_pages = pt_ref.shape[1]
    page = pool_ref.shape[2]
    past = n_pages * page

    def page_copy(i, j):
        return pltpu.make_async_copy(pool_ref.at[layer_off + pt_ref[g * gs + i, j]],
                                     ikbuf_ref.at[i, :, j * page:(j + 1) * page], sem)

    def start_body(i, c):
        for j in range(n_pages):
            page_copy(i, j).start(priority=j % 2)
        return c

    lax.fori_loop(0, gs, start_body, 0)
    ikbuf_ref[:, :, past:s_pad] = jnp.zeros((gs, IDX_DIM, s_pad - past), F32)
    ikbuf_ref[:, :, past:past + t_len] = iknew_ref[...]

    def wait_body(i, c):
        for j in range(n_pages):
            page_copy(i, j).wait()
        return c

    lax.fori_loop(0, gs, wait_body, 0)

    tile = score_ref.shape[1]
    per_tile = tile // t_len

    def score_body(it, c):
        for u in range(per_tile):
            i = it * per_tile + u
            dd = _dot(iq_ref[i], ikbuf_ref[i].astype(BF16))
            wv = (wcol_ref[i] * (IDX_HEADS ** -0.5)) * (IDX_DIM ** -0.5)
            sc = (jnp.maximum(dd, 0.0) * wv).reshape(t_len, IDX_HEADS, s_pad).sum(axis=1)
            score_ref[it, u * t_len:(u + 1) * t_len, :] = sc
        return c

    lax.fori_loop(0, gs // per_tile, score_body, 0)

    rows = gs * t_len
    s_idx = lax.broadcasted_iota(I32, (rows, s_pad), 1)
    t_idx = lax.broadcasted_iota(I32, (rows, s_pad), 0) % t_len
    valid = s_idx <= past + t_idx
    keys = jnp.where(valid, _sortable_key(score_ref[...].reshape(rows, s_pad)), INT_MIN)

    def bit_body(p, thr):
        bit = jnp.left_shift(jnp.int32(1), 31 - p)
        cand = thr ^ bit
        cnt = jnp.sum((keys >= cand).astype(I32), axis=1, keepdims=True)
        return jnp.where(cnt >= k_top, cand, thr)

    thr = lax.fori_loop(0, 32, bit_body, jnp.full((rows, 1), INT_MIN, I32))
    gt = keys > thr
    eq = keys == thr
    need = (k_top - jnp.sum(gt.astype(I32), axis=1, keepdims=True)).astype(F32)
    utri = (lax.broadcasted_iota(I32, (LANES, LANES), 0) <= lax.broadcasted_iota(I32, (LANES, LANES), 1)
            ).astype(BF16)
    eqb = eq.astype(BF16)
    carry = jnp.zeros((rows, 1), F32)
    for c in range(s_pad // LANES):
        cols = slice(c * LANES, (c + 1) * LANES)
        prefix = _dot(eqb[:, cols], utri) + carry
        take = gt[:, cols] | (eq[:, cols] & (prefix <= need))
        take = take & valid[:, cols]
        m = jnp.where(take, 0.0, NEG).reshape(gs // per_tile, tile, LANES)
        for u in range(per_tile):
            mask_ref[:, u, :, cols] = m[:, u * t_len:(u + 1) * t_len, :]
        carry = prefix[:, LANES - 1:LANES]


def _select_s(page_table, iq, wcol, iknew, pool, layer, gs, k_top):
    nb, rows_q, _ = iq.shape
    t_len = rows_q // IDX_HEADS
    n_pages = page_table.shape[1]
    page = pool.shape[2]
    s_pad = n_pages * page + LANES
    assert SUBLANES % t_len == 0
    per_tile = SUBLANES // t_len
    assert nb % gs == 0 and gs % per_tile == 0
    kern = functools.partial(_select_s_kernel, k_top=k_top, layer_off=layer, t_len=t_len)
    grid_spec = pltpu.PrefetchScalarGridSpec(
        num_scalar_prefetch=1, grid=(nb // gs,),
        in_specs=[pl.BlockSpec((gs, rows_q, IDX_DIM), lambda g, pt: (g, 0, 0)),
                  pl.BlockSpec((gs, rows_q, 1), lambda g, pt: (g, 0, 0)),
                  pl.BlockSpec((gs, IDX_DIM, t_len), lambda g, pt: (g, 0, 0)),
                  pl.BlockSpec(memory_space=pl.ANY)],
        out_specs=pl.BlockSpec((gs // per_tile, per_tile, t_len, s_pad), lambda g, pt: (g, 0, 0, 0)),
        scratch_shapes=[pltpu.VMEM((gs, IDX_DIM, s_pad), F32),
                        pltpu.VMEM((gs // per_tile, SUBLANES, s_pad), F32),
                        pltpu.SemaphoreType.DMA(())])
    return pl.pallas_call(
        kern, grid_spec=grid_spec,
        out_shape=jax.ShapeDtypeStruct((nb // per_tile, per_tile, t_len, s_pad), F32),
        compiler_params=_params(1), name="select_sample")(page_table, iq, wcol, iknew, pool)


def _attn_s_kernel(pt_ref, q_ref, knew_ref, vnew_ref, mask_ref, dtab_ref, *rest, n_pages, t_len):
    k_pages = rest[:n_pages]
    v_pages = rest[n_pages:2 * n_pages]
    o_ref, kpad_ref, vpad_ref = rest[2 * n_pages:]
    rows = t_len * N_HEADS
    page = k_pages[0].shape[2]

    def flat_bf16(page_ref):
        return page_ref[...].reshape(ATT_W, page).astype(BF16)

    def per_head_rows(x):
        return jnp.concatenate(
            [jnp.broadcast_to(x[t:t + 1, :], (N_HEADS, x.shape[1])) for t in range(t_len)], axis=0)

    col_head = lax.broadcasted_iota(I32, (rows, ATT_W), 1) // HEAD_DIM
    row_head = lax.broadcasted_iota(I32, (rows, ATT_W), 0) % N_HEADS
    own = col_head == row_head
    qbd = jnp.where(own, per_head_rows(q_ref[...].astype(F32)), 0.0).astype(BF16)
    kpad_ref[...] = jnp.zeros(kpad_ref.shape, F32)
    vpad_ref[...] = jnp.zeros(vpad_ref.shape, F32)
    kpad_ref[0:t_len, :] = knew_ref[...]
    vpad_ref[0:t_len, :] = vnew_ref[...]
    knew = kpad_ref[...].astype(BF16)
    vnew = vpad_ref[...].astype(BF16)
    logits = [_dot(qbd, flat_bf16(kp)) for kp in k_pages] + [_dot_nt(qbd, knew)]
    x = jnp.concatenate(logits, axis=1)
    x = x + dtab_ref[...] + per_head_rows(mask_ref[...])
    m = x.max(axis=1, keepdims=True)
    p = jnp.exp(x - m)
    l = p.sum(axis=1, keepdims=True)
    pb = p.astype(BF16)
    out = _dot(pb[:, n_pages * page:], vnew)
    for j, vp in enumerate(v_pages):
        out = out + _dot_nt(pb[:, j * page:(j + 1) * page], flat_bf16(vp))
    out = out / l
    out = jnp.where(own, out, 0.0).reshape(t_len, N_HEADS, ATT_W).sum(axis=1)
    o_ref[...] = out.astype(BF16)


def _attn_s(page_table, q, knew, vnew, mask, dtab, k_pool, v_pool, layer_off):
    nb, t_len, _ = q.shape
    n_pages = page_table.shape[1]
    page = k_pool.shape[3]
    s_pad = n_pages * page + LANES

    def page_spec(j):
        return pl.BlockSpec((None, N_HEADS, HEAD_DIM, page),
                            lambda b, pt, j=j: (layer_off + pt[b, j], 0, 0, 0))

    seq = lambda w: pl.BlockSpec((None, t_len, w), lambda b, pt: (b, 0, 0))
    per_tile = mask.shape[1]
    mask_spec = pl.BlockSpec((None, None, t_len, s_pad),
                             lambda b, pt: (b // per_tile, b % per_tile, 0, 0))
    grid_spec = pltpu.PrefetchScalarGridSpec(
        num_scalar_prefetch=1, grid=(nb,),
        in_specs=[seq(ATT_W), seq(ATT_W), seq(ATT_W), mask_spec,
                  pl.BlockSpec(dtab.shape, lambda b, pt: (0, 0))]
                 + [page_spec(j) for j in range(n_pages)] * 2,
        out_specs=seq(ATT_W),
        scratch_shapes=[pltpu.VMEM((page, ATT_W), F32), pltpu.VMEM((page, ATT_W), F32)])
    kern = functools.partial(_attn_s_kernel, n_pages=n_pages, t_len=t_len)
    return pl.pallas_call(
        kern, grid_spec=grid_spec, out_shape=jax.ShapeDtypeStruct((nb, t_len, ATT_W), BF16),
        compiler_params=_params(1), name="attn_sample")(
            page_table, q, knew, vnew, mask, dtab, *([k_pool] * n_pages), *([v_pool] * n_pages))


def _post_kernel(x_ref, ya_ref, o_ref, ga_ref, gb_ref, p_ref, wr_ref, wa_ref, wo_ref,
                 nf_ref, wg_ref, wu_ref, wd_ref, np_ref, wpg_ref, wpp_ref, nfin_ref,
                 out_ref, *, final):
    a = _dot(ya_ref[...], wr_ref[...])
    b = _dot(o_ref[...], wa_ref[...])
    mix = _sigmoid(ga_ref[...]) * a + _sigmoid(gb_ref[...]) * b
    x = x_ref[...] + _dot(mix.astype(BF16), wo_ref[...])
    h2 = _rms(x, nf_ref[...]).astype(BF16)
    g = _dot(h2, wg_ref[...])
    u = _dot(h2, wu_ref[...])
    act = (g * _sigmoid(g)) * u
    x = x + _dot(act.astype(BF16), wd_ref[...])
    h3 = _rms(x, np_ref[...]).astype(BF16)
    x = x + _sigmoid(_dot(h3, wpg_ref[...])) * _dot(p_ref[...].astype(BF16), wpp_ref[...])
    if final:
        x = _rms(x, nfin_ref[...])
    out_ref[...] = x


def _layer_spec(w, l):
    tail = (0,) * (w.ndim - 1)
    return pl.BlockSpec((None,) + w.shape[1:], lambda *_: (l,) + tail,
                        pipeline_mode=pl.Buffered(1))


def _post(x, ya, o, ga, gb, p, weights, tm, final, l, p_row0=0):
    n, d = x.shape
    row = lambda w: pl.BlockSpec((tm, w), lambda i: (i, 0))
    p_blk0 = p_row0 // tm
    return pl.pallas_call(
        functools.partial(_post_kernel, final=final), grid=(n // tm,),
        in_specs=[row(d), row(d), row(ATT_W), row(d), row(d),
                  pl.BlockSpec((tm, p.shape[1]), lambda i: (i + p_blk0, 0))]
                 + [_layer_spec(c, l) if c.ndim == 3 else _const_spec(c.shape) for c in weights],
        out_specs=row(d), out_shape=jax.ShapeDtypeStruct((n, d), F32),
        compiler_params=_params(1), name="post")(x, ya, o, ga, gb, p, *weights)


def _rel_bucket(n, n_buckets):
    max_exact = n_buckets // 2
    nf = jnp.maximum(n, 1).astype(F32)
    large = max_exact + (jnp.log(nf / max_exact) / math.log(MAX_DISTANCE / max_exact)
                         * (n_buckets - max_exact)).astype(I32)
    large = jnp.minimum(large, n_buckets - 1)
    return jnp.where(n < max_exact, n, large)


def _bias_delta(rel_bias, n):
    nbk = rel_bias.shape[0]
    b = rel_bias.astype(F32)
    onehot = _rel_bucket(jnp.maximum(n, 0), nbk)[..., None] == jnp.arange(nbk, dtype=I32)
    picked = jnp.where(onehot[..., None], b, 0.0).sum(axis=-2)
    return picked - b[nbk - 1]


def _block_diag_groups(w, group):
    nbk, c, _ = w.shape
    per = group // c
    wg = w.reshape(nbk // per, per, c, c)
    eye = jnp.eye(per, dtype=w.dtype)
    return jnp.einsum('gpcd,pq->gpcqd', wg, eye).reshape(nbk // per, group, group)


def kernel(x_prompt, x_sample, p_prompt, p_sample, cache_k, cache_v, cache_idx_k, state_rglru_h,
           state_conv, page_table, rel_bias, norm_mix, w_in, conv_w, conv_b, w_rg_a, b_rg_a,
           w_rg_x, b_rg_x, lru_lambda, w_rnn_out, w_att_out, w_o, norm_ffn, w_ffn_gate, w_ffn_up,
           w_ffn_down, norm_ple, w_ple_gate, w_ple_proj, norm_final):
    nb, t, d = x_prompt.shape
    sb, st, _ = x_sample.shape
    depth = w_in.shape[0]
    n_pool, page = cache_k.shape[1], cache_k.shape[2]
    n_pages = page_table.shape[1]
    past = n_pages * page
    cw = conv_w.shape[1]
    s_pad = past + LANES
    k_top_s = min(TOPK_MAX, (past + st) // 4)
    row2 = lambda v: v.reshape(1, -1).astype(F32)

    idx3 = jnp.arange(3, dtype=I32)[:, None, None]
    s_rel = jnp.arange(TK, dtype=I32)[None, :, None]
    t_rel = jnp.arange(TQ, dtype=I32)[None, None, :]
    dtab_p = _bias_delta(rel_bias, idx3 * TQ + t_rel - s_rel).transpose(0, 3, 1, 2)
    s_all = jnp.arange(s_pad, dtype=I32)[None, :]
    q_pos = past + jnp.arange(st, dtype=I32)[:, None]
    dtab_s = _bias_delta(rel_bias, q_pos - s_all)
    dtab_s = dtab_s.transpose(0, 2, 1).reshape(st * N_HEADS, s_pad)

    k_pool = cache_k.transpose(0, 1, 3, 4, 2).reshape(depth * n_pool, N_HEADS, HEAD_DIM, page)
    v_pool = cache_v.transpose(0, 1, 3, 4, 2).reshape(depth * n_pool, N_HEADS, HEAD_DIM, page)
    ik_pool = cache_idx_k.transpose(0, 1, 3, 2).reshape(depth * n_pool, IDX_DIM, page)

    w_r, w_a, w_oo = w_rnn_out.astype(BF16), w_att_out.astype(BF16), w_o.astype(BF16)
    w_fg, w_fu, w_fd = w_ffn_gate.astype(BF16), w_ffn_up.astype(BF16), w_ffn_down.astype(BF16)
    w_pg, w_pp = w_ple_gate.astype(BF16), w_ple_proj.astype(BF16)
    p_rows = p_prompt.reshape(depth * nb * t, -1)

    xp = x_prompt.reshape(nb * t, d)
    xs = x_sample.transpose(1, 0, 2).reshape(st * sb, d)
    outs_p, outs_s = [], []
    caches = (jnp.zeros((depth, nb, ATT_W, t), F32), jnp.zeros((depth, nb, ATT_W, t), F32),
              jnp.zeros((depth, nb, IDX_DIM, t), F32))
    for l in range(depth):
        splits = [d, 2 * d, 2 * d + ATT_W, 2 * d + 2 * ATT_W, 2 * d + 3 * ATT_W,
                  2 * d + 3 * ATT_W + IDX_HEADS * IDX_DIM,
                  2 * d + 3 * ATT_W + IDX_HEADS * IDX_DIM + IDX_DIM,
                  2 * d + 3 * ATT_W + IDX_HEADS * IDX_DIM + IDX_DIM + IDX_HEADS,
                  3 * d + 3 * ATT_W + IDX_HEADS * IDX_DIM + IDX_DIM + IDX_HEADS]
        w_xr, w_gr, w_q, w_k, w_v, w_iq, w_ik, w_iw, w_ga, w_gb = jnp.split(w_in[l], splits, axis=1)
        w_ikw = jnp.concatenate(
            [w_ik, w_iw, jnp.zeros((d, LANES - IDX_DIM - IDX_HEADS), F32)], axis=1)
        wm_p = jnp.concatenate([w_xr, w_gr, w_ga, w_gb, w_k, w_ikw], axis=1).astype(BF16)
        wt_p = jnp.concatenate([w_q, w_iq, w_k, w_v, w_ik, w_iw], axis=1).T.astype(BF16)
        gw = 2 * LANES
        wa_bd = _block_diag_groups(w_rg_a[l], gw).astype(BF16)
        wx_bd = _block_diag_groups(w_rg_x[l], gw).astype(BF16)
        lru_consts = (conv_w[l], row2(conv_b[l]), wa_bd, wx_bd, row2(b_rg_a[l]), row2(b_rg_x[l]),
                      row2(lru_lambda[l]))
        post_w = (w_r, w_a, w_oo, row2(norm_ffn[l]), w_fg, w_fu, w_fd, row2(norm_ple[l]), w_pg, w_pp,
                  row2(norm_final))
        final = l == depth - 1

        (xr, gr, ga, gb, kb, ikwb, qT, iqT, vTb, iwT, *caches) = _inproj_p(
            xp, row2(norm_mix[l]), wm_p, wt_p, caches, tm=256, nb=nb, l=l)
        ya, h_last, new_buf = _rglru_p(xr.reshape(nb, t, d), gr.reshape(nb, t, d), *lru_consts, tt=128)
        o = _attn_p(qT, iqT, iwT, kb, vTb, ikwb, dtab_p, nb, t)
        xp = _post(xp, ya.reshape(nb * t, d), o, ga, gb, p_rows, post_w, tm=256, final=final, l=l,
                   p_row0=l * nb * t)
        outs_p.append((h_last, new_buf))

        (xr, gr, ga, gb, k, v, ikw, q, iq) = _inproj_s(xs, row2(norm_mix[l]), wm_p, wt_p, tm=st * sb)
        tmaj = lambda a2: a2.reshape(st, sb, -1)
        ya, h_last, new_buf = _rglru_s(tmaj(xr), tmaj(gr), state_conv[l].transpose(1, 0, 2),
                                       state_rglru_h[l], *lru_consts)
        bmaj = lambda a2: a2.reshape(st, sb, -1).transpose(1, 0, 2)
        k_b, v_b, ikw_b, q_b, iq_b = bmaj(k), bmaj(v), bmaj(ikw), bmaj(q), bmaj(iq)
        iq_rows = iq_b.reshape(sb, st * IDX_HEADS, IDX_DIM)
        w_col = ikw_b[:, :, IDX_DIM:IDX_DIM + IDX_HEADS].reshape(sb, st * IDX_HEADS, 1)
        mask = _select_s(page_table, iq_rows, w_col, ikw_b[:, :, :IDX_DIM].transpose(0, 2, 1),
                         ik_pool, l * n_pool, gs=min(sb, 32), k_top=k_top_s)
        o = _attn_s(page_table, q_b, k_b, v_b, mask, dtab_s, k_pool, v_pool, l * n_pool)
        o = o.transpose(1, 0, 2).reshape(st * sb, ATT_W)
        xs = _post(xs, ya.reshape(st * sb, d), o, ga, gb,
                   p_sample[l].transpose(1, 0, 2).reshape(st * sb, -1), post_w,
                   tm=min(256, st * sb), final=final, l=l)
        outs_s.append((k_b.reshape(sb, st, N_HEADS, HEAD_DIM), v_b.reshape(sb, st, N_HEADS, HEAD_DIM),
                       ikw_b[:, :, :IDX_DIM], h_last, new_buf.transpose(1, 0, 2)))

    y_prompt = xp.reshape(nb, t, d)
    y_sample = xs.reshape(st, sb, d).transpose(1, 0, 2)
    stack = lambda outs, i: jnp.stack([o_[i] for o_ in outs])
    kT_all, vT_all, ikT_all = caches
    heads_last = lambda aT: aT.reshape(depth, nb, N_HEADS, HEAD_DIM, t).transpose(0, 1, 4, 2, 3)
    return (y_prompt, y_sample,
            heads_last(kT_all), heads_last(vT_all), ikT_all.transpose(0, 1, 3, 2),
            stack(outs_p, 0), stack(outs_p, 1),
            stack(outs_s, 0), stack(outs_s, 1), stack(outs_s, 2), stack(outs_s, 3), stack(outs_s, 4))
```
